```python
import jax, jax.numpy as jnp
from jax import lax
import numpy as np

D_MODEL = 1024
BATCH = 16
SEQ = 4096
DEPTH = 4

HEAD_DIM = 64
H_GROUP = D_MODEL // 128
W_GROUP = H_GROUP * HEAD_DIM
N_GROUPS = 3
D_MIX = N_GROUPS * W_GROUP
D_IN = N_GROUPS * 4 * W_GROUP + H_GROUP
CHUNK = 64
N_LEFT_CHUNKS = 8
BAND = (N_LEFT_CHUNKS + 1) * CHUNK
REL_CLIP = 128
Q_BLOCK = 128
EPS = 1e-6

kernel_name = "hymba_style_sb_chunk_fox_hybrid"


def rmsnorm(x, g):
    xf = x.astype(jnp.float32)
    y = xf * lax.rsqrt(jnp.mean(xf * xf, axis=-1, keepdims=True) + EPS) * g.astype(jnp.float32)
    return y.astype(x.dtype)


def split_heads(t):
    b, s, _ = t.shape
    return t.reshape(b, s, H_GROUP, HEAD_DIM).transpose(0, 2, 1, 3)


def merge_heads(t):
    b, h, s, d = t.shape
    return t.transpose(0, 2, 1, 3).reshape(b, s, h * d)


def stick_breaking_attention(q, k, v):
    b, h, s, d = q.shape
    nb = s // Q_BLOCK
    scale = d ** -0.5
    qb = q.reshape(b, h, nb, Q_BLOCK, d).transpose(2, 0, 1, 3, 4)
    s_pos = jnp.arange(s)

    def block(args):
        qi, i = args
        t_pos = i * Q_BLOCK + jnp.arange(Q_BLOCK)
        z = jnp.einsum('bhqd,bhkd->bhqk', qi, k).astype(jnp.float32) * scale
        causal = s_pos[None, :] < t_pos[:, None]
        log_1m = jnp.where(causal, jax.nn.log_sigmoid(-z), 0.0)
        after = lax.cumsum(log_1m, axis=3, reverse=True) - log_1m
        w = jnp.where(causal, jnp.exp(jax.nn.log_sigmoid(z) + after), 0.0)
        return jnp.einsum('bhqk,bhkd->bhqd', w.astype(v.dtype), v)

    o = lax.map(block, (qb, jnp.arange(nb)))
    return o.transpose(1, 2, 0, 3, 4).reshape(b, h, s, d)


def chunked_relpos_attention(q, k, v, rel_bias):
    b, h, s, d = q.shape
    nc = s // CHUNK
    pad = N_LEFT_CHUNKS * CHUNK
    scale = d ** -0.5
    kp = jnp.pad(k, ((0, 0), (0, 0), (pad, 0), (0, 0)))
    vp = jnp.pad(v, ((0, 0), (0, 0), (pad, 0), (0, 0)))
    qc = q.reshape(b, h, nc, CHUNK, d).transpose(2, 0, 1, 3, 4)
    i_pos = jnp.arange(CHUNK)
    j_pos = jnp.arange(BAND)
    rel = i_pos[:, None] - j_pos[None, :] + pad
    bias = rel_bias[:, jnp.clip(rel, -REL_CLIP, REL_CLIP) + REL_CLIP].astype(jnp.float32)

    def block(args):
        qi, n = args
        kb = lax.dynamic_slice_in_dim(kp, n * CHUNK, BAND, axis=2)
        vb = lax.dynamic_slice_in_dim(vp, n * CHUNK, BAND, axis=2)
        z = jnp.einsum('bhqd,bhkd->bhqk', qi, kb).astype(jnp.float32) * scale + bias
        valid = (n * CHUNK - pad + j_pos) >= 0
        p = jax.nn.softmax(jnp.where(valid, z, -jnp.inf), axis=-1)
        return jnp.einsum('bhqk,bhkd->bhqd', p.astype(vb.dtype), vb)

    o = lax.map(block, (qc, jnp.arange(nc)))
    return o.transpose(1, 2, 0, 3, 4).reshape(b, h, s, d)


def forgetting_attention(q, k, v, log_f):
    b, h, s, d = q.shape
    nb = s // Q_BLOCK
    scale = d ** -0.5
    cum = lax.cumsum(log_f, axis=2)
    qb = q.reshape(b, h, nb, Q_BLOCK, d).transpose(2, 0, 1, 3, 4)
    cb = cum.reshape(b, h, nb, Q_BLOCK).transpose(2, 0, 1, 3)
    s_pos = jnp.arange(s)

    def block(args):
        qi, ci, i = args
        t_pos = i * Q_BLOCK + jnp.arange(Q_BLOCK)
        z = (jnp.einsum('bhqd,bhkd->bhqk', qi, k).astype(jnp.float32) * scale
             + ci[..., :, None] - cum[..., None, :])
        causal = s_pos[None, :] <= t_pos[:, None]
        p = jax.nn.softmax(jnp.where(causal, z, -jnp.inf), axis=-1)
        return jnp.einsum('bhqk,bhkd->bhqd', p.astype(v.dtype), v)

    o = lax.map(block, (qb, cb, jnp.arange(nb)))
    return o.transpose(1, 2, 0, 3, 4).reshape(b, h, s, d)


def hybrid_layer(x, norm_g, w_in, b_forget, q_norm_ch, k_norm_ch, q_norm_fox, k_norm_fox, rel_bias, w_out):
    h = rmsnorm(x, norm_g)
    proj = jnp.einsum('bsd,dp->bsp', h, w_in)
    split_points = [W_GROUP * (i + 1) for i in range(N_GROUPS * 4)]
    (q_sb, k_sb, v_sb, g_sb,
     q_ch, k_ch, v_ch, g_ch,
     q_fx, k_fx, v_fx, g_fx,
     f_logit) = jnp.split(proj, split_points, axis=-1)

    o_sb = stick_breaking_attention(split_heads(q_sb), split_heads(k_sb), split_heads(v_sb))

    o_ch = chunked_relpos_attention(rmsnorm(split_heads(q_ch), q_norm_ch),
                                    rmsnorm(split_heads(k_ch), k_norm_ch),
                                    split_heads(v_ch), rel_bias)

    log_f = jax.nn.log_sigmoid(f_logit.astype(jnp.float32) + b_forget.astype(jnp.float32))
    o_fx = forgetting_attention(rmsnorm(split_heads(q_fx), q_norm_fox),
                                rmsnorm(split_heads(k_fx), k_norm_fox),
                                split_heads(v_fx), log_f.transpose(0, 2, 1))

    mixed = jnp.concatenate([merge_heads(o_sb) * jax.nn.silu(g_sb),
                             merge_heads(o_ch) * jax.nn.silu(g_ch),
                             merge_heads(o_fx) * jax.nn.silu(g_fx)], axis=-1)
    return x + jnp.einsum('bsm,md->bsd', mixed, w_out)


def setup_inputs(seed: int = 0) -> dict:
    key = jax.random.key(seed)
    ks = jax.random.split(key, 10)
    f32 = jnp.float32
    x = jax.random.normal(ks[0], (BATCH, SEQ, D_MODEL), f32)
    norm_g = 1.0 + 0.02 * jax.random.normal(ks[1], (DEPTH, D_MODEL), f32)
    w_in = jax.random.normal(ks[2], (DEPTH, D_MODEL, D_IN), f32) * D_MODEL ** -0.5
    b_forget = 3.0 + 0.5 * jax.random.normal(ks[3], (DEPTH, H_GROUP), f32)
    q_norm_ch = 1.0 + 0.02 * jax.random.normal(ks[4], (DEPTH, HEAD_DIM), f32)
    k_norm_ch = 1.0 + 0.02 * jax.random.normal(ks[5], (DEPTH, HEAD_DIM), f32)
    q_norm_fox = 1.0 + 0.02 * jax.random.normal(ks[6], (DEPTH, HEAD_DIM), f32)
    k_norm_fox = 1.0 + 0.02 * jax.random.normal(ks[7], (DEPTH, HEAD_DIM), f32)
    rel_bias = 0.5 * jax.random.normal(ks[8], (DEPTH, H_GROUP, 2 * REL_CLIP + 1), f32)
    w_out = jax.random.normal(ks[9], (DEPTH, D_MIX, D_MODEL), f32) * D_MIX ** -0.5
    return {"x": x, "norm_g": norm_g, "w_in": w_in, "b_forget": b_forget,
            "q_norm_ch": q_norm_ch, "k_norm_ch": k_norm_ch,
            "q_norm_fox": q_norm_fox, "k_norm_fox": k_norm_fox,
            "rel_bias": rel_bias, "w_out": w_out}


def reference(x, norm_g, w_in, b_forget, q_norm_ch, k_norm_ch, q_norm_fox, k_norm_fox, rel_bias, w_out):
    for l in range(DEPTH):
        x = hybrid_layer(x, norm_g[l], w_in[l], b_forget[l], q_norm_ch[l], k_norm_ch[l],
                         q_norm_fox[l], k_norm_fox[l], rel_bias[l], w_out[l])
    return x
```

```python
import functools

import jax
import jax.numpy as jnp
import numpy as np
from jax import lax
from jax.experimental import pallas as pl
from jax.experimental.pallas import tpu as pltpu

D_MODEL = 1024
HEAD_DIM = 64
H_GROUP = 8
W_GROUP = H_GROUP * HEAD_DIM
N_GROUPS = 3
CHUNK = 64
N_LEFT_CHUNKS = 8
LEFT = N_LEFT_CHUNKS * CHUNK
REL_CLIP = 128
EPS = 1e-6
SCALE = HEAD_DIM ** -0.5

LANES = 128
MXU_DIM = 256
F_PAD = LANES
NEG = -1e30

MM_DTYPE = jnp.bfloat16
F32 = jnp.float32

TM_PROJ = 512
TQ_ATT = 256
TQ_CHUNK = 2 * CHUNK
WIN_CHUNK = LEFT + TQ_CHUNK
VMEM_LIMIT = 56 * 1024 * 1024


def _dot(a, b):
    return jnp.dot(a, b, preferred_element_type=F32)


def _split2(x):
    hi = x.astype(MM_DTYPE)
    lo = (x - hi.astype(F32)).astype(MM_DTYPE)
    return hi, lo


def _head_mask(shape, j):
    lane = lax.broadcasted_iota(jnp.int32, shape, 1)
    return (lane >= j * HEAD_DIM) & (lane < (j + 1) * HEAD_DIM)


def _proj_kernel(x_ref, ng_ref, w_ref, wkt_ref, wf_ref, qg_ref, kg_ref, bd_ref,
                 q_ref, kt_ref, v_ref, g_ref, f_ref):
    x = x_ref[...]
    h = x * lax.rsqrt(jnp.mean(x * x, axis=-1, keepdims=True) + EPS) * ng_ref[...]
    hb = h.astype(MM_DTYPE)
    tm = x.shape[0]
    bd = bd_ref[...]
    for grp in range(N_GROUPS):
        q = _dot(hb, w_ref[grp, 0])
        if grp > 0:
            hi, lo = _split2(q * q)
            parts = []
            for c in range(W_GROUP // MXU_DIM):
                sl = slice(c * MXU_DIM, (c + 1) * MXU_DIM)
                parts.append(_dot(hi[:, sl], bd) + _dot(lo[:, sl], bd))
            ssq = jnp.concatenate(parts, axis=1)
            q = q * lax.rsqrt(ssq * (1.0 / HEAD_DIM) + EPS) * qg_ref[grp - 1]
        q_ref[grp] = (q * SCALE).astype(q_ref.dtype)

        kt = lax.dot_general(wkt_ref[grp], hb, (((1,), (1,)), ((), ())),
                             preferred_element_type=F32)
        if grp > 0:
            k3 = kt.reshape(H_GROUP, HEAD_DIM, tm)
            ssq = jnp.sum(k3 * k3, axis=1, keepdims=True)
            k3 = k3 * lax.rsqrt(ssq * (1.0 / HEAD_DIM) + EPS) * kg_ref[grp - 1]
            kt = k3.reshape(W_GROUP, tm)
        kt_ref[grp] = kt.astype(kt_ref.dtype)

        v_ref[grp] = _dot(hb, w_ref[grp, 1]).astype(v_ref.dtype)
        g_ref[grp] = _dot(hb, w_ref[grp, 2])
    f_ref[...] = _dot(hb, wf_ref[...])


def _projection(x, ng, w_qvg, w_kt, w_f, qg, kg, bd):
    b, s, d = x.shape
    tm = min(TM_PROJ, s)
    const = dict(pipeline_mode=pl.Buffered(1))
    grp_shape = jax.ShapeDtypeStruct((N_GROUPS, b, s, W_GROUP), MM_DTYPE)
    return pl.pallas_call(
        _proj_kernel,
        grid=(b, s // tm),
        in_specs=[
            pl.BlockSpec((None, tm, d), lambda bi, i: (bi, i, 0)),
            pl.BlockSpec((1, d), lambda bi, i: (0, 0)),
            pl.BlockSpec((N_GROUPS, 3, d, W_GROUP), lambda bi, i: (0, 0, 0, 0), **const),
            pl.BlockSpec((N_GROUPS, W_GROUP, d), lambda bi, i: (0, 0, 0), **const),
            pl.BlockSpec((d, F_PAD), lambda bi, i: (0, 0), **const),
            pl.BlockSpec((2, 1, W_GROUP), lambda bi, i: (0, 0, 0)),
            pl.BlockSpec((2, HEAD_DIM, 1), lambda bi, i: (0, 0, 0)),
            pl.BlockSpec((MXU_DIM, MXU_DIM), lambda bi, i: (0, 0)),
        ],
        out_specs=[
            pl.BlockSpec((N_GROUPS, None, tm, W_GROUP), lambda bi, i: (0, bi, i, 0)),
            pl.BlockSpec((N_GROUPS, None, W_GROUP, tm), lambda bi, i: (0, bi, 0, i)),
            pl.BlockSpec((N_GROUPS, None, tm, W_GROUP), lambda bi, i: (0, bi, i, 0)),
            pl.BlockSpec((N_GROUPS, None, tm, W_GROUP), lambda bi, i: (0, bi, i, 0)),
            pl.BlockSpec((None, tm, F_PAD), lambda bi, i: (bi, i, 0)),
        ],
        out_shape=[
            grp_shape,
            jax.ShapeDtypeStruct((N_GROUPS, b, W_GROUP, s), MM_DTYPE),
            grp_shape,
            jax.ShapeDtypeStruct((N_GROUPS, b, s, W_GROUP), F32),
            jax.ShapeDtypeStruct((b, s, F_PAD), F32),
        ],
        compiler_params=pltpu.CompilerParams(
            dimension_semantics=("arbitrary", "arbitrary"), vmem_limit_bytes=VMEM_LIMIT),
        name="proj",
    )(x, ng, w_qvg, w_kt, w_f, qg, kg, bd)


CUM_BLK = 512


def _cum_kernel(f_ref, b_ref, u_ref, o_ref):
    z = f_ref[...] + b_ref[...]
    lf = jnp.minimum(z, 0.0) - jnp.log1p(jnp.exp(-jnp.abs(z)))
    lft = lf.T[:H_GROUP]
    s = lft.shape[1]
    u = u_ref[...]
    carry = jnp.zeros((H_GROUP, 1), F32)
    for c in range(s // CUM_BLK):
        blk = lft[:, c * CUM_BLK:(c + 1) * CUM_BLK]
        hi, rest = blk.astype(MM_DTYPE), None
        rest = blk - hi.astype(F32)
        mid, lo = _split2(rest)
        cs = (_dot(hi, u) + _dot(mid, u)) + _dot(lo, u) + carry
        o_ref[:, c * CUM_BLK:(c + 1) * CUM_BLK] = cs
        carry = cs[:, CUM_BLK - 1:CUM_BLK]


def _cum_forget(f, bias_row, u):
    b, s, _ = f.shape
    return pl.pallas_call(
        _cum_kernel,
        grid=(b,),
        in_specs=[
            pl.BlockSpec((None, s, F_PAD), lambda bi: (bi, 0, 0)),
            pl.BlockSpec((1, F_PAD), lambda bi: (0, 0)),
            pl.BlockSpec((CUM_BLK, CUM_BLK), lambda bi: (0, 0)),
        ],
        out_specs=pl.BlockSpec((None, H_GROUP, s), lambda bi: (bi, 0, 0)),
        out_shape=jax.ShapeDtypeStruct((b, H_GROUP, s), F32),
        compiler_params=pltpu.CompilerParams(dimension_semantics=("arbitrary",)),
        name="cum_forget",
    )(f, bias_row, u)


def _gate_store(o_ref, g_ref, o0, o1):
    lane = lax.broadcasted_iota(jnp.int32, o0.shape, 1)
    o = jnp.where(lane < HEAD_DIM, o0, o1)
    g = g_ref[...]
    o_ref[...] = (o * (g * jax.nn.sigmoid(g))).astype(o_ref.dtype)


def _att_specs(grp, tq, s):
    return [
        pl.BlockSpec((None, None, tq, LANES), lambda b, p, i: (grp, b, i, p)),
        pl.BlockSpec((None, None, LANES, s), lambda b, p, i: (grp, b, p, 0)),
        pl.BlockSpec((None, None, s, LANES), lambda b, p, i: (grp, b, 0, p)),
        pl.BlockSpec((None, None, tq, LANES), lambda b, p, i: (grp, b, i, p)),
    ]


_ATT_PARAMS = pltpu.CompilerParams(
    dimension_semantics=("arbitrary", "arbitrary", "arbitrary"), vmem_limit_bytes=VMEM_LIMIT)


def _sb_kernel(q_ref, kt_ref, v_ref, g_ref, u_ref, o_ref):
    i = pl.program_id(2)
    t = q_ref.shape[0]
    q = q_ref[...]
    u = u_ref[...]
    row = lax.broadcasted_iota(jnp.int32, (t, t), 0)
    col = lax.broadcasted_iota(jnp.int32, (t, t), 1)
    strict = col < row

    def tile(qm, kj, carry, diag):
        k0 = pl.multiple_of(kj * t, t)
        z = _dot(qm, kt_ref[:, pl.ds(k0, t)])
        sp = jnp.maximum(z, 0.0) + jnp.log(1.0 + jnp.exp(-jnp.abs(z)))
        if diag:
            sp = jnp.where(strict, sp, 0.0)
        hi, lo = _split2(sp)
        after = _dot(hi, u) + _dot(lo, u)
        w = jnp.exp((z - sp) + after)
        if diag:
            w = jnp.where(strict, w, 0.0)
        pv = _dot(w.astype(MM_DTYPE), v_ref[pl.ds(k0, t), :])
        tot = after[:, :1] - sp[:, :1]
        return jnp.exp(carry) * pv, carry + tot

    outs = []
    for j in range(2):
        qm = jnp.where(_head_mask(q.shape, j), q, jnp.zeros_like(q))
        acc, carry = tile(qm, i, jnp.zeros((t, 1), F32), True)

        def body(it, state, qm=qm):
            acc, carry = state
            d, carry = tile(qm, i - 1 - it, carry, False)
            return acc + d, carry

        acc, _ = lax.fori_loop(0, i, body, (acc, carry))
        outs.append(acc)
    _gate_store(o_ref, g_ref, outs[0], outs[1])


def _sb_attention(q, kt, v, g, u):
    _, b, s, _ = q.shape
    t = min(TQ_ATT, s)
    return pl.pallas_call(
        _sb_kernel,
        grid=(b, W_GROUP // LANES, s // t),
        in_specs=_att_specs(0, t, s) + [pl.BlockSpec((t, t), lambda b_, p, i: (0, 0))],
        out_specs=pl.BlockSpec((None, t, LANES), lambda b_, p, i: (b_, i, p)),
        out_shape=jax.ShapeDtypeStruct((b, s, W_GROUP), MM_DTYPE),
        compiler_params=_ATT_PARAMS,
        name="sb_attention",
    )(q, kt, v, g, u)


def _chunk_kernel(q_ref, kt_ref, v_ref, g_ref, bias_ref, o_ref, kpad, vpad):
    i = pl.program_id(2)
    t = q_ref.shape[0]
    win = bias_ref.shape[2]

    @pl.when(i == 0)
    def _():
        kpad[:, :LEFT] = jnp.zeros((LANES, LEFT), kpad.dtype)
        kpad[:, LEFT:] = kt_ref[...]
        vpad[:LEFT, :] = jnp.zeros((LEFT, LANES), vpad.dtype)
        vpad[LEFT:, :] = v_ref[...]

    r0 = pl.multiple_of(i * t, t)
    kw = kpad[:, pl.ds(r0, win)]
    vw = vpad[pl.ds(r0, win), :]
    col = lax.broadcasted_iota(jnp.int32, (1, win), 1)
    padrow = jnp.where(col >= LEFT - r0, 0.0, NEG)
    q = q_ref[...]
    outs = []
    for j in range(2):
        qm = jnp.where(_head_mask(q.shape, j), q, jnp.zeros_like(q))
        z = _dot(qm, kw) + bias_ref[j] + padrow
        m = jnp.max(z, axis=1, keepdims=True)
        p = jnp.exp(z - m)
        l = jnp.sum(p, axis=1, keepdims=True)
        outs.append(_dot(p.astype(MM_DTYPE), vw) / l)
    _gate_store(o_ref, g_ref, outs[0], outs[1])


def _chunk_attention(q, kt, v, g, bias):
    _, b, s, _ = q.shape
    t = TQ_CHUNK
    return pl.pallas_call(
        _chunk_kernel,
        grid=(b, W_GROUP // LANES, s // t),
        in_specs=_att_specs(1, t, s) + [
            pl.BlockSpec((2, t, WIN_CHUNK), lambda b_, p, i: (p, 0, 0))],
        out_specs=pl.BlockSpec((None, t, LANES), lambda b_, p, i: (b_, i, p)),
        out_shape=jax.ShapeDtypeStruct((b, s, W_GROUP), MM_DTYPE),
        scratch_shapes=[pltpu.VMEM((LANES, LEFT + s), MM_DTYPE),
                        pltpu.VMEM((LEFT + s, LANES), MM_DTYPE)],
        compiler_params=_ATT_PARAMS,
        name="chunk_attention",
    )(q, kt, v, g, bias)


def _chunk_bias_table(rel_bias):
    r = np.arange(TQ_CHUNK)[:, None]
    c = np.arange(WIN_CHUNK)[None, :]
    band = c - CHUNK * (r // CHUNK)
    inside = (band >= 0) & (band < LEFT + CHUNK)
    idx = np.clip(r - c + LEFT, -REL_CLIP, REL_CLIP) + REL_CLIP
    return jnp.where(inside[None], rel_bias[:, idx].astype(F32), NEG)


def _fox_kernel(q_ref, kt_ref, v_ref, g_ref, c_ref, o_ref):
    i = pl.program_id(2)
    p = pl.program_id(1)
    t = q_ref.shape[0]
    q = q_ref[...]
    row = lax.broadcasted_iota(jnp.int32, (t, t), 0)
    col = lax.broadcasted_iota(jnp.int32, (t, t), 1)
    causal = col <= row

    def tile(qm, head, kj, state, diag):
        m, l, acc = state
        k0 = pl.multiple_of(kj * t, t)
        z = _dot(qm, kt_ref[:, pl.ds(k0, t)]) - c_ref[pl.ds(head, 1), pl.ds(k0, t)]
        if diag:
            z = jnp.where(causal, z, NEG)
        m_new = jnp.maximum(m, jnp.max(z, axis=1, keepdims=True))
        alpha = jnp.exp(m - m_new)
        pr = jnp.exp(z - m_new)
        l = alpha * l + jnp.sum(pr, axis=1, keepdims=True)
        acc = alpha * acc + _dot(pr.astype(MM_DTYPE), v_ref[pl.ds(k0, t), :])
        return m_new, l, acc

    outs = []
    for j in range(2):
        qm = jnp.where(_head_mask(q.shape, j), q, jnp.zeros_like(q))
        head = 2 * p + j
        init = (jnp.full((t, 1), NEG, F32), jnp.zeros((t, 1), F32), jnp.zeros((t, LANES), F32))
        state = lax.fori_loop(
            0, i, lambda kj, st, qm=qm, head=head: tile(qm, head, kj, st, False), init)
        _, l, acc = tile(qm, head, i, state, True)
        outs.append(acc / l)
    _gate_store(o_ref, g_ref, outs[0], outs[1])


def _fox_attention(q, kt, v, g, cum):
    _, b, s, _ = q.shape
    t = min(TQ_ATT, s)
    return pl.pallas_call(
        _fox_kernel,
        grid=(b, W_GROUP // LANES, s // t),
        in_specs=_att_specs(2, t, s) + [
            pl.BlockSpec((None, H_GROUP, s), lambda b_, p, i: (b_, 0, 0))],
        out_specs=pl.BlockSpec((None, t, LANES), lambda b_, p, i: (b_, i, p)),
        out_shape=jax.ShapeDtypeStruct((b, s, W_GROUP), MM_DTYPE),
        compiler_params=_ATT_PARAMS,
        name="fox_attention",
    )(q, kt, v, g, cum)


def _out_kernel(x_ref, a_ref, b_ref, c_ref, w_ref, o_ref):
    acc = _dot(a_ref[...], w_ref[0]) + _dot(b_ref[...], w_ref[1]) + _dot(c_ref[...], w_ref[2])
    o_ref[...] = x_ref[...] + acc


def _out_projection(x, ma, mb, mc, w):
    b, s, d = x.shape
    tm = min(TM_PROJ, s)
    mix = pl.BlockSpec((None, tm, W_GROUP), lambda bi, i: (bi, i, 0))
    return pl.pallas_call(
        _out_kernel,
        grid=(b, s // tm),
        in_specs=[pl.BlockSpec((None, tm, d), lambda bi, i: (bi, i, 0)), mix, mix, mix,
                  pl.BlockSpec((N_GROUPS, W_GROUP, d), lambda bi, i: (0, 0, 0))],
        out_specs=pl.BlockSpec((None, tm, d), lambda bi, i: (bi, i, 0)),
        out_shape=jax.ShapeDtypeStruct(x.shape, x.dtype),
        compiler_params=pltpu.CompilerParams(
            dimension_semantics=("arbitrary", "arbitrary"), vmem_limit_bytes=VMEM_LIMIT),
        name="out_proj",
    )(x, ma, mb, mc, w)


def _constants(t_att):
    r = np.arange(MXU_DIM)
    bd = (r[:, None] // HEAD_DIM == r[None, :] // HEAD_DIM).astype(np.float32)
    r = np.arange(t_att)
    u_sb = -(r[:, None] > r[None, :]).astype(np.float32)
    r = np.arange(CUM_BLK)
    u_cum = (r[:, None] <= r[None, :]).astype(np.float32)
    return (jnp.asarray(bd, MM_DTYPE), jnp.asarray(u_sb, MM_DTYPE), jnp.asarray(u_cum, MM_DTYPE))


def _layer(x, ng, w_in, b_forget, qn_ch, kn_ch, qn_fox, kn_fox, rel_bias, w_out, consts):
    bd, u_sb, u_cum = consts
    d = x.shape[-1]
    w4 = w_in[:, :N_GROUPS * 4 * W_GROUP].reshape(d, N_GROUPS, 4, W_GROUP)
    w_qvg = jnp.transpose(w4[:, :, (0, 2, 3), :], (1, 2, 0, 3)).astype(MM_DTYPE)
    w_kt = jnp.transpose(w4[:, :, 1, :], (1, 2, 0)).astype(MM_DTYPE)
    w_f = jnp.pad(w_in[:, N_GROUPS * 4 * W_GROUP:], ((0, 0), (0, F_PAD - H_GROUP))).astype(MM_DTYPE)
    qg = jnp.stack([jnp.tile(qn_ch, H_GROUP), jnp.tile(qn_fox, H_GROUP)])[:, None, :].astype(F32)
    kg = jnp.stack([kn_ch, kn_fox])[:, :, None].astype(F32)
    bias_row = jnp.pad(b_forget.astype(F32), (0, F_PAD - H_GROUP))[None, :]

    q, kt, v, g, f = _projection(x, ng[None, :], w_qvg, w_kt, w_f, qg, kg, bd)
    cum = _cum_forget(f, bias_row, u_cum)
    m_sb = _sb_attention(q, kt, v, g, u_sb)
    m_ch = _chunk_attention(q, kt, v, g, _chunk_bias_table(rel_bias))
    m_fx = _fox_attention(q, kt, v, g, cum)
    w_o = w_out.reshape(N_GROUPS, W_GROUP, d).astype(MM_DTYPE)
    return _out_projection(x, m_sb, m_ch, m_fx, w_o)


def kernel(x, norm_g, w_in, b_forget, q_norm_ch, k_norm_ch, q_norm_fox, k_norm_fox, rel_bias, w_out):
    consts = _constants(min(TQ_ATT, x.shape[1]))
    for l in range(norm_g.shape[0]):
        x = _layer(x, norm_g[l], w_in[l], b_forget[l], q_norm_ch[l], k_norm_ch[l],
                   q_norm_fox[l], k_norm_fox[l], rel_bias[l], w_out[l], consts)
    return x
```

```python
import jax
import jax.numpy as jnp
import numpy as np
from jax import lax
from jax.experimental import pallas as pl
from jax.experimental.pallas import tpu as pltpu

D_MODEL = 1024
HEAD_DIM = 64
H_GROUP = 8
W_GROUP = H_GROUP * HEAD_DIM
N_GROUPS = 3
CHUNK = 64
N_LEFT_CHUNKS = 8
LEFT = N_LEFT_CHUNKS * CHUNK
REL_CLIP = 128
EPS = 1e-6
SCALE = HEAD_DIM ** -0.5

LANES = 128
MXU_DIM = 256
BF16_ROWS = 16
F_PAD = LANES
NEG = -1e30

MM_DTYPE = jnp.bfloat16
F32 = jnp.float32

TM_PROJ = 512
TQ_ATT = 512
TK_SB = MXU_DIM
TQ_CHUNK = 2 * CHUNK
WIN_CHUNK = LEFT + TQ_CHUNK
TS_CHUNK = 512
VMEM_LIMIT = 56 * 1024 * 1024


def _dot(a, b):
    return jnp.dot(a, b, preferred_element_type=F32)


def _split2(x):
    hi = x.astype(MM_DTYPE)
    lo = (x - hi.astype(F32)).astype(MM_DTYPE)
    return hi, lo


def _head_mask(shape, j, axis=1):
    idx = lax.broadcasted_iota(jnp.int32, shape, axis)
    return (idx >= j * HEAD_DIM) & (idx < (j + 1) * HEAD_DIM)


def _proj_kernel(x_ref, ng_ref, w_ref, wkt_ref, wf_ref, qg_ref, kg_ref, bd_ref,
                 q_ref, kt_ref, v_ref, g_ref, f_ref):
    x = x_ref[...]
    h = x * lax.rsqrt(jnp.mean(x * x, axis=-1, keepdims=True) + EPS) * ng_ref[...]
    hb = h.astype(MM_DTYPE)
    tm = x.shape[0]
    bd = bd_ref[...]
    for grp in range(N_GROUPS):
        q = _dot(hb, w_ref[grp, 0])
        if grp > 0:
            hi, lo = _split2(q * q)
            parts = []
            for c in range(W_GROUP // MXU_DIM):
                sl = slice(c * MXU_DIM, (c + 1) * MXU_DIM)
                parts.append(_dot(hi[:, sl], bd) + _dot(lo[:, sl], bd))
            ssq = jnp.concatenate(parts, axis=1)
            q = q * lax.rsqrt(ssq * (1.0 / HEAD_DIM) + EPS) * qg_ref[grp - 1]
        q_ref[grp] = (q * SCALE).astype(q_ref.dtype)

        kt = lax.dot_general(wkt_ref[grp], hb, (((1,), (1,)), ((), ())),
                             preferred_element_type=F32)
        if grp > 0:
            k3 = kt.reshape(H_GROUP, HEAD_DIM, tm)
            ssq = jnp.sum(k3 * k3, axis=1, keepdims=True)
            k3 = k3 * lax.rsqrt(ssq * (1.0 / HEAD_DIM) + EPS) * kg_ref[grp - 1]
            kt = k3.reshape(W_GROUP, tm)
        kt_ref[grp] = kt.astype(kt_ref.dtype)

        v_ref[grp] = _dot(hb, w_ref[grp, 1]).astype(v_ref.dtype)
        g_ref[grp] = _dot(hb, w_ref[grp, 2])
    f_ref[...] = _dot(hb, wf_ref[...])


def _projection(x, ng, w_qvg, w_kt, w_f, qg, kg, bd):
    b, s, d = x.shape
    tm = min(TM_PROJ, s)
    const = dict(pipeline_mode=pl.Buffered(1))
    grp_shape = jax.ShapeDtypeStruct((N_GROUPS, b, s, W_GROUP), MM_DTYPE)
    return pl.pallas_call(
        _proj_kernel,
        grid=(b, s // tm),
        in_specs=[
            pl.BlockSpec((None, tm, d), lambda bi, i: (bi, i, 0)),
            pl.BlockSpec((1, d), lambda bi, i: (0, 0)),
            pl.BlockSpec((N_GROUPS, 3, d, W_GROUP), lambda bi, i: (0, 0, 0, 0), **const),
            pl.BlockSpec((N_GROUPS, W_GROUP, d), lambda bi, i: (0, 0, 0), **const),
            pl.BlockSpec((d, F_PAD), lambda bi, i: (0, 0), **const),
            pl.BlockSpec((2, 1, W_GROUP), lambda bi, i: (0, 0, 0)),
            pl.BlockSpec((2, HEAD_DIM, 1), lambda bi, i: (0, 0, 0)),
            pl.BlockSpec((MXU_DIM, MXU_DIM), lambda bi, i: (0, 0)),
        ],
        out_specs=[
            pl.BlockSpec((N_GROUPS, None, tm, W_GROUP), lambda bi, i: (0, bi, i, 0)),
            pl.BlockSpec((N_GROUPS, None, W_GROUP, tm), lambda bi, i: (0, bi, 0, i)),
            pl.BlockSpec((N_GROUPS, None, tm, W_GROUP), lambda bi, i: (0, bi, i, 0)),
            pl.BlockSpec((N_GROUPS, None, tm, W_GROUP), lambda bi, i: (0, bi, i, 0)),
            pl.BlockSpec((None, tm, F_PAD), lambda bi, i: (bi, i, 0)),
        ],
        out_shape=[
            grp_shape,
            jax.ShapeDtypeStruct((N_GROUPS, b, W_GROUP, s), MM_DTYPE),
            grp_shape,
            jax.ShapeDtypeStruct((N_GROUPS, b, s, W_GROUP), F32),
            jax.ShapeDtypeStruct((b, s, F_PAD), F32),
        ],
        compiler_params=pltpu.CompilerParams(
            dimension_semantics=("arbitrary", "arbitrary"), vmem_limit_bytes=VMEM_LIMIT),
        name="proj",
    )(x, ng, w_qvg, w_kt, w_f, qg, kg, bd)


CUM_BLK = 512


def _cum_kernel(f_ref, b_ref, u_ref, o_ref):
    z = f_ref[...] + b_ref[...]
    lf = jnp.minimum(z, 0.0) - jnp.log1p(jnp.exp(-jnp.abs(z)))
    lft = lf.T[:H_GROUP]
    s = lft.shape[1]
    u = u_ref[...]
    carry = jnp.zeros((H_GROUP, 1), F32)
    for c in range(s // CUM_BLK):
        blk = lft[:, c * CUM_BLK:(c + 1) * CUM_BLK]
        hi = blk.astype(MM_DTYPE)
        mid, lo = _split2(blk - hi.astype(F32))
        cs = (_dot(hi, u) + _dot(mid, u)) + _dot(lo, u) + carry
        o_ref[:, c * CUM_BLK:(c + 1) * CUM_BLK] = cs
        carry = cs[:, CUM_BLK - 1:CUM_BLK]


def _cum_forget(f, bias_row, u):
    b, s, _ = f.shape
    return pl.pallas_call(
        _cum_kernel,
        grid=(b,),
        in_specs=[
            pl.BlockSpec((None, s, F_PAD), lambda bi: (bi, 0, 0)),
            pl.BlockSpec((1, F_PAD), lambda bi: (0, 0)),
            pl.BlockSpec((CUM_BLK, CUM_BLK), lambda bi: (0, 0)),
        ],
        out_specs=pl.BlockSpec((None, H_GROUP, s), lambda bi: (bi, 0, 0)),
        out_shape=jax.ShapeDtypeStruct((b, H_GROUP, s), F32),
        compiler_params=pltpu.CompilerParams(dimension_semantics=("arbitrary",)),
        name="cum_forget",
    )(f, bias_row, u)


def _gated(g, o0, o1):
    lane = lax.broadcasted_iota(jnp.int32, o0.shape, 1)
    o = jnp.where(lane < HEAD_DIM, o0, o1)
    return o * (g * jax.nn.sigmoid(g))


def _att_specs(grp, tq, s):
    return [
        pl.BlockSpec((None, None, tq, LANES), lambda b, p, i: (grp, b, i, p)),
        pl.BlockSpec((None, None, LANES, s), lambda b, p, i: (grp, b, p, 0)),
        pl.BlockSpec((None, None, s, LANES), lambda b, p, i: (grp, b, 0, p)),
        pl.BlockSpec((None, None, tq, LANES), lambda b, p, i: (grp, b, i, p)),
    ]


_ATT_PARAMS = pltpu.CompilerParams(
    dimension_semantics=("arbitrary", "arbitrary", "arbitrary"), vmem_limit_bytes=VMEM_LIMIT)


def _sb_kernel(q_ref, kt_ref, v_ref, g_ref, u_ref, o_ref, acc_ref, carry_ref):
    i = pl.program_id(2)
    t = q_ref.shape[0]
    tk = u_ref.shape[0]
    n_sub = t // tk
    q = q_ref[...].astype(F32)
    qm = [jnp.where(_head_mask(q.shape, j), q, 0.0).astype(MM_DTYPE) for j in range(2)]
    u = u_ref[...]
    acc_ref[...] = jnp.zeros_like(acc_ref)
    carry_ref[...] = jnp.zeros_like(carry_ref)

    def subtile(j, k0, r_lo, diag):
        rows = slice(r_lo, t)
        z = _dot(qm[j][rows], kt_ref[:, pl.ds(k0, tk)])
        sp = jnp.maximum(z, 0.0) + jnp.log(1.0 + jnp.exp(-jnp.abs(z)))
        if diag:
            strict = (lax.broadcasted_iota(jnp.int32, z.shape, 1)
                      < lax.broadcasted_iota(jnp.int32, z.shape, 0))
            sp = jnp.where(strict, sp, 0.0)
        hi, lo = _split2(sp)
        after = _dot(hi, u) + _dot(lo, u)
        w = jnp.exp((z - sp) + after)
        if diag:
            w = jnp.where(strict, w, 0.0)
        pv = _dot(w.astype(MM_DTYPE), v_ref[pl.ds(k0, tk), :])
        carry = carry_ref[j, rows]
        acc_ref[j, rows] += jnp.exp(carry) * pv
        carry_ref[j, rows] = carry + (after[:, :1] - sp[:, :1])

    r0 = i * t
    for c in reversed(range(n_sub)):
        for j in range(2):
            subtile(j, pl.multiple_of(r0 + c * tk, tk), c * tk, True)

    def body(it, _):
        base = (i - 1 - it) * t
        for c in reversed(range(n_sub)):
            for j in range(2):
                subtile(j, pl.multiple_of(base + c * tk, tk), 0, False)
        return 0

    lax.fori_loop(0, i, body, 0)
    o_ref[...] = _gated(g_ref[...], acc_ref[0], acc_ref[1]).astype(o_ref.dtype)


def _sb_attention(q, kt, v, g, u):
    _, b, s, _ = q.shape
    t = min(TQ_ATT, s)
    tk = u.shape[0]
    return pl.pallas_call(
        _sb_kernel,
        grid=(b, W_GROUP // LANES, s // t),
        in_specs=_att_specs(0, t, s) + [pl.BlockSpec((tk, tk), lambda b_, p, i: (0, 0))],
        out_specs=pl.BlockSpec((None, t, LANES), lambda b_, p, i: (b_, i, p)),
        out_shape=jax.ShapeDtypeStruct((b, s, W_GROUP), MM_DTYPE),
        scratch_shapes=[pltpu.VMEM((2, t, LANES), F32), pltpu.VMEM((2, t, 1), F32)],
        compiler_params=_ATT_PARAMS,
        name="sb_attention",
    )(q, kt, v, g, u)


def _chunk_kernel(q_ref, kt_ref, v_ref, g_ref, bias_ref, o_ref, kpad, vpad):
    i = pl.program_id(2)
    ts = q_ref.shape[0]
    t = bias_ref.shape[1]
    win = bias_ref.shape[2]

    @pl.when(i == 0)
    def _():
        kpad[:, :LEFT] = jnp.zeros((LANES, LEFT), kpad.dtype)
        kpad[:, LEFT:] = kt_ref[...]
        vpad[:LEFT, :] = jnp.zeros((LEFT, LANES), vpad.dtype)
        vpad[LEFT:, :] = v_ref[...]

    col = lax.broadcasted_iota(jnp.int32, (1, win), 1)
    for r in range(ts // t):
        rows = slice(r * t, (r + 1) * t)
        r0 = pl.multiple_of(i * ts + r * t, t)
        kw = kpad[:, pl.ds(r0, win)]
        vw = vpad[pl.ds(r0, win), :]
        padrow = jnp.where(col >= LEFT - r0, 0.0, NEG)
        q = q_ref[rows].astype(F32)
        outs = []
        for j in range(2):
            qm = jnp.where(_head_mask(q.shape, j), q, 0.0).astype(MM_DTYPE)
            z = _dot(qm, kw) + bias_ref[j] + padrow
            m = jnp.max(z, axis=1, keepdims=True)
            p = jnp.exp(z - m)
            l = jnp.sum(p, axis=1, keepdims=True)
            outs.append(_dot(p.astype(MM_DTYPE), vw) / l)
        o_ref[rows] = _gated(g_ref[rows], outs[0], outs[1]).astype(o_ref.dtype)


def _chunk_attention(q, kt, v, g, bias):
    _, b, s, _ = q.shape
    ts = min(TS_CHUNK, s)
    return pl.pallas_call(
        _chunk_kernel,
        grid=(b, W_GROUP // LANES, s // ts),
        in_specs=_att_specs(1, ts, s) + [
            pl.BlockSpec((2, TQ_CHUNK, WIN_CHUNK), lambda b_, p, i: (p, 0, 0))],
        out_specs=pl.BlockSpec((None, ts, LANES), lambda b_, p, i: (b_, i, p)),
        out_shape=jax.ShapeDtypeStruct((b, s, W_GROUP), MM_DTYPE),
        scratch_shapes=[pltpu.VMEM((LANES, LEFT + s), MM_DTYPE),
                        pltpu.VMEM((LEFT + s, LANES), MM_DTYPE)],
        compiler_params=_ATT_PARAMS,
        name="chunk_attention",
    )(q, kt, v, g, bias)


def _chunk_bias_table(rel_bias):
    n_diag = TQ_CHUNK + WIN_CHUNK - 1
    k = np.arange(n_diag)
    rel = (TQ_CHUNK - 1 + LEFT) - k
    vec = rel_bias[:, np.clip(rel, -REL_CLIP, REL_CLIP) + REL_CLIP].astype(F32)
    h = vec.shape[0]
    padded = jnp.pad(vec, ((0, 0), (0, 1)))
    flat = jnp.tile(padded, (1, TQ_CHUNK))[:, :TQ_CHUNK * n_diag]
    toep = flat.reshape(h, TQ_CHUNK, n_diag)[:, :, TQ_CHUNK - 1:]
    r = np.arange(TQ_CHUNK)[:, None]
    c = np.arange(WIN_CHUNK)[None, :]
    band = c - CHUNK * (r // CHUNK)
    inside = (band >= 0) & (band < LEFT + CHUNK)
    return jnp.where(inside[None], toep, NEG)


def _fox_kernel(q_ref, kt_ref, v_ref, g_ref, c_ref, o_ref, kaug, vaug, m_ref, acc_ref):
    i = pl.program_id(2)
    p = pl.program_id(1)
    t = q_ref.shape[0]
    s = kt_ref.shape[1]
    aug0 = [HEAD_DIM * (1 - j) for j in range(2)]

    @pl.when(i == 0)
    def _():
        kt = kt_ref[...]
        v = v_ref[...]
        rid = lax.broadcasted_iota(jnp.int32, (BF16_ROWS, s), 0)
        for j in range(2):
            kaug[j] = jnp.where(_head_mask(kt.shape, j, axis=0), kt, jnp.zeros_like(kt))
            vaug[j] = jnp.where(_head_mask(v.shape, j), v, jnp.ones_like(v))
            x = -c_ref[pl.ds(2 * p + j, 1), :]
            hi = x.astype(MM_DTYPE).astype(F32)
            mid = (x - hi).astype(MM_DTYPE).astype(F32)
            lo = (x - hi) - mid
            blk = jnp.where(rid == 0, hi, jnp.where(rid == 1, mid, jnp.where(rid == 2, lo, 0.0)))
            kaug[j, aug0[j]:aug0[j] + BF16_ROWS, :] = blk.astype(kaug.dtype)

    q = q_ref[...].astype(F32)
    lane = lax.broadcasted_iota(jnp.int32, q.shape, 1)
    qa = []
    for j in range(2):
        ones = (lane >= aug0[j]) & (lane < aug0[j] + 3)
        qa.append(jnp.where(_head_mask(q.shape, j), q, jnp.where(ones, 1.0, 0.0)).astype(MM_DTYPE))
    m_ref[...] = jnp.full(m_ref.shape, NEG, F32)
    acc_ref[...] = jnp.zeros_like(acc_ref)

    def tile(kj, diag):
        k0 = pl.multiple_of(kj * t, t)
        for j in range(2):
            z = _dot(qa[j], kaug[j, :, pl.ds(k0, t)])
            if diag:
                causal = (lax.broadcasted_iota(jnp.int32, z.shape, 1)
                          <= lax.broadcasted_iota(jnp.int32, z.shape, 0))
                z = jnp.where(causal, z, NEG)
            m = m_ref[j]
            m_new = jnp.maximum(m, jnp.max(z, axis=1, keepdims=True))
            pr = jnp.exp(z - m_new).astype(MM_DTYPE)
            acc_ref[j] = jnp.exp(m - m_new) * acc_ref[j] + _dot(pr, vaug[j, pl.ds(k0, t), :])
            m_ref[j] = m_new

    def body(kj, _):
        tile(kj, False)
        return 0

    lax.fori_loop(0, i, body, 0)
    tile(i, True)
    outs = []
    for j in range(2):
        acc = acc_ref[j]
        outs.append(acc / pltpu.roll(acc, HEAD_DIM, 1))
    o_ref[...] = _gated(g_ref[...], outs[0], outs[1]).astype(o_ref.dtype)


def _fox_attention(q, kt, v, g, cum):
    _, b, s, _ = q.shape
    t = min(TQ_ATT, s)
    return pl.pallas_call(
        _fox_kernel,
        grid=(b, W_GROUP // LANES, s // t),
        in_specs=_att_specs(2, t, s) + [
            pl.BlockSpec((None, H_GROUP, s), lambda b_, p, i: (b_, 0, 0))],
        out_specs=pl.BlockSpec((None, t, LANES), lambda b_, p, i: (b_, i, p)),
        out_shape=jax.ShapeDtypeStruct((b, s, W_GROUP), MM_DTYPE),
        scratch_shapes=[pltpu.VMEM((2, LANES, s), MM_DTYPE), pltpu.VMEM((2, s, LANES), MM_DTYPE),
                        pltpu.VMEM((2, t, 1), F32), pltpu.VMEM((2, t, LANES), F32)],
        compiler_params=_ATT_PARAMS,
        name="fox_attention",
    )(q, kt, v, g, cum)


def _out_kernel(x_ref, a_ref, b_ref, c_ref, w_ref, o_ref):
    acc = _dot(a_ref[...], w_ref[0]) + _dot(b_ref[...], w_ref[1]) + _dot(c_ref[...], w_ref[2])
    o_ref[...] = x_ref[...] + acc


def _out_projection(x, ma, mb, mc, w):
    b, s, d = x.shape
    tm = min(TM_PROJ, s)
    mix = pl.BlockSpec((None, tm, W_GROUP), lambda bi, i: (bi, i, 0))
    return pl.pallas_call(
        _out_kernel,
        grid=(b, s // tm),
        in_specs=[pl.BlockSpec((None, tm, d), lambda bi, i: (bi, i, 0)), mix, mix, mix,
                  pl.BlockSpec((N_GROUPS, W_GROUP, d), lambda bi, i: (0, 0, 0))],
        out_specs=pl.BlockSpec((None, tm, d), lambda bi, i: (bi, i, 0)),
        out_shape=jax.ShapeDtypeStruct(x.shape, x.dtype),
        compiler_params=pltpu.CompilerParams(
            dimension_semantics=("arbitrary", "arbitrary"), vmem_limit_bytes=VMEM_LIMIT),
        name="out_proj",
    )(x, ma, mb, mc, w)


def _constants():
    r = np.arange(MXU_DIM)
    bd = (r[:, None] // HEAD_DIM == r[None, :] // HEAD_DIM).astype(np.float32)
    r = np.arange(TK_SB)
    u_sb = -(r[:, None] > r[None, :]).astype(np.float32)
    r = np.arange(CUM_BLK)
    u_cum = (r[:, None] <= r[None, :]).astype(np.float32)
    return (jnp.asarray(bd, MM_DTYPE), jnp.asarray(u_sb, MM_DTYPE), jnp.asarray(u_cum, MM_DTYPE))


def _layer(x, ng, w_in, b_forget, qn_ch, kn_ch, qn_fox, kn_fox, rel_bias, w_out, consts):
    bd, u_sb, u_cum = consts
    d = x.shape[-1]
    w4 = w_in[:, :N_GROUPS * 4 * W_GROUP].reshape(d, N_GROUPS, 4, W_GROUP)
    w_qvg = jnp.transpose(w4[:, :, (0, 2, 3), :], (1, 2, 0, 3)).astype(MM_DTYPE)
    w_kt = jnp.transpose(w4[:, :, 1, :], (1, 2, 0)).astype(MM_DTYPE)
    w_f = jnp.pad(w_in[:, N_GROUPS * 4 * W_GROUP:], ((0, 0), (0, F_PAD - H_GROUP))).astype(MM_DTYPE)
    qg = jnp.stack([jnp.tile(qn_ch, H_GROUP), jnp.tile(qn_fox, H_GROUP)])[:, None, :].astype(F32)
    kg = jnp.stack([kn_ch, kn_fox])[:, :, None].astype(F32)
    bias_row = jnp.pad(b_forget.astype(F32), (0, F_PAD - H_GROUP))[None, :]

    q, kt, v, g, f = _projection(x, ng[None, :], w_qvg, w_kt, w_f, qg, kg, bd)
    cum = _cum_forget(f, bias_row, u_cum)
    m_sb = _sb_attention(q, kt, v, g, u_sb)
    m_ch = _chunk_attention(q, kt, v, g, _chunk_bias_table(rel_bias))
    m_fx = _fox_attention(q, kt, v, g, cum)
    w_o = w_out.reshape(N_GROUPS, W_GROUP, d).astype(MM_DTYPE)
    return _out_projection(x, m_sb, m_ch, m_fx, w_o)


def kernel(x, norm_g, w_in, b_forget, q_norm_ch, k_norm_ch, q_norm_fox, k_norm_fox, rel_bias, w_out):
    consts = _constants()
    for l in range(norm_g.shape[0]):
        x = _layer(x, norm_g[l], w_in[l], b_forget[l], q_norm_ch[l], k_norm_ch[l],
                   q_norm_fox[l], k_norm_fox[l], rel_bias[l], w_out[l], consts)
    return x
```

```python
import jax
import jax.numpy as jnp
import numpy as np
from jax import lax
from jax.experimental import pallas as pl
from jax.experimental.pallas import tpu as pltpu

D_MODEL = 1024
HEAD_DIM = 64
H_GROUP = 8
W_GROUP = H_GROUP * HEAD_DIM
N_GROUPS = 3
CHUNK = 64
N_LEFT_CHUNKS = 8
LEFT = N_LEFT_CHUNKS * CHUNK
REL_CLIP = 128
EPS = 1e-6
SCALE = HEAD_DIM ** -0.5
LOG2E = 1.4426950408889634

LANES = 128
MXU_DIM = 256
F_PAD = LANES
NEG = -1e30

MM_DTYPE = jnp.bfloat16
F32 = jnp.float32

TM_PROJ = 512
TQ_ATT = 1024
BF16_ROWS = 16
FOX_V_ROWS = HEAD_DIM + BF16_ROWS
TK_SB = MXU_DIM
TQ_CHUNK = 2 * CHUNK
WIN_CHUNK = LEFT + TQ_CHUNK
TS_CHUNK = 512
CUM_BLK = 512
VMEM_LIMIT = 56 * 1024 * 1024

NAT_KA, NAT_QB, NAT_VB, NAT_KC, NAT_G0 = 0, 1, 2, 3, 4
TR_QA, TR_VA, TR_KB, TR_QC, TR_VC = 0, 1, 2, 3, 4
N_NAT, N_TR = 7, 5


def _dot(a, b):
    return jnp.dot(a, b, preferred_element_type=F32)


def _dot_nt(a, b):
    return lax.dot_general(a, b, (((1,), (1,)), ((), ())), preferred_element_type=F32)


def _split2(x):
    hi = x.astype(MM_DTYPE)
    lo = (x - hi.astype(F32)).astype(MM_DTYPE)
    return hi, lo


def _split3(x):
    hi = x.astype(MM_DTYPE)
    mid, lo = _split2(x - hi.astype(F32))
    return hi, mid, lo


def _head_mask(shape, j, axis):
    idx = lax.broadcasted_iota(jnp.int32, shape, axis)
    return (idx >= j * HEAD_DIM) & (idx < (j + 1) * HEAD_DIM)


def _norm_rows(y, gain_row, bd):
    hi, lo = _split2(y * y)
    parts = []
    for c in range(W_GROUP // MXU_DIM):
        sl = slice(c * MXU_DIM, (c + 1) * MXU_DIM)
        parts.append(_dot(hi[:, sl], bd) + _dot(lo[:, sl], bd))
    ssq = jnp.concatenate(parts, axis=1)
    return y * lax.rsqrt(ssq * (1.0 / HEAD_DIM) + EPS) * gain_row


def _norm_cols(yt, gain_col):
    y3 = yt.reshape(H_GROUP, HEAD_DIM, yt.shape[1])
    ssq = jnp.sum(y3 * y3, axis=1, keepdims=True)
    y3 = y3 * lax.rsqrt(ssq * (1.0 / HEAD_DIM) + EPS) * gain_col
    return y3.reshape(yt.shape)


def _proj_kernel(x_ref, ng_ref, wn_ref, wt_ref, wf_ref, grow_ref, gcol_ref, bd_ref,
                 qa_ref, ka_ref, va_ref, qb_ref, kb_ref, vb_ref, qc_ref, kc_ref, vc_ref,
                 g_ref, f_ref):
    x = x_ref[...]
    h = x * lax.rsqrt(jnp.mean(x * x, axis=-1, keepdims=True) + EPS) * ng_ref[...]
    hb = h.astype(MM_DTYPE)
    bd = bd_ref[...]
    dt = qa_ref.dtype

    def nat(k):
        return _dot(hb, wn_ref[k])

    def tr(k):
        return _dot_nt(wt_ref[k], hb)

    qa_ref[...] = (tr(TR_QA) * SCALE).astype(dt)
    ka_ref[...] = nat(NAT_KA).astype(dt)
    va_ref[...] = tr(TR_VA).astype(dt)
    qb_ref[...] = (_norm_rows(nat(NAT_QB), grow_ref[0], bd) * SCALE).astype(dt)
    kb_ref[...] = _norm_cols(tr(TR_KB), gcol_ref[0]).astype(dt)
    vb_ref[...] = nat(NAT_VB).astype(dt)
    qc_ref[...] = (_norm_cols(tr(TR_QC), gcol_ref[1]) * SCALE).astype(dt)
    kc_ref[...] = _norm_rows(nat(NAT_KC), grow_ref[1], bd).astype(dt)
    vc_ref[...] = tr(TR_VC).astype(dt)
    for grp in range(N_GROUPS):
        g_ref[grp] = nat(NAT_G0 + grp)
    f_ref[...] = _dot(hb, wf_ref[...])


def _projection(x, ng, w_nat, w_tr, w_f, grow, gcol, bd):
    b, s, d = x.shape
    tm = min(TM_PROJ, s)
    const = dict(pipeline_mode=pl.Buffered(1))
    nat_spec = pl.BlockSpec((None, tm, W_GROUP), lambda bi, i: (bi, i, 0))
    tr_spec = pl.BlockSpec((None, W_GROUP, tm), lambda bi, i: (bi, 0, i))
    nat_shape = jax.ShapeDtypeStruct((b, s, W_GROUP), MM_DTYPE)
    tr_shape = jax.ShapeDtypeStruct((b, W_GROUP, s), MM_DTYPE)
    return pl.pallas_call(
        _proj_kernel,
        grid=(b, s // tm),
        in_specs=[
            pl.BlockSpec((None, tm, d), lambda bi, i: (bi, i, 0)),
            pl.BlockSpec((1, d), lambda bi, i: (0, 0)),
            pl.BlockSpec((N_NAT, d, W_GROUP), lambda bi, i: (0, 0, 0), **const),
            pl.BlockSpec((N_TR, W_GROUP, d), lambda bi, i: (0, 0, 0), **const),
            pl.BlockSpec((d, F_PAD), lambda bi, i: (0, 0), **const),
            pl.BlockSpec((2, 1, W_GROUP), lambda bi, i: (0, 0, 0)),
            pl.BlockSpec((2, HEAD_DIM, 1), lambda bi, i: (0, 0, 0)),
            pl.BlockSpec((MXU_DIM, MXU_DIM), lambda bi, i: (0, 0)),
        ],
        out_specs=[
            tr_spec, nat_spec, tr_spec,
            nat_spec, tr_spec, nat_spec,
            tr_spec, nat_spec, tr_spec,
            pl.BlockSpec((N_GROUPS, None, tm, W_GROUP), lambda bi, i: (0, bi, i, 0)),
            pl.BlockSpec((None, tm, F_PAD), lambda bi, i: (bi, i, 0)),
        ],
        out_shape=[
            tr_shape, nat_shape, tr_shape,
            nat_shape, tr_shape, nat_shape,
            tr_shape, nat_shape, tr_shape,
            jax.ShapeDtypeStruct((N_GROUPS, b, s, W_GROUP), F32),
            jax.ShapeDtypeStruct((b, s, F_PAD), F32),
        ],
        compiler_params=pltpu.CompilerParams(
            dimension_semantics=("arbitrary", "arbitrary"), vmem_limit_bytes=VMEM_LIMIT),
        name="proj",
    )(x, ng, w_nat, w_tr, w_f, grow, gcol, bd)


def _cum_kernel(f_ref, b_ref, l_ref, o_ref):
    z = f_ref[...] + b_ref[...]
    lf = jnp.minimum(z, 0.0) - jnp.log1p(jnp.exp(-jnp.abs(z)))
    low = l_ref[...]
    carry = jnp.zeros((1, F_PAD), F32)
    for c in range(lf.shape[0] // CUM_BLK):
        rows = slice(c * CUM_BLK, (c + 1) * CUM_BLK)
        hi, mid, lo = _split3(lf[rows])
        cs = (_dot(low, hi) + _dot(low, mid)) + _dot(low, lo) + carry
        o_ref[rows] = cs
        carry = cs[CUM_BLK - 1:CUM_BLK, :]


def _cum_forget(f, bias_row, low):
    b, s, _ = f.shape
    return pl.pallas_call(
        _cum_kernel,
        grid=(b,),
        in_specs=[
            pl.BlockSpec((None, s, F_PAD), lambda bi: (bi, 0, 0)),
            pl.BlockSpec((1, F_PAD), lambda bi: (0, 0)),
            pl.BlockSpec((CUM_BLK, CUM_BLK), lambda bi: (0, 0)),
        ],
        out_specs=pl.BlockSpec((None, s, F_PAD), lambda bi: (bi, 0, 0)),
        out_shape=jax.ShapeDtypeStruct((b, s, F_PAD), F32),
        compiler_params=pltpu.CompilerParams(dimension_semantics=("arbitrary",)),
        name="cum_forget",
    )(f, bias_row, low)


def _gated(g, o0, o1):
    lane = lax.broadcasted_iota(jnp.int32, o0.shape, 1)
    o = jnp.where(lane < HEAD_DIM, o0, o1)
    return o * (g * jax.nn.sigmoid(g))


def _gated_t(g, ot0, ot1):
    o = jnp.concatenate([ot0, ot1], axis=0).T
    return o * (g * jax.nn.sigmoid(g))


def _nat_tile(tq):
    return lambda b, p, i: (b, i, p)


def _t_specs(grp, tq, s):
    return [
        pl.BlockSpec((None, LANES, tq), lambda b, p, i: (b, p, i)),
        pl.BlockSpec((None, s, LANES), lambda b, p, i: (b, 0, p)),
        pl.BlockSpec((None, LANES, s), lambda b, p, i: (b, p, 0)),
        pl.BlockSpec((None, None, tq, LANES), lambda b, p, i: (grp, b, i, p)),
    ]


_ATT_PARAMS = pltpu.CompilerParams(
    dimension_semantics=("arbitrary", "arbitrary", "arbitrary"), vmem_limit_bytes=VMEM_LIMIT)


def _neg_abs(x):
    bits = lax.bitcast_convert_type(x, jnp.uint32) | jnp.uint32(0x80000000)
    return lax.bitcast_convert_type(bits, F32)


def _run_skewed(units):
    pending = list(units)
    active = []
    while pending or active:
        if pending:
            active.append(pending.pop(0))
        for g in list(active):
            try:
                next(g)
            except StopIteration:
                active.remove(g)


def _masked_heads_t(qt):
    q = qt.astype(F32)
    return [jnp.where(_head_mask(q.shape, j, 0), q, 0.0).astype(MM_DTYPE) for j in range(2)]


def _sb_kernel(qt_ref, k_ref, vt_ref, g_ref, u_ref, o_ref, acc_ref, carry_ref):
    i = pl.program_id(2)
    t = qt_ref.shape[1]
    tk = u_ref.shape[0]
    n_sub = t // tk
    qm = _masked_heads_t(qt_ref[...])
    u = u_ref[...]
    acc_ref[...] = jnp.zeros_like(acc_ref)
    carry_ref[...] = jnp.zeros_like(carry_ref)

    def unit(j, k0, qc, diag):
        cols = slice(qc * tk, (qc + 1) * tk)
        z = _dot(k_ref[pl.ds(k0, tk), :], qm[j][:, cols])
        yield
        sp = jnp.maximum(z, 0.0) + jnp.log(1.0 + jnp.exp(_neg_abs(z)))
        if diag:
            strict = (lax.broadcasted_iota(jnp.int32, z.shape, 0)
                      < lax.broadcasted_iota(jnp.int32, z.shape, 1))
            sp = jnp.where(strict, sp, 0.0)
        hi, lo = _split2(sp)
        yield
        incl = _dot(u, hi) + _dot(u, lo)
        yield
        w = jnp.exp(z + incl)
        if diag:
            w = jnp.where(strict, w, 0.0)
        wb = w.astype(MM_DTYPE)
        yield
        pv = _dot(vt_ref[j * HEAD_DIM:(j + 1) * HEAD_DIM, pl.ds(k0, tk)], wb)
        carry = carry_ref[j, :, cols]
        acc_ref[j, :, cols] += jnp.exp(carry) * pv
        carry_ref[j, :, cols] = carry + incl[:1]

    def units(base, diag):
        out = []
        for c in reversed(range(n_sub)):
            for qc in range(c if diag else 0, n_sub):
                for j in range(2):
                    out.append(unit(j, pl.multiple_of(base + c * tk, tk), qc, diag and qc == c))
        return out

    _run_skewed(units(i * t, True))

    def body(it, _):
        _run_skewed(units((i - 1 - it) * t, False))
        return 0

    lax.fori_loop(0, i, body, 0)
    o_ref[...] = _gated_t(g_ref[...], acc_ref[0], acc_ref[1]).astype(o_ref.dtype)


def _sb_attention(qt, k, vt, g, u):
    b, _, s = qt.shape
    t = min(TQ_ATT, s)
    tk = u.shape[0]
    return pl.pallas_call(
        _sb_kernel,
        grid=(b, W_GROUP // LANES, s // t),
        in_specs=_t_specs(0, t, s) + [pl.BlockSpec((tk, tk), lambda b_, p, i: (0, 0))],
        out_specs=pl.BlockSpec((None, t, LANES), _nat_tile(t)),
        out_shape=jax.ShapeDtypeStruct((b, s, W_GROUP), MM_DTYPE),
        scratch_shapes=[pltpu.VMEM((2, HEAD_DIM, t), F32), pltpu.VMEM((2, 1, t), F32)],
        compiler_params=_ATT_PARAMS,
        name="sb_attention",
    )(qt, k, vt, g, u)


def _chunk_kernel(q_ref, kt_ref, v_ref, g_ref, bias_ref, o_ref, kpad, vpad):
    i = pl.program_id(2)
    ts = q_ref.shape[0]
    t = bias_ref.shape[1]
    win = bias_ref.shape[2]

    @pl.when(i == 0)
    def _():
        kpad[:, :LEFT] = jnp.zeros((LANES, LEFT), kpad.dtype)
        kpad[:, LEFT:] = kt_ref[...]
        vpad[:LEFT, :] = jnp.zeros((LEFT, LANES), vpad.dtype)
        vpad[LEFT:, :] = v_ref[...]

    col = lax.broadcasted_iota(jnp.int32, (1, win), 1)
    for r in range(ts // t):
        rows = slice(r * t, (r + 1) * t)
        r0 = pl.multiple_of(i * ts + r * t, t)
        kw = kpad[:, pl.ds(r0, win)]
        vw = vpad[pl.ds(r0, win), :]
        padrow = jnp.where(col >= LEFT - r0, 0.0, NEG)
        q = q_ref[rows].astype(F32)
        outs = []
        for j in range(2):
            qm = jnp.where(_head_mask(q.shape, j, 1), q, 0.0).astype(MM_DTYPE)
            z = _dot(qm, kw) + bias_ref[j] + padrow
            m = jnp.max(z, axis=1, keepdims=True)
            p = jnp.exp(z - m)
            l = jnp.sum(p, axis=1, keepdims=True)
            outs.append(_dot(p.astype(MM_DTYPE), vw) / l)
        o_ref[rows] = _gated(g_ref[rows], outs[0], outs[1]).astype(o_ref.dtype)


def _chunk_attention(q, kt, v, g, bias):
    b, s, _ = q.shape
    ts = min(TS_CHUNK, s)
    return pl.pallas_call(
        _chunk_kernel,
        grid=(b, W_GROUP // LANES, s // ts),
        in_specs=[
            pl.BlockSpec((None, ts, LANES), lambda b_, p, i: (b_, i, p)),
            pl.BlockSpec((None, LANES, s), lambda b_, p, i: (b_, p, 0)),
            pl.BlockSpec((None, s, LANES), lambda b_, p, i: (b_, 0, p)),
            pl.BlockSpec((None, None, ts, LANES), lambda b_, p, i: (1, b_, i, p)),
            pl.BlockSpec((2, TQ_CHUNK, WIN_CHUNK), lambda b_, p, i: (p, 0, 0))],
        out_specs=pl.BlockSpec((None, ts, LANES), _nat_tile(ts)),
        out_shape=jax.ShapeDtypeStruct((b, s, W_GROUP), MM_DTYPE),
        scratch_shapes=[pltpu.VMEM((LANES, LEFT + s), MM_DTYPE),
                        pltpu.VMEM((LEFT + s, LANES), MM_DTYPE)],
        compiler_params=_ATT_PARAMS,
        name="chunk_attention",
    )(q, kt, v, g, bias)


def _chunk_bias_table(rel_bias):
    n_diag = TQ_CHUNK + WIN_CHUNK - 1
    k = np.arange(n_diag)
    rel = (TQ_CHUNK - 1 + LEFT) - k
    vec = rel_bias[:, np.clip(rel, -REL_CLIP, REL_CLIP) + REL_CLIP].astype(F32)
    h = vec.shape[0]
    padded = jnp.pad(vec, ((0, 0), (0, 1)))
    flat = jnp.tile(padded, (1, TQ_CHUNK))[:, :TQ_CHUNK * n_diag]
    toep = flat.reshape(h, TQ_CHUNK, n_diag)[:, :, TQ_CHUNK - 1:]
    r = np.arange(TQ_CHUNK)[:, None]
    c = np.arange(WIN_CHUNK)[None, :]
    band = c - CHUNK * (r // CHUNK)
    inside = (band >= 0) & (band < LEFT + CHUNK)
    return jnp.where(inside[None], toep, NEG)


def _fox_kernel(qt_ref, k_ref, vt_ref, g_ref, c_ref, o_ref, kaug, vaug, m_ref, acc_ref):
    i = pl.program_id(2)
    p = pl.program_id(1)
    t = qt_ref.shape[1]
    aug0 = [HEAD_DIM * (1 - j) for j in range(2)]
    n_ones = vaug.shape[1] - HEAD_DIM

    @pl.when(i == 0)
    def _():
        k = k_ref[...].astype(F32)
        vt = vt_ref[...].astype(F32)
        pieces = _split3(-c_ref[...])
        row = lax.broadcasted_iota(jnp.int32, (LANES, LANES), 0)
        col = lax.broadcasted_iota(jnp.int32, (LANES, LANES), 1)
        for j in range(2):
            aug = None
            for c, piece in enumerate(pieces):
                place = jnp.where((row == 2 * p + j) & (col == aug0[j] + c), 1.0, 0.0)
                term = _dot(piece, place.astype(MM_DTYPE))
                aug = term if aug is None else aug + term
            kaug[j] = jnp.where(_head_mask(k.shape, j, 1), k, aug).astype(kaug.dtype)
            vaug[j] = jnp.concatenate(
                [vt[j * HEAD_DIM:(j + 1) * HEAD_DIM], jnp.ones((n_ones, vt.shape[1]), F32)],
                axis=0).astype(vaug.dtype)

    q = qt_ref[...].astype(F32)
    row = lax.broadcasted_iota(jnp.int32, q.shape, 0)
    qa = []
    for j in range(2):
        ones = (row >= aug0[j]) & (row < aug0[j] + 3)
        qa.append(jnp.where(_head_mask(q.shape, j, 0), q, jnp.where(ones, 1.0, 0.0)).astype(MM_DTYPE))
    m_ref[...] = jnp.full(m_ref.shape, NEG, F32)
    acc_ref[...] = jnp.zeros_like(acc_ref)

    tk = TK_SB
    n_sub = t // tk

    def unit(j, k0, qc, diag):
        cols = slice(qc * tk, (qc + 1) * tk)
        z = _dot(kaug[j, pl.ds(k0, tk), :], qa[j][:, cols])
        yield
        yield
        if diag:
            causal = (lax.broadcasted_iota(jnp.int32, z.shape, 0)
                      <= lax.broadcasted_iota(jnp.int32, z.shape, 1))
            z = jnp.where(causal, z, NEG)
        m = m_ref[j, :, cols]
        m_new = jnp.maximum(m, jnp.max(z, axis=0, keepdims=True))
        pr = jnp.exp(z - m_new).astype(MM_DTYPE)
        alpha = jnp.exp(m - m_new)
        m_ref[j, :, cols] = m_new
        yield
        pv = _dot(vaug[j, :, pl.ds(k0, tk)], pr)
        acc_ref[j, :, cols] = alpha * acc_ref[j, :, cols] + pv

    def units(base, diag):
        out = []
        for c in range(n_sub):
            for qc in range(c if diag else 0, n_sub):
                for j in range(2):
                    out.append(unit(j, pl.multiple_of(base + c * tk, tk), qc, diag and qc == c))
        return out

    def body(kj, _):
        _run_skewed(units(kj * t, False))
        return 0

    lax.fori_loop(0, i, body, 0)
    _run_skewed(units(i * t, True))
    outs = [acc_ref[j, :HEAD_DIM] / acc_ref[j, HEAD_DIM:HEAD_DIM + 1] for j in range(2)]
    o_ref[...] = _gated_t(g_ref[...], outs[0], outs[1]).astype(o_ref.dtype)


def _fox_attention(qt, k, vt, g, cum):
    b, _, s = qt.shape
    t = min(TQ_ATT, s)
    return pl.pallas_call(
        _fox_kernel,
        grid=(b, W_GROUP // LANES, s // t),
        in_specs=_t_specs(2, t, s) + [
            pl.BlockSpec((None, s, F_PAD), lambda b_, p, i: (b_, 0, 0))],
        out_specs=pl.BlockSpec((None, t, LANES), _nat_tile(t)),
        out_shape=jax.ShapeDtypeStruct((b, s, W_GROUP), MM_DTYPE),
        scratch_shapes=[pltpu.VMEM((2, s, LANES), MM_DTYPE),
                        pltpu.VMEM((2, FOX_V_ROWS, s), MM_DTYPE),
                        pltpu.VMEM((2, 1, t), F32), pltpu.VMEM((2, FOX_V_ROWS, t), F32)],
        compiler_params=_ATT_PARAMS,
        name="fox_attention",
    )(qt, k, vt, g, cum)


def _out_kernel(x_ref, a_ref, b_ref, c_ref, w_ref, o_ref):
    acc = _dot(a_ref[...], w_ref[0]) + _dot(b_ref[...], w_ref[1]) + _dot(c_ref[...], w_ref[2])
    o_ref[...] = x_ref[...] + acc


def _out_projection(x, ma, mb, mc, w):
    b, s, d = x.shape
    tm = min(TM_PROJ, s)
    mix = pl.BlockSpec((None, tm, W_GROUP), lambda bi, i: (bi, i, 0))
    return pl.pallas_call(
        _out_kernel,
        grid=(b, s // tm),
        in_specs=[pl.BlockSpec((None, tm, d), lambda bi, i: (bi, i, 0)), mix, mix, mix,
                  pl.BlockSpec((N_GROUPS, W_GROUP, d), lambda bi, i: (0, 0, 0))],
        out_specs=pl.BlockSpec((None, tm, d), lambda bi, i: (bi, i, 0)),
        out_shape=jax.ShapeDtypeStruct(x.shape, x.dtype),
        compiler_params=pltpu.CompilerParams(
            dimension_semantics=("arbitrary", "arbitrary"), vmem_limit_bytes=VMEM_LIMIT),
        name="out_proj",
    )(x, ma, mb, mc, w)


def _constants():
    r = np.arange(MXU_DIM)
    bd = (r[:, None] // HEAD_DIM == r[None, :] // HEAD_DIM).astype(np.float32)
    r = np.arange(TK_SB)
    u_sb = -(r[None, :] >= r[:, None]).astype(np.float32)
    r = np.arange(CUM_BLK)
    low = (r[None, :] <= r[:, None]).astype(np.float32)
    return (jnp.asarray(bd, MM_DTYPE), jnp.asarray(u_sb, MM_DTYPE), jnp.asarray(low, MM_DTYPE))


def _layer(x, ng, w_in, b_forget, qn_ch, kn_ch, qn_fox, kn_fox, rel_bias, w_out, consts):
    bd, u_sb, low = consts
    d = x.shape[-1]
    w4 = w_in[:, :N_GROUPS * 4 * W_GROUP].reshape(d, N_GROUPS, 4, W_GROUP)
    nat_blocks = [(0, 1), (1, 0), (1, 2), (2, 1), (0, 3), (1, 3), (2, 3)]
    tr_blocks = [(0, 0), (0, 2), (1, 1), (2, 0), (2, 2)]
    w_nat = jnp.stack([w4[:, g, c, :] for g, c in nat_blocks]).astype(MM_DTYPE)
    w_tr = jnp.stack([w4[:, g, c, :].T for g, c in tr_blocks]).astype(MM_DTYPE)
    w_f = jnp.pad(w_in[:, N_GROUPS * 4 * W_GROUP:], ((0, 0), (0, F_PAD - H_GROUP))).astype(MM_DTYPE)
    grow = jnp.stack([jnp.tile(qn_ch, H_GROUP), jnp.tile(kn_fox, H_GROUP)])[:, None, :].astype(F32)
    gcol = jnp.stack([kn_ch, qn_fox])[:, :, None].astype(F32)
    bias_row = jnp.pad(b_forget.astype(F32), (0, F_PAD - H_GROUP))[None, :]

    qa, ka, va, qb, kb, vb, qc, kc, vc, g, f = _projection(
        x, ng[None, :], w_nat, w_tr, w_f, grow, gcol, bd)
    cum = _cum_forget(f, bias_row, low)
    m_sb = _sb_attention(qa, ka, va, g, u_sb)
    m_ch = _chunk_attention(qb, kb, vb, g, _chunk_bias_table(rel_bias))
    m_fx = _fox_attention(qc, kc, vc, g, cum)
    w_o = w_out.reshape(N_GROUPS, W_GROUP, d).astype(MM_DTYPE)
    return _out_projection(x, m_sb, m_ch, m_fx, w_o)


def kernel(x, norm_g, w_in, b_forget, q_norm_ch, k_norm_ch, q_norm_fox, k_norm_fox, rel_bias, w_out):
    consts = _constants()
    for l in range(norm_g.shape[0]):
        x = _layer(x, norm_g[l], w_in[l], b_forget[l], q_norm_ch[l], k_norm_ch[l],
                   q_norm_fox[l], k_norm_fox[l], rel_bias[l], w_out[l], consts)
    return x
```

```python
import jax
import jax.numpy as jnp
import numpy as np
from jax import lax
from jax.experimental import pallas as pl
from jax.experimental.pallas import tpu as pltpu

D_MODEL = 1024
HEAD_DIM = 64
H_GROUP = 8
W_GROUP = H_GROUP * HEAD_DIM
N_GROUPS = 3
CHUNK = 64
N_LEFT_CHUNKS = 8
LEFT = N_LEFT_CHUNKS * CHUNK
REL_CLIP = 128
EPS = 1e-6
SCALE = HEAD_DIM ** -0.5
LOG2E = 1.4426950408889634

LANES = 128
MXU_DIM = 256
F_PAD = LANES
NEG = -1e30

MM_DTYPE = jnp.bfloat16
F32 = jnp.float32

TM_PROJ = 512
TQ_ATT = 1024
BF16_ROWS = 16
FOX_V_ROWS = HEAD_DIM + BF16_ROWS
TK_SB = MXU_DIM
TQ_CHUNK = MXU_DIM
WIN_CHUNK = LEFT + TQ_CHUNK
TS_CHUNK = 2048
CUM_BLK = 512
VMEM_LIMIT = 56 * 1024 * 1024


def _dot(a, b):
    return jnp.dot(a, b, preferred_element_type=F32)


def _dot_nt(a, b):
    return lax.dot_general(a, b, (((1,), (1,)), ((), ())), preferred_element_type=F32)


def _split2(x):
    hi = x.astype(MM_DTYPE)
    lo = (x - hi.astype(F32)).astype(MM_DTYPE)
    return hi, lo


def _split3(x):
    hi = x.astype(MM_DTYPE)
    mid, lo = _split2(x - hi.astype(F32))
    return hi, mid, lo


def _head_mask(shape, j, axis):
    idx = lax.broadcasted_iota(jnp.int32, shape, axis)
    return (idx >= j * HEAD_DIM) & (idx < (j + 1) * HEAD_DIM)


def _norm_rows(y, gain_row, bd):
    hi, lo = _split2(y * y)
    parts = []
    for c in range(W_GROUP // MXU_DIM):
        sl = slice(c * MXU_DIM, (c + 1) * MXU_DIM)
        parts.append(_dot(hi[:, sl], bd) + _dot(lo[:, sl], bd))
    ssq = jnp.concatenate(parts, axis=1)
    return y * lax.rsqrt(ssq * (1.0 / HEAD_DIM) + EPS) * gain_row


def _norm_cols(yt, gain_col):
    y3 = yt.reshape(H_GROUP, HEAD_DIM, yt.shape[1])
    ssq = jnp.sum(y3 * y3, axis=1, keepdims=True)
    y3 = y3 * lax.rsqrt(ssq * (1.0 / HEAD_DIM) + EPS) * gain_col
    return y3.reshape(yt.shape)


def _proj_kernel(x_ref, ng_ref, wn_ref, wt_ref, wf_ref, grow_ref, gcol_ref, bd_ref,
                 qt_ref, k_ref, vt_ref, g_ref, f_ref):
    x = x_ref[...]
    h = x * lax.rsqrt(jnp.mean(x * x, axis=-1, keepdims=True) + EPS) * ng_ref[...]
    hb = h.astype(MM_DTYPE)
    bd = bd_ref[...]
    dt = qt_ref.dtype
    for grp in range(N_GROUPS):
        qt = _dot_nt(wt_ref[grp], hb)
        k = _dot(hb, wn_ref[grp])
        if grp > 0:
            qt = _norm_cols(qt, gcol_ref[grp - 1])
            k = _norm_rows(k, grow_ref[grp - 1], bd)
        qt_ref[grp] = (qt * SCALE).astype(dt)
        k_ref[grp] = k.astype(dt)
        vt_ref[grp] = _dot_nt(wt_ref[N_GROUPS + grp], hb).astype(dt)
        g_ref[grp] = _dot(hb, wn_ref[N_GROUPS + grp])
    f_ref[...] = _dot(hb, wf_ref[...])


def _projection(x, ng, w_nat, w_tr, w_f, grow, gcol, bd):
    b, s, d = x.shape
    tm = min(TM_PROJ, s)
    const = dict(pipeline_mode=pl.Buffered(1))
    nat_spec = pl.BlockSpec((N_GROUPS, None, tm, W_GROUP), lambda bi, i: (0, bi, i, 0))
    tr_spec = pl.BlockSpec((N_GROUPS, None, W_GROUP, tm), lambda bi, i: (0, bi, 0, i))
    nat_shape = jax.ShapeDtypeStruct((N_GROUPS, b, s, W_GROUP), MM_DTYPE)
    tr_shape = jax.ShapeDtypeStruct((N_GROUPS, b, W_GROUP, s), MM_DTYPE)
    return pl.pallas_call(
        _proj_kernel,
        grid=(b, s // tm),
        in_specs=[
            pl.BlockSpec((None, tm, d), lambda bi, i: (bi, i, 0)),
            pl.BlockSpec((1, d), lambda bi, i: (0, 0)),
            pl.BlockSpec((2 * N_GROUPS, d, W_GROUP), lambda bi, i: (0, 0, 0), **const),
            pl.BlockSpec((2 * N_GROUPS, W_GROUP, d), lambda bi, i: (0, 0, 0), **const),
            pl.BlockSpec((d, F_PAD), lambda bi, i: (0, 0), **const),
            pl.BlockSpec((2, 1, W_GROUP), lambda bi, i: (0, 0, 0)),
            pl.BlockSpec((2, HEAD_DIM, 1), lambda bi, i: (0, 0, 0)),
            pl.BlockSpec((MXU_DIM, MXU_DIM), lambda bi, i: (0, 0)),
        ],
        out_specs=[
            tr_spec, nat_spec, tr_spec,
            pl.BlockSpec((N_GROUPS, None, tm, W_GROUP), lambda bi, i: (0, bi, i, 0)),
            pl.BlockSpec((None, tm, F_PAD), lambda bi, i: (bi, i, 0)),
        ],
        out_shape=[
            tr_shape, nat_shape, tr_shape,
            jax.ShapeDtypeStruct((N_GROUPS, b, s, W_GROUP), F32),
            jax.ShapeDtypeStruct((b, s, F_PAD), F32),
        ],
        compiler_params=pltpu.CompilerParams(
            dimension_semantics=("arbitrary", "arbitrary"), vmem_limit_bytes=VMEM_LIMIT),
        name="proj",
    )(x, ng, w_nat, w_tr, w_f, grow, gcol, bd)


def _cum_kernel(f_ref, b_ref, l_ref, o_ref):
    z = f_ref[...] + b_ref[...]
    lf = jnp.minimum(z, 0.0) - jnp.log1p(jnp.exp(-jnp.abs(z)))
    low = l_ref[...]
    carry = jnp.zeros((1, F_PAD), F32)
    for c in range(lf.shape[0] // CUM_BLK):
        rows = slice(c * CUM_BLK, (c + 1) * CUM_BLK)
        hi, mid, lo = _split3(lf[rows])
        cs = (_dot(low, hi) + _dot(low, mid)) + _dot(low, lo) + carry
        o_ref[rows] = cs
        carry = cs[CUM_BLK - 1:CUM_BLK, :]


def _cum_forget(f, bias_row, low):
    b, s, _ = f.shape
    return pl.pallas_call(
        _cum_kernel,
        grid=(b,),
        in_specs=[
            pl.BlockSpec((None, s, F_PAD), lambda bi: (bi, 0, 0)),
            pl.BlockSpec((1, F_PAD), lambda bi: (0, 0)),
            pl.BlockSpec((CUM_BLK, CUM_BLK), lambda bi: (0, 0)),
        ],
        out_specs=pl.BlockSpec((None, s, F_PAD), lambda bi: (bi, 0, 0)),
        out_shape=jax.ShapeDtypeStruct((b, s, F_PAD), F32),
        compiler_params=pltpu.CompilerParams(dimension_semantics=("arbitrary",)),
        name="cum_forget",
    )(f, bias_row, low)


def _gated(g, o0, o1):
    lane = lax.broadcasted_iota(jnp.int32, o0.shape, 1)
    o = jnp.where(lane < HEAD_DIM, o0, o1)
    return o * (g * jax.nn.sigmoid(g))


def _gated_t(g, ot0, ot1):
    o = jnp.concatenate([ot0, ot1], axis=0).T
    return o * (g * jax.nn.sigmoid(g))


def _nat_tile(tq):
    return lambda b, p, i: (b, i, p)


def _t_specs(grp, tq, s):
    return [
        pl.BlockSpec((None, None, LANES, tq), lambda b, p, i: (grp, b, p, i)),
        pl.BlockSpec((None, None, s, LANES), lambda b, p, i: (grp, b, 0, p)),
        pl.BlockSpec((None, None, LANES, s), lambda b, p, i: (grp, b, p, 0)),
        pl.BlockSpec((None, None, tq, LANES), lambda b, p, i: (grp, b, i, p)),
    ]


_ATT_PARAMS = pltpu.CompilerParams(
    dimension_semantics=("arbitrary", "arbitrary", "arbitrary"), vmem_limit_bytes=VMEM_LIMIT)


def _neg_abs(x):
    bits = lax.bitcast_convert_type(x, jnp.uint32) | jnp.uint32(0x80000000)
    return lax.bitcast_convert_type(bits, F32)


def _run_skewed(units):
    pending = list(units)
    active = []
    while pending or active:
        if pending:
            active.append(pending.pop(0))
        for g in list(active):
            try:
                next(g)
            except StopIteration:
                active.remove(g)


def _masked_heads_t(qt):
    q = qt.astype(F32)
    return [jnp.where(_head_mask(q.shape, j, 0), q, 0.0).astype(MM_DTYPE) for j in range(2)]


def _sb_kernel(qt_ref, k_ref, vt_ref, g_ref, u_ref, o_ref, acc_ref, carry_ref):
    i = pl.program_id(2)
    t = qt_ref.shape[1]
    tk = u_ref.shape[0]
    n_sub = t // tk
    qm = _masked_heads_t(qt_ref[...])
    u = u_ref[...]
    acc_ref[...] = jnp.zeros_like(acc_ref)
    carry_ref[...] = jnp.zeros_like(carry_ref)

    def unit(j, k0, qc, diag):
        cols = slice(qc * tk, (qc + 1) * tk)
        z = _dot(k_ref[pl.ds(k0, tk), :], qm[j][:, cols])
        yield
        sp = jnp.maximum(z, 0.0) + jnp.log(1.0 + jnp.exp(_neg_abs(z)))
        if diag:
            strict = (lax.broadcasted_iota(jnp.int32, z.shape, 0)
                      < lax.broadcasted_iota(jnp.int32, z.shape, 1))
            sp = jnp.where(strict, sp, 0.0)
        hi, lo = _split2(sp)
        yield
        incl = _dot(u, hi) + _dot(u, lo)
        yield
        w = jnp.exp(z + incl)
        if diag:
            w = jnp.where(strict, w, 0.0)
        wb = w.astype(MM_DTYPE)
        yield
        pv = _dot(vt_ref[j * HEAD_DIM:(j + 1) * HEAD_DIM, pl.ds(k0, tk)], wb)
        carry = carry_ref[j, :, cols]
        acc_ref[j, :, cols] += jnp.exp(carry) * pv
        carry_ref[j, :, cols] = carry + incl[:1]

    def units(base, diag):
        out = []
        for c in reversed(range(n_sub)):
            for qc in range(c if diag else 0, n_sub):
                for j in range(2):
                    out.append(unit(j, pl.multiple_of(base + c * tk, tk), qc, diag and qc == c))
        return out

    _run_skewed(units(i * t, True))

    def body(it, _):
        _run_skewed(units((i - 1 - it) * t, False))
        return 0

    lax.fori_loop(0, i, body, 0)
    o_ref[...] = _gated_t(g_ref[...], acc_ref[0], acc_ref[1]).astype(o_ref.dtype)


def _sb_attention(qt, k, vt, g, u):
    _, b, _, s = qt.shape
    t = min(TQ_ATT, s)
    tk = u.shape[0]
    return pl.pallas_call(
        _sb_kernel,
        grid=(b, W_GROUP // LANES, s // t),
        in_specs=_t_specs(0, t, s) + [pl.BlockSpec((tk, tk), lambda b_, p, i: (0, 0))],
        out_specs=pl.BlockSpec((None, t, LANES), _nat_tile(t)),
        out_shape=jax.ShapeDtypeStruct((b, s, W_GROUP), MM_DTYPE),
        scratch_shapes=[pltpu.VMEM((2, HEAD_DIM, t), F32), pltpu.VMEM((2, 1, t), F32)],
        compiler_params=_ATT_PARAMS,
        name="sb_attention",
    )(qt, k, vt, g, u)


def _chunk_kernel(qt_ref, k_ref, vt_ref, g_ref, bias_ref, o_ref, kpad, vaug, ot_ref):
    i = pl.program_id(2)
    ts = qt_ref.shape[1]
    win, tq = bias_ref.shape[1:]
    s = k_ref.shape[0]
    n_ones = vaug.shape[1] - HEAD_DIM

    @pl.when(i == 0)
    def _():
        kpad[:LEFT] = jnp.zeros((LEFT, LANES), kpad.dtype)
        kpad[LEFT:] = k_ref[...]
        vt = vt_ref[...].astype(F32)
        for j in range(2):
            vaug[j, :, :LEFT] = jnp.zeros((vaug.shape[1], LEFT), vaug.dtype)
            vaug[j, :, LEFT:] = jnp.concatenate(
                [vt[j * HEAD_DIM:(j + 1) * HEAD_DIM], jnp.ones((n_ones, s), F32)],
                axis=0).astype(vaug.dtype)

    qm = _masked_heads_t(qt_ref[...])

    def unit(j, r, first_step):
        cols = slice(r * tq, (r + 1) * tq)
        r0 = pl.multiple_of(i * ts + r * tq, tq)
        zs = []
        for kb in range(win // tq):
            k0 = pl.multiple_of(r0 + kb * tq, tq)
            z = _dot(kpad[pl.ds(k0, tq), :], qm[j][:, cols]) + bias_ref[j, kb * tq:(kb + 1) * tq]
            lo_key = LEFT - r * tq - kb * tq
            if first_step and lo_key > 0:
                key = lax.broadcasted_iota(jnp.int32, z.shape, 0)
                z = jnp.where(key >= lo_key, z, NEG)
            zs.append(z)
            yield
        m = zs[0].max(axis=0, keepdims=True)
        for z in zs[1:]:
            m = jnp.maximum(m, z.max(axis=0, keepdims=True))
        ps = []
        for z in zs:
            ps.append(jnp.exp(z - m).astype(MM_DTYPE))
            yield
        pv = None
        for kb, p in enumerate(ps):
            k0 = pl.multiple_of(r0 + kb * tq, tq)
            term = _dot(vaug[j, :, pl.ds(k0, tq)], p)
            pv = term if pv is None else pv + term
        ot_ref[j * HEAD_DIM:(j + 1) * HEAD_DIM, cols] = pv[:HEAD_DIM] / pv[HEAD_DIM:HEAD_DIM + 1]

    def run(first_step):
        _run_skewed([unit(j, r, first_step) for r in range(ts // tq) for j in range(2)])

    @pl.when(i == 0)
    def _():
        run(True)

    @pl.when(i > 0)
    def _():
        run(False)

    o_ref[...] = _gated_t(g_ref[...], ot_ref[:HEAD_DIM], ot_ref[HEAD_DIM:]).astype(o_ref.dtype)


def _chunk_attention(qt, k, vt, g, bias):
    _, b, _, s = qt.shape
    ts = min(TS_CHUNK, s)
    return pl.pallas_call(
        _chunk_kernel,
        grid=(b, W_GROUP // LANES, s // ts),
        in_specs=_t_specs(1, ts, s) + [
            pl.BlockSpec((2, WIN_CHUNK, TQ_CHUNK), lambda b_, p, i: (p, 0, 0))],
        out_specs=pl.BlockSpec((None, ts, LANES), _nat_tile(ts)),
        out_shape=jax.ShapeDtypeStruct((b, s, W_GROUP), MM_DTYPE),
        scratch_shapes=[pltpu.VMEM((LEFT + s, LANES), MM_DTYPE),
                        pltpu.VMEM((2, FOX_V_ROWS, LEFT + s), MM_DTYPE),
                        pltpu.VMEM((LANES, ts), F32)],
        compiler_params=_ATT_PARAMS,
        name="chunk_attention",
    )(qt, k, vt, g, bias)


def _chunk_bias_table(rel_bias):
    n_diag = WIN_CHUNK + TQ_CHUNK - 1
    kk = np.arange(n_diag)
    rel = kk - (WIN_CHUNK - 1) + LEFT
    vec = rel_bias[:, np.clip(rel, -REL_CLIP, REL_CLIP) + REL_CLIP].astype(F32)
    h = vec.shape[0]
    padded = jnp.pad(vec, ((0, 0), (0, 1)))
    flat = jnp.tile(padded, (1, WIN_CHUNK))[:, :WIN_CHUNK * n_diag]
    toep = flat.reshape(h, WIN_CHUNK, n_diag)[:, :, WIN_CHUNK - 1:]
    c = np.arange(WIN_CHUNK)[:, None]
    r = np.arange(TQ_CHUNK)[None, :]
    band = c - CHUNK * (r // CHUNK)
    inside = (band >= 0) & (band < LEFT + CHUNK)
    return jnp.where(inside[None], toep, NEG)


def _fox_kernel(qt_ref, k_ref, vt_ref, g_ref, c_ref, o_ref, kaug, vaug, m_ref, acc_ref):
    i = pl.program_id(2)
    p = pl.program_id(1)
    t = qt_ref.shape[1]
    aug0 = [HEAD_DIM * (1 - j) for j in range(2)]
    n_ones = vaug.shape[1] - HEAD_DIM

    @pl.when(i == 0)
    def _():
        k = k_ref[...].astype(F32)
        vt = vt_ref[...].astype(F32)
        pieces = _split3(-c_ref[...])
        row = lax.broadcasted_iota(jnp.int32, (LANES, LANES), 0)
        col = lax.broadcasted_iota(jnp.int32, (LANES, LANES), 1)
        for j in range(2):
            aug = None
            for c, piece in enumerate(pieces):
                place = jnp.where((row == 2 * p + j) & (col == aug0[j] + c), 1.0, 0.0)
                term = _dot(piece, place.astype(MM_DTYPE))
                aug = term if aug is None else aug + term
            kaug[j] = jnp.where(_head_mask(k.shape, j, 1), k, aug).astype(kaug.dtype)
            vaug[j] = jnp.concatenate(
                [vt[j * HEAD_DIM:(j + 1) * HEAD_DIM], jnp.ones((n_ones, vt.shape[1]), F32)],
                axis=0).astype(vaug.dtype)

    q = qt_ref[...].astype(F32)
    row = lax.broadcasted_iota(jnp.int32, q.shape, 0)
    qa = []
    for j in range(2):
        ones = (row >= aug0[j]) & (row < aug0[j] + 3)
        qa.append(jnp.where(_head_mask(q.shape, j, 0), q, jnp.where(ones, 1.0, 0.0)).astype(MM_DTYPE))
    m_ref[...] = jnp.full(m_ref.shape, NEG, F32)
    acc_ref[...] = jnp.zeros_like(acc_ref)

    tk = TK_SB
    n_sub = t // tk

    def unit(j, k0, qc, diag):
        cols = slice(qc * tk, (qc + 1) * tk)
        z = _dot(kaug[j, pl.ds(k0, tk), :], qa[j][:, cols])
        yield
        yield
        if diag:
            causal = (lax.broadcasted_iota(jnp.int32, z.shape, 0)
                      <= lax.broadcasted_iota(jnp.int32, z.shape, 1))
            z = jnp.where(causal, z, NEG)
        m = m_ref[j, :, cols]
        m_new = jnp.maximum(m, jnp.max(z, axis=0, keepdims=True))
        pr = jnp.exp(z - m_new).astype(MM_DTYPE)
        alpha = jnp.exp(m - m_new)
        m_ref[j, :, cols] = m_new
        yield
        pv = _dot(vaug[j, :, pl.ds(k0, tk)], pr)
        acc_ref[j, :, cols] = alpha * acc_ref[j, :, cols] + pv

    def units(base, diag):
        out = []
        for c in range(n_sub):
            for qc in range(c if diag else 0, n_sub):
                for j in range(2):
                    out.append(unit(j, pl.multiple_of(base + c * tk, tk), qc, diag and qc == c))
        return out

    def body(kj, _):
        _run_skewed(units(kj * t, False))
        return 0

    lax.fori_loop(0, i, body, 0)
    _run_skewed(units(i * t, True))
    outs = [acc_ref[j, :HEAD_DIM] / acc_ref[j, HEAD_DIM:HEAD_DIM + 1] for j in range(2)]
    o_ref[...] = _gated_t(g_ref[...], outs[0], outs[1]).astype(o_ref.dtype)


def _fox_attention(qt, k, vt, g, cum):
    _, b, _, s = qt.shape
    t = min(TQ_ATT, s)
    return pl.pallas_call(
        _fox_kernel,
        grid=(b, W_GROUP // LANES, s // t),
        in_specs=_t_specs(2, t, s) + [
            pl.BlockSpec((None, s, F_PAD), lambda b_, p, i: (b_, 0, 0))],
        out_specs=pl.BlockSpec((None, t, LANES), _nat_tile(t)),
        out_shape=jax.ShapeDtypeStruct((b, s, W_GROUP), MM_DTYPE),
        scratch_shapes=[pltpu.VMEM((2, s, LANES), MM_DTYPE),
                        pltpu.VMEM((2, FOX_V_ROWS, s), MM_DTYPE),
                        pltpu.VMEM((2, 1, t), F32), pltpu.VMEM((2, FOX_V_ROWS, t), F32)],
        compiler_params=_ATT_PARAMS,
        name="fox_attention",
    )(qt, k, vt, g, cum)


def _out_kernel(x_ref, a_ref, b_ref, c_ref, w_ref, o_ref):
    acc = _dot(a_ref[...], w_ref[0]) + _dot(b_ref[...], w_ref[1]) + _dot(c_ref[...], w_ref[2])
    o_ref[...] = x_ref[...] + acc


def _out_projection(x, ma, mb, mc, w):
    b, s, d = x.shape
    tm = min(TM_PROJ, s)
    mix = pl.BlockSpec((None, tm, W_GROUP), lambda bi, i: (bi, i, 0))
    return pl.pallas_call(
        _out_kernel,
        grid=(b, s // tm),
        in_specs=[pl.BlockSpec((None, tm, d), lambda bi, i: (bi, i, 0)), mix, mix, mix,
                  pl.BlockSpec((N_GROUPS, W_GROUP, d), lambda bi, i: (0, 0, 0))],
        out_specs=pl.BlockSpec((None, tm, d), lambda bi, i: (bi, i, 0)),
        out_shape=jax.ShapeDtypeStruct(x.shape, x.dtype),
        compiler_params=pltpu.CompilerParams(
            dimension_semantics=("arbitrary", "arbitrary"), vmem_limit_bytes=VMEM_LIMIT),
        name="out_proj",
    )(x, ma, mb, mc, w)


def _constants():
    r = np.arange(MXU_DIM)
    bd = (r[:, None] // HEAD_DIM == r[None, :] // HEAD_DIM).astype(np.float32)
    r = np.arange(TK_SB)
    u_sb = -(r[None, :] >= r[:, None]).astype(np.float32)
    r = np.arange(CUM_BLK)
    low = (r[None, :] <= r[:, None]).astype(np.float32)
    return (jnp.asarray(bd, MM_DTYPE), jnp.asarray(u_sb, MM_DTYPE), jnp.asarray(low, MM_DTYPE))


def _layer(x, ng, w_in, b_forget, qn_ch, kn_ch, qn_fox, kn_fox, rel_bias, w_out, consts):
    bd, u_sb, low = consts
    d = x.shape[-1]
    w4 = w_in[:, :N_GROUPS * 4 * W_GROUP].reshape(d, N_GROUPS, 4, W_GROUP)
    w_nat = jnp.transpose(w4[:, :, (1, 3), :], (2, 1, 0, 3)).reshape(2 * N_GROUPS, d, W_GROUP)
    w_tr = jnp.transpose(w4[:, :, (0, 2), :], (2, 1, 3, 0)).reshape(2 * N_GROUPS, W_GROUP, d)
    w_f = jnp.pad(w_in[:, N_GROUPS * 4 * W_GROUP:], ((0, 0), (0, F_PAD - H_GROUP))).astype(MM_DTYPE)
    grow = jnp.stack([jnp.tile(kn_ch, H_GROUP), jnp.tile(kn_fox, H_GROUP)])[:, None, :].astype(F32)
    gcol = jnp.stack([qn_ch, qn_fox])[:, :, None].astype(F32)
    bias_row = jnp.pad(b_forget.astype(F32), (0, F_PAD - H_GROUP))[None, :]

    qt, k, vt, g, f = _projection(x, ng[None, :], w_nat.astype(MM_DTYPE), w_tr.astype(MM_DTYPE),
                                  w_f, grow, gcol, bd)
    cum = _cum_forget(f, bias_row, low)
    m_sb = _sb_attention(qt, k, vt, g, u_sb)
    m_ch = _chunk_attention(qt, k, vt, g, _chunk_bias_table(rel_bias))
    m_fx = _fox_attention(qt, k, vt, g, cum)
    w_o = w_out.reshape(N_GROUPS, W_GROUP, d).astype(MM_DTYPE)
    return _out_projection(x, m_sb, m_ch, m_fx, w_o)


def kernel(x, norm_g, w_in, b_forget, q_norm_ch, k_norm_ch, q_norm_fox, k_norm_fox, rel_bias, w_out):
    consts = _constants()
    for l in range(norm_g.shape[0]):
        x = _layer(x, norm_g[l], w_in[l], b_forget[l], q_norm_ch[l], k_norm_ch[l],
                   q_norm_fox[l], k_norm_fox[l], rel_bias[l], w_out[l], consts)
    return x
```

```python
import jax
import jax.numpy as jnp
import numpy as np
from jax import lax
from jax.experimental import pallas as pl
from jax.experimental.pallas import tpu as pltpu

D_MODEL = 1024
HEAD_DIM = 64
H_GROUP = 8
W_GROUP = H_GROUP * HEAD_DIM
N_GROUPS = 3
CHUNK = 64
N_LEFT_CHUNKS = 8
LEFT = N_LEFT_CHUNKS * CHUNK
REL_CLIP = 128
EPS = 1e-6
SCALE = HEAD_DIM ** -0.5
LOG2E = 1.4426950408889634

LANES = 128
MXU_DIM = 256
F_PAD = LANES
NEG = -1e30

MM_DTYPE = jnp.bfloat16
F32 = jnp.float32

TM_PROJ = 512
TQ_ATT = 1024
BF16_ROWS = 16
FOX_V_ROWS = HEAD_DIM + BF16_ROWS
FOX_SKEW = 5
SB_SKEW = 3
SB_KEYS_PER_TRIP = 512
SB_DEAD_LOG = -104.0
TK_SB = MXU_DIM
TQ_CHUNK = MXU_DIM
WIN_CHUNK = LEFT + TQ_CHUNK
TS_CHUNK = 2048
CUM_BLK = 512
VMEM_LIMIT = 56 * 1024 * 1024


def _dot(a, b):
    return jnp.dot(a, b, preferred_element_type=F32)


def _dot_nt(a, b):
    return lax.dot_general(a, b, (((1,), (1,)), ((), ())), preferred_element_type=F32)


def _split2(x):
    hi = x.astype(MM_DTYPE)
    lo = (x - hi.astype(F32)).astype(MM_DTYPE)
    return hi, lo


def _split3(x):
    hi = x.astype(MM_DTYPE)
    mid, lo = _split2(x - hi.astype(F32))
    return hi, mid, lo


def _head_mask(shape, j, axis):
    idx = lax.broadcasted_iota(jnp.int32, shape, axis)
    return (idx >= j * HEAD_DIM) & (idx < (j + 1) * HEAD_DIM)


def _norm_rows(y, gain_row, bd):
    hi, lo = _split2(y * y)
    parts = []
    for c in range(W_GROUP // MXU_DIM):
        sl = slice(c * MXU_DIM, (c + 1) * MXU_DIM)
        parts.append(_dot(hi[:, sl], bd) + _dot(lo[:, sl], bd))
    ssq = jnp.concatenate(parts, axis=1)
    return y * lax.rsqrt(ssq * (1.0 / HEAD_DIM) + EPS) * gain_row


def _norm_cols(yt, gain_col):
    y3 = yt.reshape(H_GROUP, HEAD_DIM, yt.shape[1])
    ssq = jnp.sum(y3 * y3, axis=1, keepdims=True)
    y3 = y3 * lax.rsqrt(ssq * (1.0 / HEAD_DIM) + EPS) * gain_col
    return y3.reshape(yt.shape)


def _proj_kernel(x_ref, ng_ref, wn_ref, wt_ref, wf_ref, grow_ref, gcol_ref, bd_ref,
                 qt_ref, k_ref, vt_ref, g_ref, f_ref):
    x = x_ref[...]
    h = x * lax.rsqrt(jnp.mean(x * x, axis=-1, keepdims=True) + EPS) * ng_ref[...]
    hb = h.astype(MM_DTYPE)
    bd = bd_ref[...]
    dt = qt_ref.dtype
    for grp in range(N_GROUPS):
        qt = _dot_nt(wt_ref[grp], hb)
        k = _dot(hb, wn_ref[grp])
        if grp > 0:
            qt = _norm_cols(qt, gcol_ref[grp - 1])
            k = _norm_rows(k, grow_ref[grp - 1], bd)
        qt_ref[grp] = (qt * SCALE).astype(dt)
        k_ref[grp] = k.astype(dt)
        vt_ref[grp] = _dot_nt(wt_ref[N_GROUPS + grp], hb).astype(dt)
        g_ref[grp] = _dot(hb, wn_ref[N_GROUPS + grp])
    f_ref[...] = _dot(hb, wf_ref[...])


def _projection(x, ng, w_nat, w_tr, w_f, grow, gcol, bd):
    b, s, d = x.shape
    tm = min(TM_PROJ, s)
    const = dict(pipeline_mode=pl.Buffered(1))
    nat_spec = pl.BlockSpec((N_GROUPS, None, tm, W_GROUP), lambda bi, i: (0, bi, i, 0))
    tr_spec = pl.BlockSpec((N_GROUPS, None, W_GROUP, tm), lambda bi, i: (0, bi, 0, i))
    nat_shape = jax.ShapeDtypeStruct((N_GROUPS, b, s, W_GROUP), MM_DTYPE)
    tr_shape = jax.ShapeDtypeStruct((N_GROUPS, b, W_GROUP, s), MM_DTYPE)
    return pl.pallas_call(
        _proj_kernel,
        grid=(b, s // tm),
        in_specs=[
            pl.BlockSpec((None, tm, d), lambda bi, i: (bi, i, 0)),
            pl.BlockSpec((1, d), lambda bi, i: (0, 0)),
            pl.BlockSpec((2 * N_GROUPS, d, W_GROUP), lambda bi, i: (0, 0, 0), **const),
            pl.BlockSpec((2 * N_GROUPS, W_GROUP, d), lambda bi, i: (0, 0, 0), **const),
            pl.BlockSpec((d, F_PAD), lambda bi, i: (0, 0), **const),
            pl.BlockSpec((2, 1, W_GROUP), lambda bi, i: (0, 0, 0)),
            pl.BlockSpec((2, HEAD_DIM, 1), lambda bi, i: (0, 0, 0)),
            pl.BlockSpec((MXU_DIM, MXU_DIM), lambda bi, i: (0, 0)),
        ],
        out_specs=[
            tr_spec, nat_spec, tr_spec,
            pl.BlockSpec((N_GROUPS, None, tm, W_GROUP), lambda bi, i: (0, bi, i, 0)),
            pl.BlockSpec((None, tm, F_PAD), lambda bi, i: (bi, i, 0)),
        ],
        out_shape=[
            tr_shape, nat_shape, tr_shape,
            jax.ShapeDtypeStruct((N_GROUPS, b, s, W_GROUP), F32),
            jax.ShapeDtypeStruct((b, s, F_PAD), F32),
        ],
        compiler_params=pltpu.CompilerParams(
            dimension_semantics=("arbitrary", "arbitrary"), vmem_limit_bytes=VMEM_LIMIT),
        name="proj",
    )(x, ng, w_nat, w_tr, w_f, grow, gcol, bd)


def _cum_kernel(f_ref, b_ref, l_ref, o_ref):
    z = f_ref[...] + b_ref[...]
    lf = jnp.minimum(z, 0.0) - jnp.log1p(jnp.exp(-jnp.abs(z)))
    low = l_ref[...]
    carry = jnp.zeros((1, F_PAD), F32)
    for c in range(lf.shape[0] // CUM_BLK):
        rows = slice(c * CUM_BLK, (c + 1) * CUM_BLK)
        hi, mid, lo = _split3(lf[rows])
        cs = (_dot(low, hi) + _dot(low, mid)) + _dot(low, lo) + carry
        o_ref[rows] = cs
        carry = cs[CUM_BLK - 1:CUM_BLK, :]


def _cum_forget(f, bias_row, low):
    b, s, _ = f.shape
    return pl.pallas_call(
        _cum_kernel,
        grid=(b,),
        in_specs=[
            pl.BlockSpec((None, s, F_PAD), lambda bi: (bi, 0, 0)),
            pl.BlockSpec((1, F_PAD), lambda bi: (0, 0)),
            pl.BlockSpec((CUM_BLK, CUM_BLK), lambda bi: (0, 0)),
        ],
        out_specs=pl.BlockSpec((None, s, F_PAD), lambda bi: (bi, 0, 0)),
        out_shape=jax.ShapeDtypeStruct((b, s, F_PAD), F32),
        compiler_params=pltpu.CompilerParams(dimension_semantics=("arbitrary",)),
        name="cum_forget",
    )(f, bias_row, low)


def _gated(g, o0, o1):
    lane = lax.broadcasted_iota(jnp.int32, o0.shape, 1)
    o = jnp.where(lane < HEAD_DIM, o0, o1)
    return o * (g * jax.nn.sigmoid(g))


def _gated_t(g, ot0, ot1):
    o = jnp.concatenate([ot0, ot1], axis=0).T
    return o * (g * jax.nn.sigmoid(g))


def _nat_tile(tq):
    return lambda b, p, i: (b, i, p)


def _t_specs(grp, tq, s):
    return [
        pl.BlockSpec((None, None, LANES, tq), lambda b, p, i: (grp, b, p, i)),
        pl.BlockSpec((None, None, s, LANES), lambda b, p, i: (grp, b, 0, p)),
        pl.BlockSpec((None, None, LANES, s), lambda b, p, i: (grp, b, p, 0)),
        pl.BlockSpec((None, None, tq, LANES), lambda b, p, i: (grp, b, i, p)),
    ]


_ATT_PARAMS = pltpu.CompilerParams(
    dimension_semantics=("arbitrary", "arbitrary", "arbitrary"), vmem_limit_bytes=VMEM_LIMIT)


def _neg_abs(x):
    bits = lax.bitcast_convert_type(x, jnp.uint32) | jnp.uint32(0x80000000)
    return lax.bitcast_convert_type(bits, F32)


def _run_skewed(units):
    pending = list(units)
    active = []
    while pending or active:
        if pending:
            active.append(pending.pop(0))
        for g in list(active):
            try:
                next(g)
            except StopIteration:
                active.remove(g)


def _masked_heads_t(qt):
    q = qt.astype(F32)
    return [jnp.where(_head_mask(q.shape, j, 0), q, 0.0).astype(MM_DTYPE) for j in range(2)]


def _sb_kernel(qt_ref, k_ref, vt_ref, g_ref, u_ref, o_ref, acc_ref, carry_ref):
    i = pl.program_id(2)
    t = qt_ref.shape[1]
    tk = u_ref.shape[0]
    n_sub = t // tk
    qm = _masked_heads_t(qt_ref[...])
    u = u_ref[...]
    acc_ref[...] = jnp.zeros_like(acc_ref)
    carry_ref[...] = jnp.zeros_like(carry_ref)

    def unit(j, k0, qc, diag):
        cols = slice(qc * tk, (qc + 1) * tk)
        z = _dot(k_ref[pl.ds(k0, tk), :], qm[j][:, cols])
        for _ in range(SB_SKEW):
            yield
        sp = jnp.maximum(z, 0.0) + jnp.log(1.0 + jnp.exp(_neg_abs(z)))
        if diag:
            strict = (lax.broadcasted_iota(jnp.int32, z.shape, 0)
                      < lax.broadcasted_iota(jnp.int32, z.shape, 1))
            sp = jnp.where(strict, sp, 0.0)
        hi, lo = _split2(sp)
        yield
        incl = _dot(u, hi) + _dot(u, lo)
        for _ in range(SB_SKEW):
            yield
        w = jnp.exp(z + incl)
        if diag:
            w = jnp.where(strict, w, 0.0)
        wb = w.astype(MM_DTYPE)
        yield
        pv = _dot(vt_ref[j * HEAD_DIM:(j + 1) * HEAD_DIM, pl.ds(k0, tk)], wb)
        carry = carry_ref[j, :, cols]
        acc_ref[j, :, cols] += jnp.exp(carry) * pv
        carry_ref[j, :, cols] = carry + incl[:1]

    def units(base, n_blocks, diag):
        out = []
        for c in reversed(range(n_blocks)):
            for qc in range(c if diag else 0, n_sub):
                for j in range(2):
                    out.append(unit(j, pl.multiple_of(base + c * tk, tk), qc, diag and qc == c))
        return out

    def live():
        return jnp.max(carry_ref[...]) >= SB_DEAD_LOG

    _run_skewed(units(i * t, n_sub, True))
    n_left = i * (t // SB_KEYS_PER_TRIP)

    def body(state):
        it, _ = state
        base = i * t - (it + 1) * SB_KEYS_PER_TRIP
        _run_skewed(units(base, SB_KEYS_PER_TRIP // tk, False))
        return it + 1, live()

    lax.while_loop(lambda st: (st[0] < n_left) & st[1], body, (0, live()))
    o_ref[...] = _gated_t(g_ref[...], acc_ref[0], acc_ref[1]).astype(o_ref.dtype)


def _sb_attention(qt, k, vt, g, u):
    _, b, _, s = qt.shape
    t = min(TQ_ATT, s)
    tk = u.shape[0]
    return pl.pallas_call(
        _sb_kernel,
        grid=(b, W_GROUP // LANES, s // t),
        in_specs=_t_specs(0, t, s) + [pl.BlockSpec((tk, tk), lambda b_, p, i: (0, 0))],
        out_specs=pl.BlockSpec((None, t, LANES), _nat_tile(t)),
        out_shape=jax.ShapeDtypeStruct((b, s, W_GROUP), MM_DTYPE),
        scratch_shapes=[pltpu.VMEM((2, HEAD_DIM, t), F32), pltpu.VMEM((2, 1, t), F32)],
        compiler_params=_ATT_PARAMS,
        name="sb_attention",
    )(qt, k, vt, g, u)


def _chunk_kernel(qt_ref, k_ref, vt_ref, g_ref, bias_ref, o_ref, kpad, vaug, ot_ref):
    i = pl.program_id(2)
    ts = qt_ref.shape[1]
    win, tq = bias_ref.shape[1:]
    s = k_ref.shape[0]
    n_ones = vaug.shape[1] - HEAD_DIM

    @pl.when(i == 0)
    def _():
        kpad[:LEFT] = jnp.zeros((LEFT, LANES), kpad.dtype)
        kpad[LEFT:] = k_ref[...]
        vt = vt_ref[...].astype(F32)
        for j in range(2):
            vaug[j, :, :LEFT] = jnp.zeros((vaug.shape[1], LEFT), vaug.dtype)
            vaug[j, :, LEFT:] = jnp.concatenate(
                [vt[j * HEAD_DIM:(j + 1) * HEAD_DIM], jnp.ones((n_ones, s), F32)],
                axis=0).astype(vaug.dtype)

    qm = _masked_heads_t(qt_ref[...])

    def unit(j, r, first_step):
        cols = slice(r * tq, (r + 1) * tq)
        r0 = pl.multiple_of(i * ts + r * tq, tq)
        zs = []
        for kb in range(win // tq):
            k0 = pl.multiple_of(r0 + kb * tq, tq)
            z = _dot(kpad[pl.ds(k0, tq), :], qm[j][:, cols]) + bias_ref[j, kb * tq:(kb + 1) * tq]
            lo_key = LEFT - r * tq - kb * tq
            if first_step and lo_key > 0:
                key = lax.broadcasted_iota(jnp.int32, z.shape, 0)
                z = jnp.where(key >= lo_key, z, NEG)
            zs.append(z)
            yield
        m = zs[0].max(axis=0, keepdims=True)
        for z in zs[1:]:
            m = jnp.maximum(m, z.max(axis=0, keepdims=True))
        ps = []
        for z in zs:
            ps.append(jnp.exp(z - m).astype(MM_DTYPE))
            yield
        pv = None
        for kb, p in enumerate(ps):
            k0 = pl.multiple_of(r0 + kb * tq, tq)
            term = _dot(vaug[j, :, pl.ds(k0, tq)], p)
            pv = term if pv is None else pv + term
        ot_ref[j * HEAD_DIM:(j + 1) * HEAD_DIM, cols] = pv[:HEAD_DIM] / pv[HEAD_DIM:HEAD_DIM + 1]

    def run(first_step):
        _run_skewed([unit(j, r, first_step) for r in range(ts // tq) for j in range(2)])

    @pl.when(i == 0)
    def _():
        run(True)

    @pl.when(i > 0)
    def _():
        run(False)

    o_ref[...] = _gated_t(g_ref[...], ot_ref[:HEAD_DIM], ot_ref[HEAD_DIM:]).astype(o_ref.dtype)


def _chunk_attention(qt, k, vt, g, bias):
    _, b, _, s = qt.shape
    ts = min(TS_CHUNK, s)
    return pl.pallas_call(
        _chunk_kernel,
        grid=(b, W_GROUP // LANES, s // ts),
        in_specs=_t_specs(1, ts, s) + [
            pl.BlockSpec((2, WIN_CHUNK, TQ_CHUNK), lambda b_, p, i: (p, 0, 0))],
        out_specs=pl.BlockSpec((None, ts, LANES), _nat_tile(ts)),
        out_shape=jax.ShapeDtypeStruct((b, s, W_GROUP), MM_DTYPE),
        scratch_shapes=[pltpu.VMEM((LEFT + s, LANES), MM_DTYPE),
                        pltpu.VMEM((2, FOX_V_ROWS, LEFT + s), MM_DTYPE),
                        pltpu.VMEM((LANES, ts), F32)],
        compiler_params=_ATT_PARAMS,
        name="chunk_attention",
    )(qt, k, vt, g, bias)


N_DIAG = WIN_CHUNK + TQ_CHUNK


def _bias_kernel(v_ref, o_ref):
    x = jnp.broadcast_to(v_ref[...], (WIN_CHUNK, N_DIAG))
    y = pltpu.roll(x, 1, 1, stride=1, stride_axis=0)
    t = y[:, WIN_CHUNK:]
    c = lax.broadcasted_iota(jnp.int32, t.shape, 0)
    r = lax.broadcasted_iota(jnp.int32, t.shape, 1)
    band = c - (r - (r & (CHUNK - 1)))
    o_ref[...] = jnp.where((band >= 0) & (band < LEFT + CHUNK), t, NEG)


def _chunk_bias_table(rel_bias):
    kk = np.arange(N_DIAG - 1)
    rel = kk - (WIN_CHUNK - 1) + LEFT
    vec = rel_bias[:, np.clip(rel, -REL_CLIP, REL_CLIP) + REL_CLIP].astype(F32)
    vec = jnp.pad(vec, ((0, 0), (0, 1)))[:, None, :]
    h = vec.shape[0]
    return pl.pallas_call(
        _bias_kernel,
        grid=(h,),
        in_specs=[pl.BlockSpec((None, 1, N_DIAG), lambda i: (i, 0, 0))],
        out_specs=pl.BlockSpec((None, WIN_CHUNK, TQ_CHUNK), lambda i: (i, 0, 0)),
        out_shape=jax.ShapeDtypeStruct((h, WIN_CHUNK, TQ_CHUNK), F32),
        compiler_params=pltpu.CompilerParams(dimension_semantics=("arbitrary",)),
        name="chunk_bias",
    )(vec)


def _fox_kernel(qt_ref, k_ref, vt_ref, g_ref, c_ref, o_ref, kaug, vaug, m_ref, acc_ref):
    i = pl.program_id(2)
    p = pl.program_id(1)
    t = qt_ref.shape[1]
    aug0 = [HEAD_DIM * (1 - j) for j in range(2)]
    n_ones = vaug.shape[1] - HEAD_DIM

    @pl.when(i == 0)
    def _():
        k = k_ref[...].astype(F32)
        vt = vt_ref[...].astype(F32)
        pieces = _split3(-c_ref[...])
        row = lax.broadcasted_iota(jnp.int32, (LANES, LANES), 0)
        col = lax.broadcasted_iota(jnp.int32, (LANES, LANES), 1)
        for j in range(2):
            aug = None
            for c, piece in enumerate(pieces):
                place = jnp.where((row == 2 * p + j) & (col == aug0[j] + c), 1.0, 0.0)
                term = _dot(piece, place.astype(MM_DTYPE))
                aug = term if aug is None else aug + term
            kaug[j] = jnp.where(_head_mask(k.shape, j, 1), k, aug).astype(kaug.dtype)
            vaug[j] = jnp.concatenate(
                [vt[j * HEAD_DIM:(j + 1) * HEAD_DIM], jnp.ones((n_ones, vt.shape[1]), F32)],
                axis=0).astype(vaug.dtype)

    q = qt_ref[...].astype(F32)
    row = lax.broadcasted_iota(jnp.int32, q.shape, 0)
    qa = []
    for j in range(2):
        ones = (row >= aug0[j]) & (row < aug0[j] + 3)
        qa.append(jnp.where(_head_mask(q.shape, j, 0), q, jnp.where(ones, 1.0, 0.0)).astype(MM_DTYPE))
    m_ref[...] = jnp.full(m_ref.shape, NEG, F32)
    acc_ref[...] = jnp.zeros_like(acc_ref)

    tk = TK_SB
    n_sub = t // tk

    def unit(j, k0, qc, diag):
        cols = slice(qc * tk, (qc + 1) * tk)
        z = _dot(kaug[j, pl.ds(k0, tk), :], qa[j][:, cols])
        for _ in range(FOX_SKEW):
            yield
        if diag:
            causal = (lax.broadcasted_iota(jnp.int32, z.shape, 0)
                      <= lax.broadcasted_iota(jnp.int32, z.shape, 1))
            z = jnp.where(causal, z, NEG)
        m = m_ref[j, :, cols]
        m_new = jnp.maximum(m, jnp.max(z, axis=0, keepdims=True))
        pr = jnp.exp(z - m_new).astype(MM_DTYPE)
        alpha = jnp.exp(m - m_new)
        m_ref[j, :, cols] = m_new
        yield
        pv = _dot(vaug[j, :, pl.ds(k0, tk)], pr)
        acc_ref[j, :, cols] = alpha * acc_ref[j, :, cols] + pv

    def units(base, diag):
        out = []
        for c in range(n_sub):
            for qc in range(c if diag else 0, n_sub):
                for j in range(2):
                    out.append(unit(j, pl.multiple_of(base + c * tk, tk), qc, diag and qc == c))
        return out

    def body(kj, _):
        _run_skewed(units(kj * t, False))
        return 0

    lax.fori_loop(0, i, body, 0)
    _run_skewed(units(i * t, True))
    outs = [acc_ref[j, :HEAD_DIM] / acc_ref[j, HEAD_DIM:HEAD_DIM + 1] for j in range(2)]
    o_ref[...] = _gated_t(g_ref[...], outs[0], outs[1]).astype(o_ref.dtype)


def _fox_attention(qt, k, vt, g, cum):
    _, b, _, s = qt.shape
    t = min(TQ_ATT, s)
    return pl.pallas_call(
        _fox_kernel,
        grid=(b, W_GROUP // LANES, s // t),
        in_specs=_t_specs(2, t, s) + [
            pl.BlockSpec((None, s, F_PAD), lambda b_, p, i: (b_, 0, 0))],
        out_specs=pl.BlockSpec((None, t, LANES), _nat_tile(t)),
        out_shape=jax.ShapeDtypeStruct((b, s, W_GROUP), MM_DTYPE),
        scratch_shapes=[pltpu.VMEM((2, s, LANES), MM_DTYPE),
                        pltpu.VMEM((2, FOX_V_ROWS, s), MM_DTYPE),
                        pltpu.VMEM((2, 1, t), F32), pltpu.VMEM((2, FOX_V_ROWS, t), F32)],
        compiler_params=_ATT_PARAMS,
        name="fox_attention",
    )(qt, k, vt, g, cum)


def _out_kernel(x_ref, a_ref, b_ref, c_ref, w_ref, o_ref):
    acc = _dot(a_ref[...], w_ref[0]) + _dot(b_ref[...], w_ref[1]) + _dot(c_ref[...], w_ref[2])
    o_ref[...] = x_ref[...] + acc


def _out_projection(x, ma, mb, mc, w):
    b, s, d = x.shape
    tm = min(TM_PROJ, s)
    mix = pl.BlockSpec((None, tm, W_GROUP), lambda bi, i: (bi, i, 0))
    return pl.pallas_call(
        _out_kernel,
        grid=(b, s // tm),
        in_specs=[pl.BlockSpec((None, tm, d), lambda bi, i: (bi, i, 0)), mix, mix, mix,
                  pl.BlockSpec((N_GROUPS, W_GROUP, d), lambda bi, i: (0, 0, 0))],
        out_specs=pl.BlockSpec((None, tm, d), lambda bi, i: (bi, i, 0)),
        out_shape=jax.ShapeDtypeStruct(x.shape, x.dtype),
        compiler_params=pltpu.CompilerParams(
            dimension_semantics=("arbitrary", "arbitrary"), vmem_limit_bytes=VMEM_LIMIT),
        name="out_proj",
    )(x, ma, mb, mc, w)


def _constants():
    r = np.arange(MXU_DIM)
    bd = (r[:, None] // HEAD_DIM == r[None, :] // HEAD_DIM).astype(np.float32)
    r = np.arange(TK_SB)
    u_sb = -(r[None, :] >= r[:, None]).astype(np.float32)
    r = np.arange(CUM_BLK)
    low = (r[None, :] <= r[:, None]).astype(np.float32)
    return (jnp.asarray(bd, MM_DTYPE), jnp.asarray(u_sb, MM_DTYPE), jnp.asarray(low, MM_DTYPE))


def _layer(x, ng, w_in, b_forget, qn_ch, kn_ch, qn_fox, kn_fox, rel_bias, w_out, consts):
    bd, u_sb, low = consts
    d = x.shape[-1]
    w4 = w_in[:, :N_GROUPS * 4 * W_GROUP].reshape(d, N_GROUPS, 4, W_GROUP)
    w_nat = jnp.transpose(w4[:, :, (1, 3), :], (2, 1, 0, 3)).reshape(2 * N_GROUPS, d, W_GROUP)
    w_tr = jnp.transpose(w4[:, :, (0, 2), :], (2, 1, 3, 0)).reshape(2 * N_GROUPS, W_GROUP, d)
    w_f = jnp.pad(w_in[:, N_GROUPS * 4 * W_GROUP:], ((0, 0), (0, F_PAD - H_GROUP))).astype(MM_DTYPE)
    grow = jnp.stack([jnp.tile(kn_ch, H_GROUP), jnp.tile(kn_fox, H_GROUP)])[:, None, :].astype(F32)
    gcol = jnp.stack([qn_ch, qn_fox])[:, :, None].astype(F32)
    bias_row = jnp.pad(b_forget.astype(F32), (0, F_PAD - H_GROUP))[None, :]

    qt, k, vt, g, f = _projection(x, ng[None, :], w_nat.astype(MM_DTYPE), w_tr.astype(MM_DTYPE),
                                  w_f, grow, gcol, bd)
    cum = _cum_forget(f, bias_row, low)
    m_sb = _sb_attention(qt, k, vt, g, u_sb)
    m_ch = _chunk_attention(qt, k, vt, g, _chunk_bias_table(rel_bias))
    m_fx = _fox_attention(qt, k, vt, g, cum)
    w_o = w_out.reshape(N_GROUPS, W_GROUP, d).astype(MM_DTYPE)
    return _out_projection(x, m_sb, m_ch, m_fx, w_o)


def kernel(x, norm_g, w_in, b_forget, q_norm_ch, k_norm_ch, q_norm_fox, k_norm_fox, rel_bias, w_out):
    consts = _constants()
    for l in range(norm_g.shape[0]):
        x = _layer(x, norm_g[l], w_in[l], b_forget[l], q_norm_ch[l], k_norm_ch[l],
                   q_norm_fox[l], k_norm_fox[l], rel_bias[l], w_out[l], consts)
    return x
```

```python
import jax
import jax.numpy as jnp
import numpy as np
from jax import lax
from jax.experimental import pallas as pl
from jax.experimental.pallas import tpu as pltpu

D_MODEL = 1024
HEAD_DIM = 64
H_GROUP = 8
W_GROUP = H_GROUP * HEAD_DIM
N_GROUPS = 3
CHUNK = 64
N_LEFT_CHUNKS = 8
LEFT = N_LEFT_CHUNKS * CHUNK
REL_CLIP = 128
EPS = 1e-6
SCALE = HEAD_DIM ** -0.5
LOG2E = 1.4426950408889634

LANES = 128
MXU_DIM = 256
F_PAD = LANES
NEG = -1e30

MM_DTYPE = jnp.bfloat16
F32 = jnp.float32

TM_PROJ = 512
TQ_ATT = 1024
BF16_ROWS = 16
FOX_V_ROWS = HEAD_DIM + BF16_ROWS
FOX_SKEW = 5
SB_SKEW = 3
SB_STATIC_DIAGS = 2
SB_DEAD_LOG = -104.0
TK_SB = MXU_DIM
TQ_CHUNK = MXU_DIM
WIN_CHUNK = LEFT + TQ_CHUNK
TS_CHUNK = 2048
CUM_BLK = 512
VMEM_LIMIT = 56 * 1024 * 1024


def _dot(a, b):
    return jnp.dot(a, b, preferred_element_type=F32)


def _dot_nt(a, b):
    return lax.dot_general(a, b, (((1,), (1,)), ((), ())), preferred_element_type=F32)


def _split2(x):
    hi = x.astype(MM_DTYPE)
    lo = (x - hi.astype(F32)).astype(MM_DTYPE)
    return hi, lo


def _split3(x):
    hi = x.astype(MM_DTYPE)
    mid, lo = _split2(x - hi.astype(F32))
    return hi, mid, lo


def _head_mask(shape, j, axis):
    idx = lax.broadcasted_iota(jnp.int32, shape, axis)
    return (idx >= j * HEAD_DIM) & (idx < (j + 1) * HEAD_DIM)


def _norm_rows(y, gain_row, bd):
    hi, lo = _split2(y * y)
    parts = []
    for c in range(W_GROUP // MXU_DIM):
        sl = slice(c * MXU_DIM, (c + 1) * MXU_DIM)
        parts.append(_dot(hi[:, sl], bd) + _dot(lo[:, sl], bd))
    ssq = jnp.concatenate(parts, axis=1)
    return y * lax.rsqrt(ssq * (1.0 / HEAD_DIM) + EPS) * gain_row


def _norm_cols(yt, gain_col):
    y3 = yt.reshape(H_GROUP, HEAD_DIM, yt.shape[1])
    ssq = jnp.sum(y3 * y3, axis=1, keepdims=True)
    y3 = y3 * lax.rsqrt(ssq * (1.0 / HEAD_DIM) + EPS) * gain_col
    return y3.reshape(yt.shape)


def _proj_kernel(x_ref, ng_ref, wn_ref, wt_ref, wf_ref, grow_ref, gcol_ref, bd_ref,
                 qt_ref, k_ref, vt_ref, g_ref, f_ref):
    x = x_ref[...]
    h = x * lax.rsqrt(jnp.mean(x * x, axis=-1, keepdims=True) + EPS) * ng_ref[...]
    hb = h.astype(MM_DTYPE)
    bd = bd_ref[...]
    dt = qt_ref.dtype
    for grp in range(N_GROUPS):
        qt = _dot_nt(wt_ref[grp], hb)
        k = _dot(hb, wn_ref[grp])
        if grp > 0:
            qt = _norm_cols(qt, gcol_ref[grp - 1])
            k = _norm_rows(k, grow_ref[grp - 1], bd)
        qt_ref[grp] = (qt * SCALE).astype(dt)
        k_ref[grp] = k.astype(dt)
        vt_ref[grp] = _dot_nt(wt_ref[N_GROUPS + grp], hb).astype(dt)
        g_ref[grp] = _dot(hb, wn_ref[N_GROUPS + grp])
    f_ref[...] = _dot(hb, wf_ref[...])


def _projection(x, ng, w_nat, w_tr, w_f, grow, gcol, bd):
    b, s, d = x.shape
    tm = min(TM_PROJ, s)
    const = dict(pipeline_mode=pl.Buffered(1))
    nat_spec = pl.BlockSpec((N_GROUPS, None, tm, W_GROUP), lambda bi, i: (0, bi, i, 0))
    tr_spec = pl.BlockSpec((N_GROUPS, None, W_GROUP, tm), lambda bi, i: (0, bi, 0, i))
    nat_shape = jax.ShapeDtypeStruct((N_GROUPS, b, s, W_GROUP), MM_DTYPE)
    tr_shape = jax.ShapeDtypeStruct((N_GROUPS, b, W_GROUP, s), MM_DTYPE)
    return pl.pallas_call(
        _proj_kernel,
        grid=(b, s // tm),
        in_specs=[
            pl.BlockSpec((None, tm, d), lambda bi, i: (bi, i, 0)),
            pl.BlockSpec((1, d), lambda bi, i: (0, 0)),
            pl.BlockSpec((2 * N_GROUPS, d, W_GROUP), lambda bi, i: (0, 0, 0), **const),
            pl.BlockSpec((2 * N_GROUPS, W_GROUP, d), lambda bi, i: (0, 0, 0), **const),
            pl.BlockSpec((d, F_PAD), lambda bi, i: (0, 0), **const),
            pl.BlockSpec((2, 1, W_GROUP), lambda bi, i: (0, 0, 0)),
            pl.BlockSpec((2, HEAD_DIM, 1), lambda bi, i: (0, 0, 0)),
            pl.BlockSpec((MXU_DIM, MXU_DIM), lambda bi, i: (0, 0)),
        ],
        out_specs=[
            tr_spec, nat_spec, tr_spec,
            pl.BlockSpec((N_GROUPS, None, tm, W_GROUP), lambda bi, i: (0, bi, i, 0)),
            pl.BlockSpec((None, tm, F_PAD), lambda bi, i: (bi, i, 0)),
        ],
        out_shape=[
            tr_shape, nat_shape, tr_shape,
            jax.ShapeDtypeStruct((N_GROUPS, b, s, W_GROUP), F32),
            jax.ShapeDtypeStruct((b, s, F_PAD), F32),
        ],
        compiler_params=pltpu.CompilerParams(
            dimension_semantics=("arbitrary", "arbitrary"), vmem_limit_bytes=VMEM_LIMIT),
        name="proj",
    )(x, ng, w_nat, w_tr, w_f, grow, gcol, bd)


def _cum_kernel(f_ref, b_ref, l_ref, o_ref):
    z = f_ref[...] + b_ref[...]
    lf = jnp.minimum(z, 0.0) - jnp.log1p(jnp.exp(-jnp.abs(z)))
    low = l_ref[...]
    carry = jnp.zeros((1, F_PAD), F32)
    for c in range(lf.shape[0] // CUM_BLK):
        rows = slice(c * CUM_BLK, (c + 1) * CUM_BLK)
        hi, mid, lo = _split3(lf[rows])
        cs = (_dot(low, hi) + _dot(low, mid)) + _dot(low, lo) + carry
        o_ref[rows] = cs
        carry = cs[CUM_BLK - 1:CUM_BLK, :]


def _cum_forget(f, bias_row, low):
    b, s, _ = f.shape
    return pl.pallas_call(
        _cum_kernel,
        grid=(b,),
        in_specs=[
            pl.BlockSpec((None, s, F_PAD), lambda bi: (bi, 0, 0)),
            pl.BlockSpec((1, F_PAD), lambda bi: (0, 0)),
            pl.BlockSpec((CUM_BLK, CUM_BLK), lambda bi: (0, 0)),
        ],
        out_specs=pl.BlockSpec((None, s, F_PAD), lambda bi: (bi, 0, 0)),
        out_shape=jax.ShapeDtypeStruct((b, s, F_PAD), F32),
        compiler_params=pltpu.CompilerParams(dimension_semantics=("arbitrary",)),
        name="cum_forget",
    )(f, bias_row, low)


def _gated(g, o0, o1):
    lane = lax.broadcasted_iota(jnp.int32, o0.shape, 1)
    o = jnp.where(lane < HEAD_DIM, o0, o1)
    return o * (g * jax.nn.sigmoid(g))


def _gated_t(g, ot0, ot1):
    o = jnp.concatenate([ot0, ot1], axis=0).T
    return o * (g * jax.nn.sigmoid(g))


def _nat_tile(tq):
    return lambda b, p, i: (b, i, p)


def _t_specs(grp, tq, s):
    return [
        pl.BlockSpec((None, None, LANES, tq), lambda b, p, i: (grp, b, p, i)),
        pl.BlockSpec((None, None, s, LANES), lambda b, p, i: (grp, b, 0, p)),
        pl.BlockSpec((None, None, LANES, s), lambda b, p, i: (grp, b, p, 0)),
        pl.BlockSpec((None, None, tq, LANES), lambda b, p, i: (grp, b, i, p)),
    ]


_ATT_PARAMS = pltpu.CompilerParams(
    dimension_semantics=("arbitrary", "arbitrary", "arbitrary"), vmem_limit_bytes=VMEM_LIMIT)


def _neg_abs(x):
    bits = lax.bitcast_convert_type(x, jnp.uint32) | jnp.uint32(0x80000000)
    return lax.bitcast_convert_type(bits, F32)


def _run_skewed(units):
    pending = list(units)
    active = []
    while pending or active:
        if pending:
            active.append(pending.pop(0))
        for g in list(active):
            try:
                next(g)
            except StopIteration:
                active.remove(g)


def _masked_heads_t(qt):
    q = qt.astype(F32)
    return [jnp.where(_head_mask(q.shape, j, 0), q, 0.0).astype(MM_DTYPE) for j in range(2)]


def _sb_kernel(qt_ref, k_ref, vt_ref, g_ref, u_ref, o_ref, acc_ref, carry_ref):
    t = qt_ref.shape[1]
    tk = u_ref.shape[0]
    n_sub = t // tk
    qm = _masked_heads_t(qt_ref[...])
    u = u_ref[...]
    acc_ref[...] = jnp.zeros_like(acc_ref)
    carry_ref[...] = jnp.zeros_like(carry_ref)

    def unit(j, kb, qc, diag):
        cols = slice(qc * tk, (qc + 1) * tk)
        static = isinstance(kb, int)
        k0 = kb * tk if static else pl.multiple_of(jnp.maximum(kb, 0) * tk, tk)
        z = _dot(k_ref[pl.ds(k0, tk), :], qm[j][:, cols])
        for _ in range(SB_SKEW):
            yield
        sp = jnp.maximum(z, 0.0) + jnp.log(1.0 + jnp.exp(_neg_abs(z)))
        if diag:
            strict = (lax.broadcasted_iota(jnp.int32, z.shape, 0)
                      < lax.broadcasted_iota(jnp.int32, z.shape, 1))
            sp = jnp.where(strict, sp, 0.0)
        hi, lo = _split2(sp)
        yield
        incl = _dot(u, hi) + _dot(u, lo)
        for _ in range(SB_SKEW):
            yield
        w = jnp.exp(z + incl)
        if diag:
            w = jnp.where(strict, w, 0.0)
        wb = w.astype(MM_DTYPE)
        yield
        pv = _dot(vt_ref[j * HEAD_DIM:(j + 1) * HEAD_DIM, pl.ds(k0, tk)], wb)
        carry = carry_ref[j, :, cols]
        scale = jnp.exp(carry)
        step = incl[:1]
        if not static:
            scale = jnp.where(kb >= 0, scale, 0.0)
            step = jnp.where(kb >= 0, step, 0.0)
        acc_ref[j, :, cols] += scale * pv
        carry_ref[j, :, cols] = carry + step

    def diagonal(d, first_qc, masked=False):
        return [unit(j, qc - d, qc, masked) for qc in range(first_qc, n_sub) for j in range(2)]

    def live(d):
        lane = lax.broadcasted_iota(jnp.int32, carry_ref.shape, 2)
        return jnp.max(jnp.where(lane >= d * tk, carry_ref[...], NEG)) >= SB_DEAD_LOG

    n_static = min(SB_STATIC_DIAGS, n_sub)
    _run_skewed([un for d in range(n_static) for un in diagonal(d, d, masked=d == 0)])

    def body(state):
        d, _ = state
        _run_skewed(diagonal(d, n_static))
        return d + 1, live(d + 1)

    lax.while_loop(lambda st: (st[0] < n_sub) & st[1], body, (n_static, live(n_static)))
    o_ref[...] = _gated_t(g_ref[...], acc_ref[0], acc_ref[1]).astype(o_ref.dtype)


def _sb_attention(qt, k, vt, g, u):
    _, b, _, s = qt.shape
    t = s
    tk = u.shape[0]
    return pl.pallas_call(
        _sb_kernel,
        grid=(b, W_GROUP // LANES, s // t),
        in_specs=_t_specs(0, t, s) + [pl.BlockSpec((tk, tk), lambda b_, p, i: (0, 0))],
        out_specs=pl.BlockSpec((None, t, LANES), _nat_tile(t)),
        out_shape=jax.ShapeDtypeStruct((b, s, W_GROUP), MM_DTYPE),
        scratch_shapes=[pltpu.VMEM((2, HEAD_DIM, t), F32), pltpu.VMEM((2, 1, t), F32)],
        compiler_params=_ATT_PARAMS,
        name="sb_attention",
    )(qt, k, vt, g, u)


def _chunk_kernel(qt_ref, k_ref, vt_ref, g_ref, bias_ref, o_ref, kpad, vaug, ot_ref):
    i = pl.program_id(2)
    ts = qt_ref.shape[1]
    win, tq = bias_ref.shape[1:]
    s = k_ref.shape[0]
    n_ones = vaug.shape[1] - HEAD_DIM

    @pl.when(i == 0)
    def _():
        kpad[:LEFT] = jnp.zeros((LEFT, LANES), kpad.dtype)
        kpad[LEFT:] = k_ref[...]
        vt = vt_ref[...].astype(F32)
        for j in range(2):
            vaug[j, :, :LEFT] = jnp.zeros((vaug.shape[1], LEFT), vaug.dtype)
            vaug[j, :, LEFT:] = jnp.concatenate(
                [vt[j * HEAD_DIM:(j + 1) * HEAD_DIM], jnp.ones((n_ones, s), F32)],
                axis=0).astype(vaug.dtype)

    qm = _masked_heads_t(qt_ref[...])

    def unit(j, r, first_step):
        cols = slice(r * tq, (r + 1) * tq)
        r0 = pl.multiple_of(i * ts + r * tq, tq)
        zs = []
        for kb in range(win // tq):
            k0 = pl.multiple_of(r0 + kb * tq, tq)
            z = _dot(kpad[pl.ds(k0, tq), :], qm[j][:, cols]) + bias_ref[j, kb * tq:(kb + 1) * tq]
            lo_key = LEFT - r * tq - kb * tq
            if first_step and lo_key > 0:
                key = lax.broadcasted_iota(jnp.int32, z.shape, 0)
                z = jnp.where(key >= lo_key, z, NEG)
            zs.append(z)
            yield
        m = zs[0].max(axis=0, keepdims=True)
        for z in zs[1:]:
            m = jnp.maximum(m, z.max(axis=0, keepdims=True))
        ps = []
        for z in zs:
            ps.append(jnp.exp(z - m).astype(MM_DTYPE))
            yield
        pv = None
        for kb, p in enumerate(ps):
            k0 = pl.multiple_of(r0 + kb * tq, tq)
            term = _dot(vaug[j, :, pl.ds(k0, tq)], p)
            pv = term if pv is None else pv + term
        ot_ref[j * HEAD_DIM:(j + 1) * HEAD_DIM, cols] = pv[:HEAD_DIM] / pv[HEAD_DIM:HEAD_DIM + 1]

    def run(first_step):
        _run_skewed([unit(j, r, first_step) for r in range(ts // tq) for j in range(2)])

    @pl.when(i == 0)
    def _():
        run(True)

    @pl.when(i > 0)
    def _():
        run(False)

    o_ref[...] = _gated_t(g_ref[...], ot_ref[:HEAD_DIM], ot_ref[HEAD_DIM:]).astype(o_ref.dtype)


def _chunk_attention(qt, k, vt, g, bias):
    _, b, _, s = qt.shape
    ts = min(TS_CHUNK, s)
    return pl.pallas_call(
        _chunk_kernel,
        grid=(b, W_GROUP // LANES, s // ts),
        in_specs=_t_specs(1, ts, s) + [
            pl.BlockSpec((2, WIN_CHUNK, TQ_CHUNK), lambda b_, p, i: (p, 0, 0))],
        out_specs=pl.BlockSpec((None, ts, LANES), _nat_tile(ts)),
        out_shape=jax.ShapeDtypeStruct((b, s, W_GROUP), MM_DTYPE),
        scratch_shapes=[pltpu.VMEM((LEFT + s, LANES), MM_DTYPE),
                        pltpu.VMEM((2, FOX_V_ROWS, LEFT + s), MM_DTYPE),
                        pltpu.VMEM((LANES, ts), F32)],
        compiler_params=_ATT_PARAMS,
        name="chunk_attention",
    )(qt, k, vt, g, bias)


N_DIAG = WIN_CHUNK + TQ_CHUNK


def _bias_kernel(v_ref, o_ref):
    x = jnp.broadcast_to(v_ref[...], (WIN_CHUNK, N_DIAG))
    y = pltpu.roll(x, 1, 1, stride=1, stride_axis=0)
    t = y[:, WIN_CHUNK:]
    c = lax.broadcasted_iota(jnp.int32, t.shape, 0)
    r = lax.broadcasted_iota(jnp.int32, t.shape, 1)
    band = c - (r - (r & (CHUNK - 1)))
    o_ref[...] = jnp.where((band >= 0) & (band < LEFT + CHUNK), t, NEG)


def _chunk_bias_table(rel_bias):
    kk = np.arange(N_DIAG - 1)
    rel = kk - (WIN_CHUNK - 1) + LEFT
    vec = rel_bias[:, np.clip(rel, -REL_CLIP, REL_CLIP) + REL_CLIP].astype(F32)
    vec = jnp.pad(vec, ((0, 0), (0, 1)))[:, None, :]
    h = vec.shape[0]
    return pl.pallas_call(
        _bias_kernel,
        grid=(h,),
        in_specs=[pl.BlockSpec((None, 1, N_DIAG), lambda i: (i, 0, 0))],
        out_specs=pl.BlockSpec((None, WIN_CHUNK, TQ_CHUNK), lambda i: (i, 0, 0)),
        out_shape=jax.ShapeDtypeStruct((h, WIN_CHUNK, TQ_CHUNK), F32),
        compiler_params=pltpu.CompilerParams(dimension_semantics=("arbitrary",)),
        name="chunk_bias",
    )(vec)


def _fox_kernel(qt_ref, k_ref, vt_ref, g_ref, c_ref, o_ref, kaug, vaug, m_ref, acc_ref):
    i = pl.program_id(2)
    p = pl.program_id(1)
    t = qt_ref.shape[1]
    aug0 = [HEAD_DIM * (1 - j) for j in range(2)]
    n_ones = vaug.shape[1] - HEAD_DIM

    @pl.when(i == 0)
    def _():
        k = k_ref[...].astype(F32)
        vt = vt_ref[...].astype(F32)
        pieces = jnp.concatenate(_split3(-c_ref[...]), axis=1)
        row = lax.broadcasted_iota(jnp.int32, (3 * LANES, 2 * LANES), 0)
        col = lax.broadcasted_iota(jnp.int32, (3 * LANES, 2 * LANES), 1)
        place = None
        for j in range(2):
            for c in range(3):
                hit = (row == c * LANES + 2 * p + j) & (col == j * LANES + aug0[j] + c)
                place = hit if place is None else place | hit
        aug = _dot(pieces, jnp.where(place, 1.0, 0.0).astype(MM_DTYPE))
        for j in range(2):
            kaug[j] = jnp.where(_head_mask(k.shape, j, 1), k,
                                aug[:, j * LANES:(j + 1) * LANES]).astype(kaug.dtype)
            vaug[j] = jnp.concatenate(
                [vt[j * HEAD_DIM:(j + 1) * HEAD_DIM], jnp.ones((n_ones, vt.shape[1]), F32)],
                axis=0).astype(vaug.dtype)

    q = qt_ref[...].astype(F32)
    row = lax.broadcasted_iota(jnp.int32, q.shape, 0)
    qa = []
    for j in range(2):
        ones = (row >= aug0[j]) & (row < aug0[j] + 3)
        qa.append(jnp.where(_head_mask(q.shape, j, 0), q, jnp.where(ones, 1.0, 0.0)).astype(MM_DTYPE))
    m_ref[...] = jnp.full(m_ref.shape, NEG, F32)
    acc_ref[...] = jnp.zeros_like(acc_ref)

    tk = TK_SB
    n_sub = t // tk

    def unit(j, k0, qc, diag):
        cols = slice(qc * tk, (qc + 1) * tk)
        z = _dot(kaug[j, pl.ds(k0, tk), :], qa[j][:, cols])
        for _ in range(FOX_SKEW):
            yield
        if diag:
            causal = (lax.broadcasted_iota(jnp.int32, z.shape, 0)
                      <= lax.broadcasted_iota(jnp.int32, z.shape, 1))
            z = jnp.where(causal, z, NEG)
        m = m_ref[j, :, cols]
        m_new = jnp.maximum(m, jnp.max(z, axis=0, keepdims=True))
        pr = jnp.exp(z - m_new).astype(MM_DTYPE)
        alpha = jnp.exp(m - m_new)
        m_ref[j, :, cols] = m_new
        yield
        pv = _dot(vaug[j, :, pl.ds(k0, tk)], pr)
        acc_ref[j, :, cols] = alpha * acc_ref[j, :, cols] + pv

    def units(base, diag):
        out = []
        for c in range(n_sub):
            for qc in range(c if diag else 0, n_sub):
                for j in range(2):
                    out.append(unit(j, pl.multiple_of(base + c * tk, tk), qc, diag and qc == c))
        return out

    def body(kj, _):
        _run_skewed(units(kj * t, False))
        return 0

    lax.fori_loop(0, i, body, 0)
    _run_skewed(units(i * t, True))
    outs = [acc_ref[j, :HEAD_DIM] / acc_ref[j, HEAD_DIM:HEAD_DIM + 1] for j in range(2)]
    o_ref[...] = _gated_t(g_ref[...], outs[0], outs[1]).astype(o_ref.dtype)


def _fox_attention(qt, k, vt, g, cum):
    _, b, _, s = qt.shape
    t = min(TQ_ATT, s)
    return pl.pallas_call(
        _fox_kernel,
        grid=(b, W_GROUP // LANES, s // t),
        in_specs=_t_specs(2, t, s) + [
            pl.BlockSpec((None, s, F_PAD), lambda b_, p, i: (b_, 0, 0))],
        out_specs=pl.BlockSpec((None, t, LANES), _nat_tile(t)),
        out_shape=jax.ShapeDtypeStruct((b, s, W_GROUP), MM_DTYPE),
        scratch_shapes=[pltpu.VMEM((2, s, LANES), MM_DTYPE),
                        pltpu.VMEM((2, FOX_V_ROWS, s), MM_DTYPE),
                        pltpu.VMEM((2, 1, t), F32), pltpu.VMEM((2, FOX_V_ROWS, t), F32)],
        compiler_params=_ATT_PARAMS,
        name="fox_attention",
    )(qt, k, vt, g, cum)


def _out_kernel(x_ref, a_ref, b_ref, c_ref, w_ref, o_ref):
    acc = _dot(a_ref[...], w_ref[0]) + _dot(b_ref[...], w_ref[1]) + _dot(c_ref[...], w_ref[2])
    o_ref[...] = x_ref[...] + acc


def _out_projection(x, ma, mb, mc, w):
    b, s, d = x.shape
    tm = min(TM_PROJ, s)
    mix = pl.BlockSpec((None, tm, W_GROUP), lambda bi, i: (bi, i, 0))
    return pl.pallas_call(
        _out_kernel,
        grid=(b, s // tm),
        in_specs=[pl.BlockSpec((None, tm, d), lambda bi, i: (bi, i, 0)), mix, mix, mix,
                  pl.BlockSpec((N_GROUPS, W_GROUP, d), lambda bi, i: (0, 0, 0))],
        out_specs=pl.BlockSpec((None, tm, d), lambda bi, i: (bi, i, 0)),
        out_shape=jax.ShapeDtypeStruct(x.shape, x.dtype),
        compiler_params=pltpu.CompilerParams(
            dimension_semantics=("arbitrary", "arbitrary"), vmem_limit_bytes=VMEM_LIMIT),
        name="out_proj",
    )(x, ma, mb, mc, w)


def _constants():
    r = np.arange(MXU_DIM)
    bd = (r[:, None] // HEAD_DIM == r[None, :] // HEAD_DIM).astype(np.float32)
    r = np.arange(TK_SB)
    u_sb = -(r[None, :] >= r[:, None]).astype(np.float32)
    r = np.arange(CUM_BLK)
    low = (r[None, :] <= r[:, None]).astype(np.float32)
    return (jnp.asarray(bd, MM_DTYPE), jnp.asarray(u_sb, MM_DTYPE), jnp.asarray(low, MM_DTYPE))


def _layer(x, ng, w_in, b_forget, qn_ch, kn_ch, qn_fox, kn_fox, rel_bias, w_out, consts):
    bd, u_sb, low = consts
    d = x.shape[-1]
    w4 = w_in[:, :N_GROUPS * 4 * W_GROUP].reshape(d, N_GROUPS, 4, W_GROUP)
    w_nat = jnp.transpose(w4[:, :, (1, 3), :], (2, 1, 0, 3)).reshape(2 * N_GROUPS, d, W_GROUP)
    w_tr = jnp.transpose(w4[:, :, (0, 2), :], (2, 1, 3, 0)).reshape(2 * N_GROUPS, W_GROUP, d)
    w_f = jnp.pad(w_in[:, N_GROUPS * 4 * W_GROUP:], ((0, 0), (0, F_PAD - H_GROUP))).astype(MM_DTYPE)
    grow = jnp.stack([jnp.tile(kn_ch, H_GROUP), jnp.tile(kn_fox, H_GROUP)])[:, None, :].astype(F32)
    gcol = jnp.stack([qn_ch, qn_fox])[:, :, None].astype(F32)
    bias_row = jnp.pad(b_forget.astype(F32), (0, F_PAD - H_GROUP))[None, :]

    qt, k, vt, g, f = _projection(x, ng[None, :], w_nat.astype(MM_DTYPE), w_tr.astype(MM_DTYPE),
                                  w_f, grow, gcol, bd)
    cum = _cum_forget(f, bias_row, low)
    m_sb = _sb_attention(qt, k, vt, g, u_sb)
    m_ch = _chunk_attention(qt, k, vt, g, _chunk_bias_table(rel_bias))
    m_fx = _fox_attention(qt, k, vt, g, cum)
    w_o = w_out.reshape(N_GROUPS, W_GROUP, d).astype(MM_DTYPE)
    return _out_projection(x, m_sb, m_ch, m_fx, w_o)


def kernel(x, norm_g, w_in, b_forget, q_norm_ch, k_norm_ch, q_norm_fox, k_norm_fox, rel_bias, w_out):
    consts = _constants()
    for l in range(norm_g.shape[0]):
        x = _layer(x, norm_g[l], w_in[l], b_forget[l], q_norm_ch[l], k_norm_ch[l],
                   q_norm_fox[l], k_norm_fox[l], rel_bias[l], w_out[l], consts)
    return x
```

```python
import jax
import jax.numpy as jnp
import numpy as np
from jax import lax
from jax.experimental import pallas as pl
from jax.experimental.pallas import tpu as pltpu

D_MODEL = 1024
HEAD_DIM = 64
H_GROUP = 8
W_GROUP = H_GROUP * HEAD_DIM
N_GROUPS = 3
CHUNK = 64
N_LEFT_CHUNKS = 8
LEFT = N_LEFT_CHUNKS * CHUNK
REL_CLIP = 128
EPS = 1e-6
SCALE = HEAD_DIM ** -0.5
LOG2E = 1.4426950408889634

LANES = 128
MXU_DIM = 256
F_PAD = LANES
NEG = -1e30

MM_DTYPE = jnp.bfloat16
F32 = jnp.float32

TM_PROJ = 512
TQ_ATT = 2048
FOX_KEYS_PER_TRIP = 1024
BF16_ROWS = 16
FOX_V_ROWS = HEAD_DIM + BF16_ROWS
FOX_SKEW = 5
SB_SKEW = 3
SB_STATIC_DIAGS = 2
SB_DEAD_LOG = -104.0
TK_SB = MXU_DIM
TQ_CHUNK = MXU_DIM
WIN_CHUNK = LEFT + TQ_CHUNK
TS_CHUNK = 2048
CUM_BLK = 512
VMEM_LIMIT = 56 * 1024 * 1024


def _dot(a, b):
    return jnp.dot(a, b, preferred_element_type=F32)


def _dot_nt(a, b):
    return lax.dot_general(a, b, (((1,), (1,)), ((), ())), preferred_element_type=F32)


def _split2(x):
    hi = x.astype(MM_DTYPE)
    lo = (x - hi.astype(F32)).astype(MM_DTYPE)
    return hi, lo


def _split3(x):
    hi = x.astype(MM_DTYPE)
    mid, lo = _split2(x - hi.astype(F32))
    return hi, mid, lo


def _head_mask(shape, j, axis):
    idx = lax.broadcasted_iota(jnp.int32, shape, axis)
    return (idx >= j * HEAD_DIM) & (idx < (j + 1) * HEAD_DIM)


def _norm_rows(y, gain_row, bd):
    hi, lo = _split2(y * y)
    parts = []
    for c in range(W_GROUP // MXU_DIM):
        sl = slice(c * MXU_DIM, (c + 1) * MXU_DIM)
        parts.append(_dot(hi[:, sl], bd) + _dot(lo[:, sl], bd))
    ssq = jnp.concatenate(parts, axis=1)
    return y * lax.rsqrt(ssq * (1.0 / HEAD_DIM) + EPS) * gain_row


def _norm_cols(yt, gain_col):
    y3 = yt.reshape(H_GROUP, HEAD_DIM, yt.shape[1])
    ssq = jnp.sum(y3 * y3, axis=1, keepdims=True)
    y3 = y3 * lax.rsqrt(ssq * (1.0 / HEAD_DIM) + EPS) * gain_col
    return y3.reshape(yt.shape)


def _proj_kernel(x_ref, *refs):
    _project(x_ref[...], *refs)


def _out_proj_kernel(x_ref, a_ref, b_ref, c_ref, wo_ref, *refs):
    *proj_refs, xo_ref = refs
    x = x_ref[...] + (_dot(a_ref[...], wo_ref[0]) + _dot(b_ref[...], wo_ref[1])
                      + _dot(c_ref[...], wo_ref[2]))
    xo_ref[...] = x
    _project(x, *proj_refs)


def _project(x, ng_ref, wn_ref, wt_ref, wf_ref, grow_ref, gcol_ref, bd_ref,
             qt_ref, k_ref, vt_ref, g_ref, f_ref):
    h = x * lax.rsqrt(jnp.mean(x * x, axis=-1, keepdims=True) + EPS) * ng_ref[...]
    hb = h.astype(MM_DTYPE)
    bd = bd_ref[...]
    dt = qt_ref.dtype
    for grp in range(N_GROUPS):
        qt = _dot_nt(wt_ref[grp], hb)
        k = _dot(hb, wn_ref[grp])
        if grp > 0:
            qt = _norm_cols(qt, gcol_ref[grp - 1])
            k = _norm_rows(k, grow_ref[grp - 1], bd)
        qt_ref[grp] = (qt * SCALE).astype(dt)
        k_ref[grp] = k.astype(dt)
        vt_ref[grp] = _dot_nt(wt_ref[N_GROUPS + grp], hb).astype(dt)
        g_ref[grp] = _dot(hb, wn_ref[N_GROUPS + grp])
    f_ref[...] = _dot(hb, wf_ref[...])


def _projection(x, proj_weights, prev=None):
    b, s, d = x.shape
    tm = min(TM_PROJ, s)
    const = dict(pipeline_mode=pl.Buffered(1))
    row_spec = pl.BlockSpec((None, tm, d), lambda bi, i: (bi, i, 0))
    nat_spec = pl.BlockSpec((N_GROUPS, None, tm, W_GROUP), lambda bi, i: (0, bi, i, 0))
    tr_spec = pl.BlockSpec((N_GROUPS, None, W_GROUP, tm), lambda bi, i: (0, bi, 0, i))
    nat_shape = jax.ShapeDtypeStruct((N_GROUPS, b, s, W_GROUP), MM_DTYPE)
    tr_shape = jax.ShapeDtypeStruct((N_GROUPS, b, W_GROUP, s), MM_DTYPE)
    prev_specs, extra_out_specs, extra_out_shapes = [], [], []
    if prev is not None:
        mix = pl.BlockSpec((None, tm, W_GROUP), lambda bi, i: (bi, i, 0))
        prev_specs = [mix, mix, mix,
                      pl.BlockSpec((N_GROUPS, W_GROUP, d), lambda bi, i: (0, 0, 0), **const)]
        extra_out_specs = [row_spec]
        extra_out_shapes = [jax.ShapeDtypeStruct(x.shape, x.dtype)]
    outs = pl.pallas_call(
        _proj_kernel if prev is None else _out_proj_kernel,
        grid=(b, s // tm),
        in_specs=[row_spec] + prev_specs + [
            pl.BlockSpec((1, d), lambda bi, i: (0, 0)),
            pl.BlockSpec((2 * N_GROUPS, d, W_GROUP), lambda bi, i: (0, 0, 0), **const),
            pl.BlockSpec((2 * N_GROUPS, W_GROUP, d), lambda bi, i: (0, 0, 0), **const),
            pl.BlockSpec((d, F_PAD), lambda bi, i: (0, 0), **const),
            pl.BlockSpec((2, 1, W_GROUP), lambda bi, i: (0, 0, 0)),
            pl.BlockSpec((2, HEAD_DIM, 1), lambda bi, i: (0, 0, 0)),
            pl.BlockSpec((MXU_DIM, MXU_DIM), lambda bi, i: (0, 0)),
        ],
        out_specs=[
            tr_spec, nat_spec, tr_spec,
            pl.BlockSpec((N_GROUPS, None, tm, W_GROUP), lambda bi, i: (0, bi, i, 0)),
            pl.BlockSpec((None, tm, F_PAD), lambda bi, i: (bi, i, 0)),
        ] + extra_out_specs,
        out_shape=[
            tr_shape, nat_shape, tr_shape,
            jax.ShapeDtypeStruct((N_GROUPS, b, s, W_GROUP), F32),
            jax.ShapeDtypeStruct((b, s, F_PAD), F32),
        ] + extra_out_shapes,
        compiler_params=pltpu.CompilerParams(
            dimension_semantics=("arbitrary", "arbitrary"), vmem_limit_bytes=VMEM_LIMIT),
        name="proj" if prev is None else "out_proj_proj",
    )(x, *(prev or ()), *proj_weights)
    return outs if prev is None else (outs[-1], *outs[:-1])


def _cum_kernel(f_ref, b_ref, l_ref, o_ref):
    z = f_ref[...] + b_ref[...]
    lf = jnp.minimum(z, 0.0) - jnp.log1p(jnp.exp(-jnp.abs(z)))
    low = l_ref[...]
    carry = jnp.zeros((1, F_PAD), F32)
    for c in range(lf.shape[0] // CUM_BLK):
        rows = slice(c * CUM_BLK, (c + 1) * CUM_BLK)
        hi, mid, lo = _split3(lf[rows])
        cs = (_dot(low, hi) + _dot(low, mid)) + _dot(low, lo) + carry
        o_ref[rows] = cs
        carry = cs[CUM_BLK - 1:CUM_BLK, :]


def _cum_forget(f, bias_row, low):
    b, s, _ = f.shape
    return pl.pallas_call(
        _cum_kernel,
        grid=(b,),
        in_specs=[
            pl.BlockSpec((None, s, F_PAD), lambda bi: (bi, 0, 0)),
            pl.BlockSpec((1, F_PAD), lambda bi: (0, 0)),
            pl.BlockSpec((CUM_BLK, CUM_BLK), lambda bi: (0, 0)),
        ],
        out_specs=pl.BlockSpec((None, s, F_PAD), lambda bi: (bi, 0, 0)),
        out_shape=jax.ShapeDtypeStruct((b, s, F_PAD), F32),
        compiler_params=pltpu.CompilerParams(dimension_semantics=("arbitrary",)),
        name="cum_forget",
    )(f, bias_row, low)


def _gated(g, o0, o1):
    lane = lax.broadcasted_iota(jnp.int32, o0.shape, 1)
    o = jnp.where(lane < HEAD_DIM, o0, o1)
    return o * (g * jax.nn.sigmoid(g))


def _gated_t(g, ot0, ot1):
    o = jnp.concatenate([ot0, ot1], axis=0).T
    return o * (g * jax.nn.sigmoid(g))


def _nat_tile(tq):
    return lambda b, p, i: (b, i, p)


def _t_specs(grp, tq, s):
    return [
        pl.BlockSpec((None, None, LANES, tq), lambda b, p, i: (grp, b, p, i)),
        pl.BlockSpec((None, None, s, LANES), lambda b, p, i: (grp, b, 0, p)),
        pl.BlockSpec((None, None, LANES, s), lambda b, p, i: (grp, b, p, 0)),
        pl.BlockSpec((None, None, tq, LANES), lambda b, p, i: (grp, b, i, p)),
    ]


_ATT_PARAMS = pltpu.CompilerParams(
    dimension_semantics=("arbitrary", "arbitrary", "arbitrary"), vmem_limit_bytes=VMEM_LIMIT)


def _neg_abs(x):
    bits = lax.bitcast_convert_type(x, jnp.uint32) | jnp.uint32(0x80000000)
    return lax.bitcast_convert_type(bits, F32)


def _run_skewed(units):
    pending = list(units)
    active = []
    while pending or active:
        if pending:
            active.append(pending.pop(0))
        for g in list(active):
            try:
                next(g)
            except StopIteration:
                active.remove(g)


def _masked_heads_t(qt):
    q = qt.astype(F32)
    return [jnp.where(_head_mask(q.shape, j, 0), q, 0.0).astype(MM_DTYPE) for j in range(2)]


def _sb_kernel(qt_ref, k_ref, vt_ref, g_ref, u_ref, o_ref, acc_ref, carry_ref):
    t = qt_ref.shape[1]
    tk = u_ref.shape[0]
    n_sub = t // tk
    qm = _masked_heads_t(qt_ref[...])
    u = u_ref[...]
    acc_ref[...] = jnp.zeros_like(acc_ref)
    carry_ref[...] = jnp.zeros_like(carry_ref)

    def unit(j, kb, qc, diag):
        cols = slice(qc * tk, (qc + 1) * tk)
        static = isinstance(kb, int)
        k0 = kb * tk if static else pl.multiple_of(jnp.maximum(kb, 0) * tk, tk)
        z = _dot(k_ref[pl.ds(k0, tk), :], qm[j][:, cols])
        for _ in range(SB_SKEW):
            yield
        sp = jnp.maximum(z, 0.0) + jnp.log(1.0 + jnp.exp(_neg_abs(z)))
        if diag:
            strict = (lax.broadcasted_iota(jnp.int32, z.shape, 0)
                      < lax.broadcasted_iota(jnp.int32, z.shape, 1))
            sp = jnp.where(strict, sp, 0.0)
        hi, lo = _split2(sp)
        yield
        incl = _dot(u, hi) + _dot(u, lo)
        for _ in range(SB_SKEW):
            yield
        w = jnp.exp(z + incl)
        if diag:
            w = jnp.where(strict, w, 0.0)
        wb = w.astype(MM_DTYPE)
        yield
        pv = _dot(vt_ref[j * HEAD_DIM:(j + 1) * HEAD_DIM, pl.ds(k0, tk)], wb)
        carry = carry_ref[j, :, cols]
        scale = jnp.exp(carry)
        step = incl[:1]
        if not static:
            scale = jnp.where(kb >= 0, scale, 0.0)
            step = jnp.where(kb >= 0, step, 0.0)
        acc_ref[j, :, cols] += scale * pv
        carry_ref[j, :, cols] = carry + step

    def diagonal(d, first_qc, masked=False):
        return [unit(j, qc - d, qc, masked) for qc in range(first_qc, n_sub) for j in range(2)]

    def live(d):
        lane = lax.broadcasted_iota(jnp.int32, carry_ref.shape, 2)
        return jnp.max(jnp.where(lane >= d * tk, carry_ref[...], NEG)) >= SB_DEAD_LOG

    n_static = min(SB_STATIC_DIAGS, n_sub)
    _run_skewed([un for d in range(n_static) for un in diagonal(d, d, masked=d == 0)])

    def body(state):
        d, _ = state
        _run_skewed(diagonal(d, n_static))
        return d + 1, live(d + 1)

    lax.while_loop(lambda st: (st[0] < n_sub) & st[1], body, (n_static, live(n_static)))
    o_ref[...] = _gated_t(g_ref[...], acc_ref[0], acc_ref[1]).astype(o_ref.dtype)


def _sb_attention(qt, k, vt, g, u):
    _, b, _, s = qt.shape
    t = s
    tk = u.shape[0]
    return pl.pallas_call(
        _sb_kernel,
        grid=(b, W_GROUP // LANES, s // t),
        in_specs=_t_specs(0, t, s) + [pl.BlockSpec((tk, tk), lambda b_, p, i: (0, 0))],
        out_specs=pl.BlockSpec((None, t, LANES), _nat_tile(t)),
        out_shape=jax.ShapeDtypeStruct((b, s, W_GROUP), MM_DTYPE),
        scratch_shapes=[pltpu.VMEM((2, HEAD_DIM, t), F32), pltpu.VMEM((2, 1, t), F32)],
        compiler_params=_ATT_PARAMS,
        name="sb_attention",
    )(qt, k, vt, g, u)


def _chunk_kernel(qt_ref, k_ref, vt_ref, g_ref, bias_ref, o_ref, kpad, vaug, ot_ref):
    i = pl.program_id(2)
    ts = qt_ref.shape[1]
    win, tq = bias_ref.shape[1:]
    s = k_ref.shape[0]
    n_ones = vaug.shape[1] - HEAD_DIM

    @pl.when(i == 0)
    def _():
        kpad[:LEFT] = jnp.zeros((LEFT, LANES), kpad.dtype)
        kpad[LEFT:] = k_ref[...]
        vt = vt_ref[...].astype(F32)
        for j in range(2):
            vaug[j, :, :LEFT] = jnp.zeros((vaug.shape[1], LEFT), vaug.dtype)
            vaug[j, :, LEFT:] = jnp.concatenate(
                [vt[j * HEAD_DIM:(j + 1) * HEAD_DIM], jnp.ones((n_ones, s), F32)],
                axis=0).astype(vaug.dtype)

    qm = _masked_heads_t(qt_ref[...])

    def unit(j, r, first_step):
        cols = slice(r * tq, (r + 1) * tq)
        r0 = pl.multiple_of(i * ts + r * tq, tq)
        zs = []
        for kb in range(win // tq):
            k0 = pl.multiple_of(r0 + kb * tq, tq)
            z = _dot(kpad[pl.ds(k0, tq), :], qm[j][:, cols]) + bias_ref[j, kb * tq:(kb + 1) * tq]
            lo_key = LEFT - r * tq - kb * tq
            if first_step and lo_key > 0:
                key = lax.broadcasted_iota(jnp.int32, z.shape, 0)
                z = jnp.where(key >= lo_key, z, NEG)
            zs.append(z)
            yield
        m = zs[0].max(axis=0, keepdims=True)
        for z in zs[1:]:
            m = jnp.maximum(m, z.max(axis=0, keepdims=True))
        ps = []
        for z in zs:
            ps.append(jnp.exp(z - m).astype(MM_DTYPE))
            yield
        pv = None
        for kb, p in enumerate(ps):
            k0 = pl.multiple_of(r0 + kb * tq, tq)
            term = _dot(vaug[j, :, pl.ds(k0, tq)], p)
            pv = term if pv is None else pv + term
        ot_ref[j * HEAD_DIM:(j + 1) * HEAD_DIM, cols] = pv[:HEAD_DIM] / pv[HEAD_DIM:HEAD_DIM + 1]

    def run(first_step):
        _run_skewed([unit(j, r, first_step) for r in range(ts // tq) for j in range(2)])

    @pl.when(i == 0)
    def _():
        run(True)

    @pl.when(i > 0)
    def _():
        run(False)

    o_ref[...] = _gated_t(g_ref[...], ot_ref[:HEAD_DIM], ot_ref[HEAD_DIM:]).astype(o_ref.dtype)


def _chunk_attention(qt, k, vt, g, bias):
    _, b, _, s = qt.shape
    ts = min(TS_CHUNK, s)
    return pl.pallas_call(
        _chunk_kernel,
        grid=(b, W_GROUP // LANES, s // ts),
        in_specs=_t_specs(1, ts, s) + [
            pl.BlockSpec((2, WIN_CHUNK, TQ_CHUNK), lambda b_, p, i: (p, 0, 0))],
        out_specs=pl.BlockSpec((None, ts, LANES), _nat_tile(ts)),
        out_shape=jax.ShapeDtypeStruct((b, s, W_GROUP), MM_DTYPE),
        scratch_shapes=[pltpu.VMEM((LEFT + s, LANES), MM_DTYPE),
                        pltpu.VMEM((2, FOX_V_ROWS, LEFT + s), MM_DTYPE),
                        pltpu.VMEM((LANES, ts), F32)],
        compiler_params=_ATT_PARAMS,
        name="chunk_attention",
    )(qt, k, vt, g, bias)


N_DIAG = WIN_CHUNK + TQ_CHUNK


def _bias_kernel(v_ref, o_ref):
    x = jnp.broadcast_to(v_ref[...], (WIN_CHUNK, N_DIAG))
    y = pltpu.roll(x, 1, 1, stride=1, stride_axis=0)
    t = y[:, WIN_CHUNK:]
    c = lax.broadcasted_iota(jnp.int32, t.shape, 0)
    r = lax.broadcasted_iota(jnp.int32, t.shape, 1)
    band = c - (r - (r & (CHUNK - 1)))
    o_ref[...] = jnp.where((band >= 0) & (band < LEFT + CHUNK), t, NEG)


def _chunk_bias_table(rel_bias):
    kk = np.arange(N_DIAG - 1)
    rel = kk - (WIN_CHUNK - 1) + LEFT
    vec = rel_bias[:, np.clip(rel, -REL_CLIP, REL_CLIP) + REL_CLIP].astype(F32)
    vec = jnp.pad(vec, ((0, 0), (0, 1)))[:, None, :]
    h = vec.shape[0]
    return pl.pallas_call(
        _bias_kernel,
        grid=(h,),
        in_specs=[pl.BlockSpec((None, 1, N_DIAG), lambda i: (i, 0, 0))],
        out_specs=pl.BlockSpec((None, WIN_CHUNK, TQ_CHUNK), lambda i: (i, 0, 0)),
        out_shape=jax.ShapeDtypeStruct((h, WIN_CHUNK, TQ_CHUNK), F32),
        compiler_params=pltpu.CompilerParams(dimension_semantics=("arbitrary",)),
        name="chunk_bias",
    )(vec)


def _fox_kernel(qt_ref, k_ref, vt_ref, g_ref, c_ref, o_ref, kaug, vaug, m_ref, acc_ref):
    i = pl.program_id(2)
    p = pl.program_id(1)
    t = qt_ref.shape[1]
    aug0 = [HEAD_DIM * (1 - j) for j in range(2)]
    n_ones = vaug.shape[1] - HEAD_DIM

    @pl.when(i == 0)
    def _():
        k = k_ref[...].astype(F32)
        vt = vt_ref[...].astype(F32)
        pieces = jnp.concatenate(_split3(-c_ref[...]), axis=1)
        row = lax.broadcasted_iota(jnp.int32, (3 * LANES, 2 * LANES), 0)
        col = lax.broadcasted_iota(jnp.int32, (3 * LANES, 2 * LANES), 1)
        place = None
        for j in range(2):
            for c in range(3):
                hit = (row == c * LANES + 2 * p + j) & (col == j * LANES + aug0[j] + c)
                place = hit if place is None else place | hit
        aug = _dot(pieces, jnp.where(place, 1.0, 0.0).astype(MM_DTYPE))
        for j in range(2):
            kaug[j] = jnp.where(_head_mask(k.shape, j, 1), k,
                                aug[:, j * LANES:(j + 1) * LANES]).astype(kaug.dtype)
            vaug[j] = jnp.concatenate(
                [vt[j * HEAD_DIM:(j + 1) * HEAD_DIM], jnp.ones((n_ones, vt.shape[1]), F32)],
                axis=0).astype(vaug.dtype)

    q = qt_ref[...].astype(F32)
    row = lax.broadcasted_iota(jnp.int32, q.shape, 0)
    qa = []
    for j in range(2):
        ones = (row >= aug0[j]) & (row < aug0[j] + 3)
        qa.append(jnp.where(_head_mask(q.shape, j, 0), q, jnp.where(ones, 1.0, 0.0)).astype(MM_DTYPE))
    m_ref[...] = jnp.full(m_ref.shape, NEG, F32)
    acc_ref[...] = jnp.zeros_like(acc_ref)

    tk = TK_SB
    n_sub = t // tk

    def unit(j, k0, qc, diag):
        cols = slice(qc * tk, (qc + 1) * tk)
        z = _dot(kaug[j, pl.ds(k0, tk), :], qa[j][:, cols])
        for _ in range(FOX_SKEW):
            yield
        if diag:
            causal = (lax.broadcasted_iota(jnp.int32, z.shape, 0)
                      <= lax.broadcasted_iota(jnp.int32, z.shape, 1))
            z = jnp.where(causal, z, NEG)
        m = m_ref[j, :, cols]
        m_new = jnp.maximum(m, jnp.max(z, axis=0, keepdims=True))
        pr = jnp.exp(z - m_new).astype(MM_DTYPE)
        alpha = jnp.exp(m - m_new)
        m_ref[j, :, cols] = m_new
        yield
        pv = _dot(vaug[j, :, pl.ds(k0, tk)], pr)
        acc_ref[j, :, cols] = alpha * acc_ref[j, :, cols] + pv

    def units(base, n_blocks, diag):
        out = []
        for c in range(n_blocks):
            for qc in range(c if diag else 0, n_sub):
                for j in range(2):
                    out.append(unit(j, pl.multiple_of(base + c * tk, tk), qc, diag and qc == c))
        return out

    trip = min(FOX_KEYS_PER_TRIP, t)

    def body(kj, _):
        _run_skewed(units(kj * trip, trip // tk, False))
        return 0

    lax.fori_loop(0, i * (t // trip), body, 0)
    _run_skewed(units(i * t, n_sub, True))
    outs = [acc_ref[j, :HEAD_DIM] / acc_ref[j, HEAD_DIM:HEAD_DIM + 1] for j in range(2)]
    o_ref[...] = _gated_t(g_ref[...], outs[0], outs[1]).astype(o_ref.dtype)


def _fox_attention(qt, k, vt, g, cum):
    _, b, _, s = qt.shape
    t = min(TQ_ATT, s)
    return pl.pallas_call(
        _fox_kernel,
        grid=(b, W_GROUP // LANES, s // t),
        in_specs=_t_specs(2, t, s) + [
            pl.BlockSpec((None, s, F_PAD), lambda b_, p, i: (b_, 0, 0))],
        out_specs=pl.BlockSpec((None, t, LANES), _nat_tile(t)),
        out_shape=jax.ShapeDtypeStruct((b, s, W_GROUP), MM_DTYPE),
        scratch_shapes=[pltpu.VMEM((2, s, LANES), MM_DTYPE),
                        pltpu.VMEM((2, FOX_V_ROWS, s), MM_DTYPE),
                        pltpu.VMEM((2, 1, t), F32), pltpu.VMEM((2, FOX_V_ROWS, t), F32)],
        compiler_params=_ATT_PARAMS,
        name="fox_attention",
    )(qt, k, vt, g, cum)


def _out_kernel(x_ref, a_ref, b_ref, c_ref, w_ref, o_ref):
    acc = _dot(a_ref[...], w_ref[0]) + _dot(b_ref[...], w_ref[1]) + _dot(c_ref[...], w_ref[2])
    o_ref[...] = x_ref[...] + acc


def _out_projection(x, ma, mb, mc, w):
    b, s, d = x.shape
    tm = min(TM_PROJ, s)
    mix = pl.BlockSpec((None, tm, W_GROUP), lambda bi, i: (bi, i, 0))
    return pl.pallas_call(
        _out_kernel,
        grid=(b, s // tm),
        in_specs=[pl.BlockSpec((None, tm, d), lambda bi, i: (bi, i, 0)), mix, mix, mix,
                  pl.BlockSpec((N_GROUPS, W_GROUP, d), lambda bi, i: (0, 0, 0))],
        out_specs=pl.BlockSpec((None, tm, d), lambda bi, i: (bi, i, 0)),
        out_shape=jax.ShapeDtypeStruct(x.shape, x.dtype),
        compiler_params=pltpu.CompilerParams(
            dimension_semantics=("arbitrary", "arbitrary"), vmem_limit_bytes=VMEM_LIMIT),
        name="out_proj",
    )(x, ma, mb, mc, w)


def _constants():
    r = np.arange(MXU_DIM)
    bd = (r[:, None] // HEAD_DIM == r[None, :] // HEAD_DIM).astype(np.float32)
    r = np.arange(TK_SB)
    u_sb = -(r[None, :] >= r[:, None]).astype(np.float32)
    r = np.arange(CUM_BLK)
    low = (r[None, :] <= r[:, None]).astype(np.float32)
    return (jnp.asarray(bd, MM_DTYPE), jnp.asarray(u_sb, MM_DTYPE), jnp.asarray(low, MM_DTYPE))


def _proj_weights(ng, w_in, qn_ch, kn_ch, qn_fox, kn_fox, bd):
    d = w_in.shape[0]
    w4 = w_in[:, :N_GROUPS * 4 * W_GROUP].reshape(d, N_GROUPS, 4, W_GROUP)
    w_nat = jnp.transpose(w4[:, :, (1, 3), :], (2, 1, 0, 3)).reshape(2 * N_GROUPS, d, W_GROUP)
    w_tr = jnp.transpose(w4[:, :, (0, 2), :], (2, 1, 3, 0)).reshape(2 * N_GROUPS, W_GROUP, d)
    w_f = jnp.pad(w_in[:, N_GROUPS * 4 * W_GROUP:], ((0, 0), (0, F_PAD - H_GROUP))).astype(MM_DTYPE)
    grow = jnp.stack([jnp.tile(kn_ch, H_GROUP), jnp.tile(kn_fox, H_GROUP)])[:, None, :].astype(F32)
    gcol = jnp.stack([qn_ch, qn_fox])[:, :, None].astype(F32)
    return (ng[None, :], w_nat.astype(MM_DTYPE), w_tr.astype(MM_DTYPE), w_f, grow, gcol, bd)


def _mixers(projected, b_forget, rel_bias, u_sb, low):
    qt, k, vt, g, f = projected
    bias_row = jnp.pad(b_forget.astype(F32), (0, F_PAD - H_GROUP))[None, :]
    cum = _cum_forget(f, bias_row, low)
    m_sb = _sb_attention(qt, k, vt, g, u_sb)
    m_ch = _chunk_attention(qt, k, vt, g, _chunk_bias_table(rel_bias))
    m_fx = _fox_attention(qt, k, vt, g, cum)
    return m_sb, m_ch, m_fx


def kernel(x, norm_g, w_in, b_forget, q_norm_ch, k_norm_ch, q_norm_fox, k_norm_fox, rel_bias, w_out):
    bd, u_sb, low = _constants()
    depth, d = norm_g.shape
    weights = [_proj_weights(norm_g[l], w_in[l], q_norm_ch[l], k_norm_ch[l],
                             q_norm_fox[l], k_norm_fox[l], bd) for l in range(depth)]
    w_o = [w_out[l].reshape(N_GROUPS, W_GROUP, d).astype(MM_DTYPE) for l in range(depth)]
    projected = _projection(x, weights[0])
    for l in range(depth):
        mixed = _mixers(projected, b_forget[l], rel_bias[l], u_sb, low)
        if l + 1 < depth:
            x, *projected = _projection(x, weights[l + 1], prev=(*mixed, w_o[l]))
        else:
            x = _out_projection(x, *mixed, w_o[l])
    return x
```

```python
import jax
import jax.numpy as jnp
import numpy as np
from jax import lax
from jax.experimental import pallas as pl
from jax.experimental.pallas import tpu as pltpu

D_MODEL = 1024
HEAD_DIM = 64
H_GROUP = 8
W_GROUP = H_GROUP * HEAD_DIM
N_GROUPS = 3
CHUNK = 64
N_LEFT_CHUNKS = 8
LEFT = N_LEFT_CHUNKS * CHUNK
REL_CLIP = 128
EPS = 1e-6
SCALE = HEAD_DIM ** -0.5
LOG2E = 1.4426950408889634

LANES = 128
MXU_DIM = 256
F_PAD = LANES
NEG = -1e30

MM_DTYPE = jnp.bfloat16
F32 = jnp.float32

TM_PROJ = 512
TQ_ATT = 2048
FOX_KEYS_PER_TRIP = 1024
BF16_ROWS = 16
FOX_V_ROWS = HEAD_DIM + BF16_ROWS
FOX_SKEW = 5
SB_SKEW = 3
SB_STATIC_DIAGS = 2
SB_DEAD_LOG = -104.0
TK_SB = MXU_DIM
TQ_CHUNK = MXU_DIM
WIN_CHUNK = LEFT + TQ_CHUNK
TS_CHUNK = 4096
CUM_BLK = 512
VMEM_LIMIT = 56 * 1024 * 1024


def _dot(a, b):
    return jnp.dot(a, b, preferred_element_type=F32)


def _dot_nt(a, b):
    return lax.dot_general(a, b, (((1,), (1,)), ((), ())), preferred_element_type=F32)


def _split2(x):
    hi = x.astype(MM_DTYPE)
    lo = (x - hi.astype(F32)).astype(MM_DTYPE)
    return hi, lo


def _split3(x):
    hi = x.astype(MM_DTYPE)
    mid, lo = _split2(x - hi.astype(F32))
    return hi, mid, lo


def _head_mask(shape, j, axis):
    idx = lax.broadcasted_iota(jnp.int32, shape, axis)
    return (idx >= j * HEAD_DIM) & (idx < (j + 1) * HEAD_DIM)


def _norm_rows(y, gain_row, bd):
    hi, lo = _split2(y * y)
    parts = []
    for c in range(W_GROUP // MXU_DIM):
        sl = slice(c * MXU_DIM, (c + 1) * MXU_DIM)
        parts.append(_dot(hi[:, sl], bd) + _dot(lo[:, sl], bd))
    ssq = jnp.concatenate(parts, axis=1)
    return y * lax.rsqrt(ssq * (1.0 / HEAD_DIM) + EPS) * gain_row


def _norm_cols(yt, gain_col):
    y3 = yt.reshape(H_GROUP, HEAD_DIM, yt.shape[1])
    ssq = jnp.sum(y3 * y3, axis=1, keepdims=True)
    y3 = y3 * lax.rsqrt(ssq * (1.0 / HEAD_DIM) + EPS) * gain_col
    return y3.reshape(yt.shape)


def _proj_kernel(x_ref, *refs):
    _project(x_ref[...], *refs)


def _out_proj_kernel(x_ref, a_ref, b_ref, c_ref, wo_ref, *refs):
    *proj_refs, xo_ref = refs
    x = x_ref[...] + (_dot(a_ref[...], wo_ref[0]) + _dot(b_ref[...], wo_ref[1])
                      + _dot(c_ref[...], wo_ref[2]))
    xo_ref[...] = x
    _project(x, *proj_refs)


def _project(x, ng_ref, wn_ref, wt_ref, wf_ref, grow_ref, gcol_ref, bd_ref,
             qt_ref, k_ref, vt_ref, g_ref, f_ref):
    h = x * lax.rsqrt(jnp.mean(x * x, axis=-1, keepdims=True) + EPS) * ng_ref[...]
    hb = h.astype(MM_DTYPE)
    bd = bd_ref[...]
    dt = qt_ref.dtype
    for grp in range(N_GROUPS):
        qt = _dot_nt(wt_ref[grp], hb)
        k = _dot(hb, wn_ref[grp])
        if grp > 0:
            qt = _norm_cols(qt, gcol_ref[grp - 1])
            k = _norm_rows(k, grow_ref[grp - 1], bd)
        qt_ref[grp] = (qt * SCALE).astype(dt)
        k_ref[grp] = k.astype(dt)
        vt_ref[grp] = _dot_nt(wt_ref[N_GROUPS + grp], hb).astype(dt)
        g_ref[grp] = _dot(hb, wn_ref[N_GROUPS + grp])
    f_ref[...] = _dot(hb, wf_ref[...])


def _projection(x, proj_weights, prev=None):
    b, s, d = x.shape
    tm = min(TM_PROJ, s)
    const = dict(pipeline_mode=pl.Buffered(1))
    row_spec = pl.BlockSpec((None, tm, d), lambda bi, i: (bi, i, 0))
    nat_spec = pl.BlockSpec((N_GROUPS, None, tm, W_GROUP), lambda bi, i: (0, bi, i, 0))
    tr_spec = pl.BlockSpec((N_GROUPS, None, W_GROUP, tm), lambda bi, i: (0, bi, 0, i))
    nat_shape = jax.ShapeDtypeStruct((N_GROUPS, b, s, W_GROUP), MM_DTYPE)
    tr_shape = jax.ShapeDtypeStruct((N_GROUPS, b, W_GROUP, s), MM_DTYPE)
    prev_specs, extra_out_specs, extra_out_shapes = [], [], []
    if prev is not None:
        mix = pl.BlockSpec((None, tm, W_GROUP), lambda bi, i: (bi, i, 0))
        prev_specs = [mix, mix, mix,
                      pl.BlockSpec((N_GROUPS, W_GROUP, d), lambda bi, i: (0, 0, 0), **const)]
        extra_out_specs = [row_spec]
        extra_out_shapes = [jax.ShapeDtypeStruct(x.shape, x.dtype)]
    outs = pl.pallas_call(
        _proj_kernel if prev is None else _out_proj_kernel,
        grid=(b, s // tm),
        in_specs=[row_spec] + prev_specs + [
            pl.BlockSpec((1, d), lambda bi, i: (0, 0)),
            pl.BlockSpec((2 * N_GROUPS, d, W_GROUP), lambda bi, i: (0, 0, 0), **const),
            pl.BlockSpec((2 * N_GROUPS, W_GROUP, d), lambda bi, i: (0, 0, 0), **const),
            pl.BlockSpec((d, F_PAD), lambda bi, i: (0, 0), **const),
            pl.BlockSpec((2, 1, W_GROUP), lambda bi, i: (0, 0, 0)),
            pl.BlockSpec((2, HEAD_DIM, 1), lambda bi, i: (0, 0, 0)),
            pl.BlockSpec((MXU_DIM, MXU_DIM), lambda bi, i: (0, 0)),
        ],
        out_specs=[
            tr_spec, nat_spec, tr_spec,
            pl.BlockSpec((N_GROUPS, None, tm, W_GROUP), lambda bi, i: (0, bi, i, 0)),
            pl.BlockSpec((None, tm, F_PAD), lambda bi, i: (bi, i, 0)),
        ] + extra_out_specs,
        out_shape=[
            tr_shape, nat_shape, tr_shape,
            jax.ShapeDtypeStruct((N_GROUPS, b, s, W_GROUP), F32),
            jax.ShapeDtypeStruct((b, s, F_PAD), F32),
        ] + extra_out_shapes,
        compiler_params=pltpu.CompilerParams(
            dimension_semantics=("arbitrary", "arbitrary"), vmem_limit_bytes=VMEM_LIMIT),
        name="proj" if prev is None else "out_proj_proj",
    )(x, *(prev or ()), *proj_weights)
    return outs if prev is None else (outs[-1], *outs[:-1])


def _cum_kernel(f_ref, b_ref, l_ref, o_ref):
    z = f_ref[...] + b_ref[...]
    lf = jnp.minimum(z, 0.0) - jnp.log1p(jnp.exp(-jnp.abs(z)))
    low = l_ref[...]
    carry = jnp.zeros((1, F_PAD), F32)
    for c in range(lf.shape[0] // CUM_BLK):
        rows = slice(c * CUM_BLK, (c + 1) * CUM_BLK)
        hi, mid, lo = _split3(lf[rows])
        cs = (_dot(low, hi) + _dot(low, mid)) + _dot(low, lo) + carry
        o_ref[rows] = cs
        carry = cs[CUM_BLK - 1:CUM_BLK, :]


def _cum_forget(f, bias_row, low):
    b, s, _ = f.shape
    return pl.pallas_call(
        _cum_kernel,
        grid=(b,),
        in_specs=[
            pl.BlockSpec((None, s, F_PAD), lambda bi: (bi, 0, 0)),
            pl.BlockSpec((1, F_PAD), lambda bi: (0, 0)),
            pl.BlockSpec((CUM_BLK, CUM_BLK), lambda bi: (0, 0)),
        ],
        out_specs=pl.BlockSpec((None, s, F_PAD), lambda bi: (bi, 0, 0)),
        out_shape=jax.ShapeDtypeStruct((b, s, F_PAD), F32),
        compiler_params=pltpu.CompilerParams(dimension_semantics=("arbitrary",)),
        name="cum_forget",
    )(f, bias_row, low)


def _gated(g, o0, o1):
    lane = lax.broadcasted_iota(jnp.int32, o0.shape, 1)
    o = jnp.where(lane < HEAD_DIM, o0, o1)
    return o * (g * jax.nn.sigmoid(g))


def _gated_t(g, ot0, ot1):
    o = jnp.concatenate([ot0, ot1], axis=0).T
    return o * (g * jax.nn.sigmoid(g))


def _nat_tile(tq):
    return lambda b, p, i: (b, i, p)


def _t_specs(grp, tq, s):
    return [
        pl.BlockSpec((None, None, LANES, tq), lambda b, p, i: (grp, b, p, i)),
        pl.BlockSpec((None, None, s, LANES), lambda b, p, i: (grp, b, 0, p)),
        pl.BlockSpec((None, None, LANES, s), lambda b, p, i: (grp, b, p, 0)),
        pl.BlockSpec((None, None, tq, LANES), lambda b, p, i: (grp, b, i, p)),
    ]


_ATT_PARAMS = pltpu.CompilerParams(
    dimension_semantics=("arbitrary", "arbitrary", "arbitrary"), vmem_limit_bytes=VMEM_LIMIT)


def _neg_abs(x):
    bits = lax.bitcast_convert_type(x, jnp.uint32) | jnp.uint32(0x80000000)
    return lax.bitcast_convert_type(bits, F32)


def _run_skewed(units):
    pending = list(units)
    active = []
    while pending or active:
        if pending:
            active.append(pending.pop(0))
        for g in list(active):
            try:
                next(g)
            except StopIteration:
                active.remove(g)


def _masked_heads_t(qt):
    q = qt.astype(F32)
    return [jnp.where(_head_mask(q.shape, j, 0), q, 0.0).astype(MM_DTYPE) for j in range(2)]


def _sb_kernel(qt_ref, k_ref, vt_ref, g_ref, u_ref, o_ref, acc_ref, carry_ref):
    t = qt_ref.shape[1]
    tk = u_ref.shape[0]
    n_sub = t // tk
    qm = _masked_heads_t(qt_ref[...])
    u = u_ref[...]
    acc_ref[...] = jnp.zeros_like(acc_ref)
    carry_ref[...] = jnp.zeros_like(carry_ref)

    def unit(j, kb, qc, diag):
        cols = slice(qc * tk, (qc + 1) * tk)
        static = isinstance(kb, int)
        k0 = kb * tk if static else pl.multiple_of(jnp.maximum(kb, 0) * tk, tk)
        z = _dot(k_ref[pl.ds(k0, tk), :], qm[j][:, cols])
        for _ in range(SB_SKEW):
            yield
        sp = jnp.maximum(z, 0.0) + jnp.log(1.0 + jnp.exp(_neg_abs(z)))
        if diag:
            strict = (lax.broadcasted_iota(jnp.int32, z.shape, 0)
                      < lax.broadcasted_iota(jnp.int32, z.shape, 1))
            sp = jnp.where(strict, sp, 0.0)
        hi, lo = _split2(sp)
        yield
        incl = _dot(u, hi) + _dot(u, lo)
        for _ in range(SB_SKEW):
            yield
        w = jnp.exp(z + incl)
        if diag:
            w = jnp.where(strict, w, 0.0)
        wb = w.astype(MM_DTYPE)
        yield
        pv = _dot(vt_ref[j * HEAD_DIM:(j + 1) * HEAD_DIM, pl.ds(k0, tk)], wb)
        carry = carry_ref[j, :, cols]
        scale = jnp.exp(carry)
        step = incl[:1]
        if not static:
            scale = jnp.where(kb >= 0, scale, 0.0)
            step = jnp.where(kb >= 0, step, 0.0)
        acc_ref[j, :, cols] += scale * pv
        carry_ref[j, :, cols] = carry + step

    def diagonal(d, first_qc, masked=False):
        return [unit(j, qc - d, qc, masked) for qc in range(first_qc, n_sub) for j in range(2)]

    def live(d):
        lane = lax.broadcasted_iota(jnp.int32, carry_ref.shape, 2)
        return jnp.max(jnp.where(lane >= d * tk, carry_ref[...], NEG)) >= SB_DEAD_LOG

    n_static = min(SB_STATIC_DIAGS, n_sub)
    _run_skewed([un for d in range(n_static) for un in diagonal(d, d, masked=d == 0)])

    def body(state):
        d, _ = state
        _run_skewed(diagonal(d, n_static))
        return d + 1, live(d + 1)

    lax.while_loop(lambda st: (st[0] < n_sub) & st[1], body, (n_static, live(n_static)))
    o_ref[...] = _gated_t(g_ref[...], acc_ref[0], acc_ref[1]).astype(o_ref.dtype)


def _sb_attention(qt, k, vt, g, u):
    _, b, _, s = qt.shape
    t = s
    tk = u.shape[0]
    return pl.pallas_call(
        _sb_kernel,
        grid=(b, W_GROUP // LANES, s // t),
        in_specs=_t_specs(0, t, s) + [pl.BlockSpec((tk, tk), lambda b_, p, i: (0, 0))],
        out_specs=pl.BlockSpec((None, t, LANES), _nat_tile(t)),
        out_shape=jax.ShapeDtypeStruct((b, s, W_GROUP), MM_DTYPE),
        scratch_shapes=[pltpu.VMEM((2, HEAD_DIM, t), F32), pltpu.VMEM((2, 1, t), F32)],
        compiler_params=_ATT_PARAMS,
        name="sb_attention",
    )(qt, k, vt, g, u)


def _chunk_kernel(qt_ref, k_ref, vt_ref, g_ref, bias_ref, o_ref, kpad, vaug, ot_ref):
    i = pl.program_id(2)
    ts = qt_ref.shape[1]
    win, tq = bias_ref.shape[1:]
    s = k_ref.shape[0]
    n_ones = vaug.shape[1] - HEAD_DIM

    @pl.when(i == 0)
    def _():
        kpad[:LEFT] = jnp.zeros((LEFT, LANES), kpad.dtype)
        kpad[LEFT:] = k_ref[...]
        vt = vt_ref[...].astype(F32)
        for j in range(2):
            vaug[j, :, :LEFT] = jnp.zeros((vaug.shape[1], LEFT), vaug.dtype)
            vaug[j, :, LEFT:] = jnp.concatenate(
                [vt[j * HEAD_DIM:(j + 1) * HEAD_DIM], jnp.ones((n_ones, s), F32)],
                axis=0).astype(vaug.dtype)

    qm = _masked_heads_t(qt_ref[...])

    def unit(j, r, first_step):
        cols = slice(r * tq, (r + 1) * tq)
        r0 = pl.multiple_of(i * ts + r * tq, tq)
        zs = []
        for kb in range(win // tq):
            k0 = pl.multiple_of(r0 + kb * tq, tq)
            z = _dot(kpad[pl.ds(k0, tq), :], qm[j][:, cols]) + bias_ref[j, kb * tq:(kb + 1) * tq]
            lo_key = LEFT - r * tq - kb * tq
            if first_step and lo_key > 0:
                key = lax.broadcasted_iota(jnp.int32, z.shape, 0)
                z = jnp.where(key >= lo_key, z, NEG)
            zs.append(z)
            yield
        m = zs[0].max(axis=0, keepdims=True)
        for z in zs[1:]:
            m = jnp.maximum(m, z.max(axis=0, keepdims=True))
        ps = []
        for z in zs:
            ps.append(jnp.exp(z - m).astype(MM_DTYPE))
            yield
        pv = None
        for kb, p in enumerate(ps):
            k0 = pl.multiple_of(r0 + kb * tq, tq)
            term = _dot(vaug[j, :, pl.ds(k0, tq)], p)
            pv = term if pv is None else pv + term
        ot_ref[j * HEAD_DIM:(j + 1) * HEAD_DIM, cols] = pv[:HEAD_DIM] / pv[HEAD_DIM:HEAD_DIM + 1]

    def run(first_step):
        _run_skewed([unit(j, r, first_step) for r in range(ts // tq) for j in range(2)])

    if s == ts:
        run(True)
    else:
        pl.when(i == 0)(lambda: run(True))
        pl.when(i > 0)(lambda: run(False))

    o_ref[...] = _gated_t(g_ref[...], ot_ref[:HEAD_DIM], ot_ref[HEAD_DIM:]).astype(o_ref.dtype)


def _chunk_attention(qt, k, vt, g, bias):
    _, b, _, s = qt.shape
    ts = min(TS_CHUNK, s)
    return pl.pallas_call(
        _chunk_kernel,
        grid=(b, W_GROUP // LANES, s // ts),
        in_specs=_t_specs(1, ts, s) + [
            pl.BlockSpec((2, WIN_CHUNK, TQ_CHUNK), lambda b_, p, i: (p, 0, 0))],
        out_specs=pl.BlockSpec((None, ts, LANES), _nat_tile(ts)),
        out_shape=jax.ShapeDtypeStruct((b, s, W_GROUP), MM_DTYPE),
        scratch_shapes=[pltpu.VMEM((LEFT + s, LANES), MM_DTYPE),
                        pltpu.VMEM((2, FOX_V_ROWS, LEFT + s), MM_DTYPE),
                        pltpu.VMEM((LANES, ts), F32)],
        compiler_params=_ATT_PARAMS,
        name="chunk_attention",
    )(qt, k, vt, g, bias)


N_DIAG = WIN_CHUNK + TQ_CHUNK


def _bias_kernel(v_ref, o_ref):
    x = jnp.broadcast_to(v_ref[...], (WIN_CHUNK, N_DIAG))
    y = pltpu.roll(x, 1, 1, stride=1, stride_axis=0)
    t = y[:, WIN_CHUNK:]
    c = lax.broadcasted_iota(jnp.int32, t.shape, 0)
    r = lax.broadcasted_iota(jnp.int32, t.shape, 1)
    band = c - (r - (r & (CHUNK - 1)))
    o_ref[...] = jnp.where((band >= 0) & (band < LEFT + CHUNK), t, NEG)


def _chunk_bias_table(rel_bias):
    kk = np.arange(N_DIAG - 1)
    rel = kk - (WIN_CHUNK - 1) + LEFT
    vec = rel_bias[:, np.clip(rel, -REL_CLIP, REL_CLIP) + REL_CLIP].astype(F32)
    vec = jnp.pad(vec, ((0, 0), (0, 1)))[:, None, :]
    h = vec.shape[0]
    return pl.pallas_call(
        _bias_kernel,
        grid=(h,),
        in_specs=[pl.BlockSpec((None, 1, N_DIAG), lambda i: (i, 0, 0))],
        out_specs=pl.BlockSpec((None, WIN_CHUNK, TQ_CHUNK), lambda i: (i, 0, 0)),
        out_shape=jax.ShapeDtypeStruct((h, WIN_CHUNK, TQ_CHUNK), F32),
        compiler_params=pltpu.CompilerParams(dimension_semantics=("arbitrary",)),
        name="chunk_bias",
    )(vec)


def _fox_kernel(qt_ref, k_ref, vt_ref, g_ref, c_ref, o_ref, kaug, vaug, m_ref, acc_ref):
    i = pl.program_id(2)
    p = pl.program_id(1)
    t = qt_ref.shape[1]
    aug0 = [HEAD_DIM * (1 - j) for j in range(2)]
    n_ones = vaug.shape[1] - HEAD_DIM

    @pl.when(i == 0)
    def _():
        k = k_ref[...].astype(F32)
        vt = vt_ref[...].astype(F32)
        pieces = jnp.concatenate(_split3(-c_ref[...]), axis=1)
        row = lax.broadcasted_iota(jnp.int32, (3 * LANES, 2 * LANES), 0)
        col = lax.broadcasted_iota(jnp.int32, (3 * LANES, 2 * LANES), 1)
        place = None
        for j in range(2):
            for c in range(3):
                hit = (row == c * LANES + 2 * p + j) & (col == j * LANES + aug0[j] + c)
                place = hit if place is None else place | hit
        aug = _dot(pieces, jnp.where(place, 1.0, 0.0).astype(MM_DTYPE))
        for j in range(2):
            kaug[j] = jnp.where(_head_mask(k.shape, j, 1), k,
                                aug[:, j * LANES:(j + 1) * LANES]).astype(kaug.dtype)
            vaug[j] = jnp.concatenate(
                [vt[j * HEAD_DIM:(j + 1) * HEAD_DIM], jnp.ones((n_ones, vt.shape[1]), F32)],
                axis=0).astype(vaug.dtype)

    q = qt_ref[...].astype(F32)
    row = lax.broadcasted_iota(jnp.int32, q.shape, 0)
    qa = []
    for j in range(2):
        ones = (row >= aug0[j]) & (row < aug0[j] + 3)
        qa.append(jnp.where(_head_mask(q.shape, j, 0), q, jnp.where(ones, 1.0, 0.0)).astype(MM_DTYPE))
    m_ref[...] = jnp.full(m_ref.shape, NEG, F32)
    acc_ref[...] = jnp.zeros_like(acc_ref)

    tk = TK_SB
    n_sub = t // tk

    def unit(j, k0, qc, diag):
        cols = slice(qc * tk, (qc + 1) * tk)
        z = _dot(kaug[j, pl.ds(k0, tk), :], qa[j][:, cols])
        for _ in range(FOX_SKEW):
            yield
        if diag:
            causal = (lax.broadcasted_iota(jnp.int32, z.shape, 0)
                      <= lax.broadcasted_iota(jnp.int32, z.shape, 1))
            z = jnp.where(causal, z, NEG)
        m = m_ref[j, :, cols]
        m_new = jnp.maximum(m, jnp.max(z, axis=0, keepdims=True))
        pr = jnp.exp(z - m_new).astype(MM_DTYPE)
        alpha = jnp.exp(m - m_new)
        m_ref[j, :, cols] = m_new
        yield
        pv = _dot(vaug[j, :, pl.ds(k0, tk)], pr)
        acc_ref[j, :, cols] = alpha * acc_ref[j, :, cols] + pv

    def units(base, n_blocks, diag):
        out = []
        for c in range(n_blocks):
            for qc in range(c if diag else 0, n_sub):
                for j in range(2):
                    out.append(unit(j, pl.multiple_of(base + c * tk, tk), qc, diag and qc == c))
        return out

    trip = min(FOX_KEYS_PER_TRIP, t)

    def body(kj, _):
        _run_skewed(units(kj * trip, trip // tk, False))
        return 0

    lax.fori_loop(0, i * (t // trip), body, 0)
    _run_skewed(units(i * t, n_sub, True))
    outs = [acc_ref[j, :HEAD_DIM] / acc_ref[j, HEAD_DIM:HEAD_DIM + 1] for j in range(2)]
    o_ref[...] = _gated_t(g_ref[...], outs[0], outs[1]).astype(o_ref.dtype)


def _fox_attention(qt, k, vt, g, cum):
    _, b, _, s = qt.shape
    t = min(TQ_ATT, s)
    return pl.pallas_call(
        _fox_kernel,
        grid=(b, W_GROUP // LANES, s // t),
        in_specs=_t_specs(2, t, s) + [
            pl.BlockSpec((None, s, F_PAD), lambda b_, p, i: (b_, 0, 0))],
        out_specs=pl.BlockSpec((None, t, LANES), _nat_tile(t)),
        out_shape=jax.ShapeDtypeStruct((b, s, W_GROUP), MM_DTYPE),
        scratch_shapes=[pltpu.VMEM((2, s, LANES), MM_DTYPE),
                        pltpu.VMEM((2, FOX_V_ROWS, s), MM_DTYPE),
                        pltpu.VMEM((2, 1, t), F32), pltpu.VMEM((2, FOX_V_ROWS, t), F32)],
        compiler_params=_ATT_PARAMS,
        name="fox_attention",
    )(qt, k, vt, g, cum)


def _out_kernel(x_ref, a_ref, b_ref, c_ref, w_ref, o_ref):
    acc = _dot(a_ref[...], w_ref[0]) + _dot(b_ref[...], w_ref[1]) + _dot(c_ref[...], w_ref[2])
    o_ref[...] = x_ref[...] + acc


def _out_projection(x, ma, mb, mc, w):
    b, s, d = x.shape
    tm = min(TM_PROJ, s)
    mix = pl.BlockSpec((None, tm, W_GROUP), lambda bi, i: (bi, i, 0))
    return pl.pallas_call(
        _out_kernel,
        grid=(b, s // tm),
        in_specs=[pl.BlockSpec((None, tm, d), lambda bi, i: (bi, i, 0)), mix, mix, mix,
                  pl.BlockSpec((N_GROUPS, W_GROUP, d), lambda bi, i: (0, 0, 0))],
        out_specs=pl.BlockSpec((None, tm, d), lambda bi, i: (bi, i, 0)),
        out_shape=jax.ShapeDtypeStruct(x.shape, x.dtype),
        compiler_params=pltpu.CompilerParams(
            dimension_semantics=("arbitrary", "arbitrary"), vmem_limit_bytes=VMEM_LIMIT),
        name="out_proj",
    )(x, ma, mb, mc, w)


def _constants():
    r = np.arange(MXU_DIM)
    bd = (r[:, None] // HEAD_DIM == r[None, :] // HEAD_DIM).astype(np.float32)
    r = np.arange(TK_SB)
    u_sb = -(r[None, :] >= r[:, None]).astype(np.float32)
    r = np.arange(CUM_BLK)
    low = (r[None, :] <= r[:, None]).astype(np.float32)
    return (jnp.asarray(bd, MM_DTYPE), jnp.asarray(u_sb, MM_DTYPE), jnp.asarray(low, MM_DTYPE))


def _proj_weights(ng, w_in, qn_ch, kn_ch, qn_fox, kn_fox, bd):
    d = w_in.shape[0]
    w_in = w_in.astype(MM_DTYPE)
    w4 = w_in[:, :N_GROUPS * 4 * W_GROUP].reshape(d, N_GROUPS, 4, W_GROUP)
    w_nat = jnp.transpose(w4[:, :, (1, 3), :], (2, 1, 0, 3)).reshape(2 * N_GROUPS, d, W_GROUP)
    w_tr = jnp.transpose(w4[:, :, (0, 2), :], (2, 1, 3, 0)).reshape(2 * N_GROUPS, W_GROUP, d)
    w_f = jnp.pad(w_in[:, N_GROUPS * 4 * W_GROUP:], ((0, 0), (0, F_PAD - H_GROUP)))
    grow = jnp.stack([jnp.tile(kn_ch, H_GROUP), jnp.tile(kn_fox, H_GROUP)])[:, None, :].astype(F32)
    gcol = jnp.stack([qn_ch, qn_fox])[:, :, None].astype(F32)
    return (ng[None, :], w_nat, w_tr, w_f, grow, gcol, bd)


def _mixers(projected, b_forget, rel_bias, u_sb, low):
    qt, k, vt, g, f = projected
    bias_row = jnp.pad(b_forget.astype(F32), (0, F_PAD - H_GROUP))[None, :]
    cum = _cum_forget(f, bias_row, low)
    m_sb = _sb_attention(qt, k, vt, g, u_sb)
    m_ch = _chunk_attention(qt, k, vt, g, _chunk_bias_table(rel_bias))
    m_fx = _fox_attention(qt, k, vt, g, cum)
    return m_sb, m_ch, m_fx


def kernel(x, norm_g, w_in, b_forget, q_norm_ch, k_norm_ch, q_norm_fox, k_norm_fox, rel_bias, w_out):
    bd, u_sb, low = _constants()
    depth, d = norm_g.shape
    weights = [_proj_weights(norm_g[l], w_in[l], q_norm_ch[l], k_norm_ch[l],
                             q_norm_fox[l], k_norm_fox[l], bd) for l in range(depth)]
    w_o = [w_out[l].reshape(N_GROUPS, W_GROUP, d).astype(MM_DTYPE) for l in range(depth)]
    projected = _projection(x, weights[0])
    for l in range(depth):
        mixed = _mixers(projected, b_forget[l], rel_bias[l], u_sb, low)
        if l + 1 < depth:
            x, *projected = _projection(x, weights[l + 1], prev=(*mixed, w_o[l]))
        else:
            x = _out_projection(x, *mixed, w_o[l])
    return x
```

```python
import jax
import jax.numpy as jnp
import numpy as np
from jax import lax
from jax.experimental import pallas as pl
from jax.experimental.pallas import tpu as pltpu

D_MODEL = 1024
HEAD_DIM = 64
H_GROUP = 8
W_GROUP = H_GROUP * HEAD_DIM
N_GROUPS = 3
CHUNK = 64
N_LEFT_CHUNKS = 8
LEFT = N_LEFT_CHUNKS * CHUNK
REL_CLIP = 128
EPS = 1e-6
SCALE = HEAD_DIM ** -0.5
LOG2E = 1.4426950408889634

LANES = 128
MXU_DIM = 256
F_PAD = LANES
NEG = -1e30

MM_DTYPE = jnp.bfloat16
F32 = jnp.float32

TM_PROJ = 512
TQ_ATT = 2048
FOX_KEYS_PER_TRIP = 1024
BF16_ROWS = 16
FOX_V_ROWS = HEAD_DIM + BF16_ROWS
FOX_SKEW = 5
SB_SKEW = 3
SB_STATIC_DIAGS = 2
SB_DEAD_LOG = -104.0
TK_SB = MXU_DIM
TQ_CHUNK = MXU_DIM
WIN_CHUNK = LEFT + TQ_CHUNK
TS_CHUNK = 4096
CUM_BLK = 512
VMEM_LIMIT = 56 * 1024 * 1024


def _dot(a, b):
    return jnp.dot(a, b, preferred_element_type=F32)


def _dot_nt(a, b):
    return lax.dot_general(a, b, (((1,), (1,)), ((), ())), preferred_element_type=F32)


def _split2(x):
    hi = x.astype(MM_DTYPE)
    lo = (x - hi.astype(F32)).astype(MM_DTYPE)
    return hi, lo


def _split3(x):
    hi = x.astype(MM_DTYPE)
    mid, lo = _split2(x - hi.astype(F32))
    return hi, mid, lo


def _head_mask(shape, j, axis):
    idx = lax.broadcasted_iota(jnp.int32, shape, axis)
    return (idx >= j * HEAD_DIM) & (idx < (j + 1) * HEAD_DIM)


def _norm_rows(y, gain_row, bd):
    sq = (y * y).astype(MM_DTYPE)
    parts = []
    for c in range(W_GROUP // MXU_DIM):
        sl = slice(c * MXU_DIM, (c + 1) * MXU_DIM)
        parts.append(_dot(sq[:, sl], bd))
    ssq = jnp.concatenate(parts, axis=1)
    return y * lax.rsqrt(ssq * (1.0 / HEAD_DIM) + EPS) * gain_row


def _norm_cols(yt, gain_col):
    y3 = yt.reshape(H_GROUP, HEAD_DIM, yt.shape[1])
    ssq = jnp.sum(y3 * y3, axis=1, keepdims=True)
    y3 = y3 * lax.rsqrt(ssq * (1.0 / HEAD_DIM) + EPS) * gain_col
    return y3.reshape(yt.shape)


def _proj_kernel(x_ref, *refs):
    _project(x_ref[...], *refs)


def _out_proj_kernel(x_ref, a_ref, b_ref, c_ref, wo_ref, *refs):
    *proj_refs, xo_ref = refs
    x = x_ref[...] + (_dot(a_ref[...], wo_ref[0]) + _dot(b_ref[...], wo_ref[1])
                      + _dot(c_ref[...], wo_ref[2]))
    xo_ref[...] = x
    _project(x, *proj_refs)


def _project(x, ng_ref, wn_ref, wt_ref, wf_ref, grow_ref, gcol_ref, bd_ref,
             qt_ref, k_ref, vt_ref, g_ref, f_ref):
    h = x * lax.rsqrt(jnp.mean(x * x, axis=-1, keepdims=True) + EPS) * ng_ref[...]
    hb = h.astype(MM_DTYPE)
    bd = bd_ref[...]
    dt = qt_ref.dtype
    for grp in range(N_GROUPS):
        qt = _dot_nt(wt_ref[grp], hb)
        k = _dot(hb, wn_ref[grp])
        if grp > 0:
            qt = _norm_cols(qt, gcol_ref[grp - 1])
            k = _norm_rows(k, grow_ref[grp - 1], bd)
        qt_ref[grp] = (qt * SCALE).astype(dt)
        k_ref[grp] = k.astype(dt)
        vt_ref[grp] = _dot_nt(wt_ref[N_GROUPS + grp], hb).astype(dt)
        g_ref[grp] = _dot(hb, wn_ref[N_GROUPS + grp])
    f_ref[...] = _dot(hb, wf_ref[...])


def _projection(x, proj_weights, prev=None):
    b, s, d = x.shape
    tm = min(TM_PROJ, s)
    const = dict(pipeline_mode=pl.Buffered(1))
    row_spec = pl.BlockSpec((None, tm, d), lambda bi, i: (bi, i, 0))
    nat_spec = pl.BlockSpec((N_GROUPS, None, tm, W_GROUP), lambda bi, i: (0, bi, i, 0))
    tr_spec = pl.BlockSpec((N_GROUPS, None, W_GROUP, tm), lambda bi, i: (0, bi, 0, i))
    nat_shape = jax.ShapeDtypeStruct((N_GROUPS, b, s, W_GROUP), MM_DTYPE)
    tr_shape = jax.ShapeDtypeStruct((N_GROUPS, b, W_GROUP, s), MM_DTYPE)
    prev_specs, extra_out_specs, extra_out_shapes = [], [], []
    if prev is not None:
        mix = pl.BlockSpec((None, tm, W_GROUP), lambda bi, i: (bi, i, 0))
        prev_specs = [mix, mix, mix,
                      pl.BlockSpec((N_GROUPS, W_GROUP, d), lambda bi, i: (0, 0, 0), **const)]
        extra_out_specs = [row_spec]
        extra_out_shapes = [jax.ShapeDtypeStruct(x.shape, x.dtype)]
    outs = pl.pallas_call(
        _proj_kernel if prev is None else _out_proj_kernel,
        grid=(b, s // tm),
        in_specs=[row_spec] + prev_specs + [
            pl.BlockSpec((1, d), lambda bi, i: (0, 0)),
            pl.BlockSpec((2 * N_GROUPS, d, W_GROUP), lambda bi, i: (0, 0, 0), **const),
            pl.BlockSpec((2 * N_GROUPS, W_GROUP, d), lambda bi, i: (0, 0, 0), **const),
            pl.BlockSpec((d, F_PAD), lambda bi, i: (0, 0), **const),
            pl.BlockSpec((2, 1, W_GROUP), lambda bi, i: (0, 0, 0)),
            pl.BlockSpec((2, HEAD_DIM, 1), lambda bi, i: (0, 0, 0)),
            pl.BlockSpec((MXU_DIM, MXU_DIM), lambda bi, i: (0, 0)),
        ],
        out_specs=[
            tr_spec, nat_spec, tr_spec,
            pl.BlockSpec((N_GROUPS, None, tm, W_GROUP), lambda bi, i: (0, bi, i, 0)),
            pl.BlockSpec((None, tm, F_PAD), lambda bi, i: (bi, i, 0)),
        ] + extra_out_specs,
        out_shape=[
            tr_shape, nat_shape, tr_shape,
            jax.ShapeDtypeStruct((N_GROUPS, b, s, W_GROUP), F32),
            jax.ShapeDtypeStruct((b, s, F_PAD), F32),
        ] + extra_out_shapes,
        compiler_params=pltpu.CompilerParams(
            dimension_semantics=("arbitrary", "arbitrary"), vmem_limit_bytes=VMEM_LIMIT),
        name="proj" if prev is None else "out_proj_proj",
    )(x, *(prev or ()), *proj_weights)
    return outs if prev is None else (outs[-1], *outs[:-1])


def _cum_kernel(f_ref, b_ref, l_ref, o_ref):
    z = f_ref[...] + b_ref[...]
    lf = jnp.minimum(z, 0.0) - jnp.log1p(jnp.exp(-jnp.abs(z)))
    low = l_ref[...]
    carry = jnp.zeros((1, F_PAD), F32)
    for c in range(lf.shape[0] // CUM_BLK):
        rows = slice(c * CUM_BLK, (c + 1) * CUM_BLK)
        hi, mid, lo = _split3(lf[rows])
        cs = (_dot(low, hi) + _dot(low, mid)) + _dot(low, lo) + carry
        o_ref[rows] = cs
        carry = cs[CUM_BLK - 1:CUM_BLK, :]


def _cum_forget(f, bias_row, low):
    b, s, _ = f.shape
    return pl.pallas_call(
        _cum_kernel,
        grid=(b,),
        in_specs=[
            pl.BlockSpec((None, s, F_PAD), lambda bi: (bi, 0, 0)),
            pl.BlockSpec((1, F_PAD), lambda bi: (0, 0)),
            pl.BlockSpec((CUM_BLK, CUM_BLK), lambda bi: (0, 0)),
        ],
        out_specs=pl.BlockSpec((None, s, F_PAD), lambda bi: (bi, 0, 0)),
        out_shape=jax.ShapeDtypeStruct((b, s, F_PAD), F32),
        compiler_params=pltpu.CompilerParams(dimension_semantics=("arbitrary",)),
        name="cum_forget",
    )(f, bias_row, low)


def _gated(g, o0, o1):
    lane = lax.broadcasted_iota(jnp.int32, o0.shape, 1)
    o = jnp.where(lane < HEAD_DIM, o0, o1)
    return o * (g * jax.nn.sigmoid(g))


def _gated_t(g, ot0, ot1):
    o = jnp.concatenate([ot0, ot1], axis=0).T
    return o * (g * jax.nn.sigmoid(g))


def _nat_tile(tq):
    return lambda b, p, i: (b, i, p)


def _t_specs(grp, tq, s):
    return [
        pl.BlockSpec((None, None, LANES, tq), lambda b, p, i: (grp, b, p, i)),
        pl.BlockSpec((None, None, s, LANES), lambda b, p, i: (grp, b, 0, p)),
        pl.BlockSpec((None, None, LANES, s), lambda b, p, i: (grp, b, p, 0)),
        pl.BlockSpec((None, None, tq, LANES), lambda b, p, i: (grp, b, i, p)),
    ]


_ATT_PARAMS = pltpu.CompilerParams(
    dimension_semantics=("arbitrary", "arbitrary", "arbitrary"), vmem_limit_bytes=VMEM_LIMIT)


def _neg_abs(x):
    bits = lax.bitcast_convert_type(x, jnp.uint32) | jnp.uint32(0x80000000)
    return lax.bitcast_convert_type(bits, F32)


def _run_skewed(units):
    pending = list(units)
    active = []
    while pending or active:
        if pending:
            active.append(pending.pop(0))
        for g in list(active):
            try:
                next(g)
            except StopIteration:
                active.remove(g)


def _masked_heads_t(qt):
    q = qt.astype(F32)
    return [jnp.where(_head_mask(q.shape, j, 0), q, 0.0).astype(MM_DTYPE) for j in range(2)]


def _sb_kernel(qt_ref, k_ref, vt_ref, g_ref, u_ref, o_ref, acc_ref, carry_ref):
    t = qt_ref.shape[1]
    tk = u_ref.shape[0]
    n_sub = t // tk
    qm = _masked_heads_t(qt_ref[...])
    u = u_ref[...]
    acc_ref[...] = jnp.zeros_like(acc_ref)
    carry_ref[...] = jnp.zeros_like(carry_ref)

    def unit(j, kb, qc, diag):
        cols = slice(qc * tk, (qc + 1) * tk)
        static = isinstance(kb, int)
        k0 = kb * tk if static else pl.multiple_of(jnp.maximum(kb, 0) * tk, tk)
        z = _dot(k_ref[pl.ds(k0, tk), :], qm[j][:, cols])
        for _ in range(SB_SKEW):
            yield
        sp = jnp.maximum(z, 0.0) + jnp.log(1.0 + jnp.exp(_neg_abs(z)))
        if diag:
            strict = (lax.broadcasted_iota(jnp.int32, z.shape, 0)
                      < lax.broadcasted_iota(jnp.int32, z.shape, 1))
            sp = jnp.where(strict, sp, 0.0)
        spb = sp.astype(MM_DTYPE)
        yield
        incl = _dot(u, spb)
        for _ in range(SB_SKEW):
            yield
        w = jnp.exp(z + incl)
        if diag:
            w = jnp.where(strict, w, 0.0)
        wb = w.astype(MM_DTYPE)
        yield
        pv = _dot(vt_ref[j * HEAD_DIM:(j + 1) * HEAD_DIM, pl.ds(k0, tk)], wb)
        carry = carry_ref[j, :, cols]
        scale = jnp.exp(carry)
        step = incl[:1]
        if not static:
            scale = jnp.where(kb >= 0, scale, 0.0)
            step = jnp.where(kb >= 0, step, 0.0)
        acc_ref[j, :, cols] += scale * pv
        carry_ref[j, :, cols] = carry + step

    def diagonal(d, first_qc, masked=False):
        return [unit(j, qc - d, qc, masked) for qc in range(first_qc, n_sub) for j in range(2)]

    def live(d):
        lane = lax.broadcasted_iota(jnp.int32, carry_ref.shape, 2)
        return jnp.max(jnp.where(lane >= d * tk, carry_ref[...], NEG)) >= SB_DEAD_LOG

    n_static = min(SB_STATIC_DIAGS, n_sub)
    _run_skewed([un for d in range(n_static) for un in diagonal(d, d, masked=d == 0)])

    def body(state):
        d, _ = state
        _run_skewed(diagonal(d, n_static))
        return d + 1, live(d + 1)

    lax.while_loop(lambda st: (st[0] < n_sub) & st[1], body, (n_static, live(n_static)))
    o_ref[...] = _gated_t(g_ref[...], acc_ref[0], acc_ref[1]).astype(o_ref.dtype)


def _sb_attention(qt, k, vt, g, u):
    _, b, _, s = qt.shape
    t = s
    tk = u.shape[0]
    return pl.pallas_call(
        _sb_kernel,
        grid=(b, W_GROUP // LANES, s // t),
        in_specs=_t_specs(0, t, s) + [pl.BlockSpec((tk, tk), lambda b_, p, i: (0, 0))],
        out_specs=pl.BlockSpec((None, t, LANES), _nat_tile(t)),
        out_shape=jax.ShapeDtypeStruct((b, s, W_GROUP), MM_DTYPE),
        scratch_shapes=[pltpu.VMEM((2, HEAD_DIM, t), F32), pltpu.VMEM((2, 1, t), F32)],
        compiler_params=_ATT_PARAMS,
        name="sb_attention",
    )(qt, k, vt, g, u)


def _chunk_kernel(qt_ref, k_ref, vt_ref, g_ref, bias_ref, o_ref, kpad, vaug, ot_ref):
    i = pl.program_id(2)
    ts = qt_ref.shape[1]
    win, tq = bias_ref.shape[1:]
    s = k_ref.shape[0]
    n_ones = vaug.shape[1] - HEAD_DIM

    @pl.when(i == 0)
    def _():
        kpad[:LEFT] = jnp.zeros((LEFT, LANES), kpad.dtype)
        kpad[LEFT:] = k_ref[...]
        vt = vt_ref[...].astype(F32)
        for j in range(2):
            vaug[j, :, :LEFT] = jnp.zeros((vaug.shape[1], LEFT), vaug.dtype)
            vaug[j, :, LEFT:] = jnp.concatenate(
                [vt[j * HEAD_DIM:(j + 1) * HEAD_DIM], jnp.ones((n_ones, s), F32)],
                axis=0).astype(vaug.dtype)

    qm = _masked_heads_t(qt_ref[...])

    def unit(j, r, first_step):
        cols = slice(r * tq, (r + 1) * tq)
        r0 = pl.multiple_of(i * ts + r * tq, tq)
        zs = []
        for kb in range(win // tq):
            k0 = pl.multiple_of(r0 + kb * tq, tq)
            z = _dot(kpad[pl.ds(k0, tq), :], qm[j][:, cols]) + bias_ref[j, kb * tq:(kb + 1) * tq]
            lo_key = LEFT - r * tq - kb * tq
            if first_step and lo_key > 0:
                key = lax.broadcasted_iota(jnp.int32, z.shape, 0)
                z = jnp.where(key >= lo_key, z, NEG)
            zs.append(z)
            yield
        m = zs[0].max(axis=0, keepdims=True)
        for z in zs[1:]:
            m = jnp.maximum(m, z.max(axis=0, keepdims=True))
        ps = []
        for z in zs:
            ps.append(jnp.exp(z - m).astype(MM_DTYPE))
            yield
        pv = None
        for kb, p in enumerate(ps):
            k0 = pl.multiple_of(r0 + kb * tq, tq)
            term = _dot(vaug[j, :, pl.ds(k0, tq)], p)
            pv = term if pv is None else pv + term
        ot_ref[j * HEAD_DIM:(j + 1) * HEAD_DIM, cols] = pv[:HEAD_DIM] / pv[HEAD_DIM:HEAD_DIM + 1]

    def run(first_step):
        _run_skewed([unit(j, r, first_step) for r in range(ts // tq) for j in range(2)])

    if s == ts:
        run(True)
    else:
        pl.when(i == 0)(lambda: run(True))
        pl.when(i > 0)(lambda: run(False))

    o_ref[...] = _gated_t(g_ref[...], ot_ref[:HEAD_DIM], ot_ref[HEAD_DIM:]).astype(o_ref.dtype)


def _chunk_attention(qt, k, vt, g, bias):
    _, b, _, s = qt.shape
    ts = min(TS_CHUNK, s)
    return pl.pallas_call(
        _chunk_kernel,
        grid=(b, W_GROUP // LANES, s // ts),
        in_specs=_t_specs(1, ts, s) + [
            pl.BlockSpec((2, WIN_CHUNK, TQ_CHUNK), lambda b_, p, i: (p, 0, 0))],
        out_specs=pl.BlockSpec((None, ts, LANES), _nat_tile(ts)),
        out_shape=jax.ShapeDtypeStruct((b, s, W_GROUP), MM_DTYPE),
        scratch_shapes=[pltpu.VMEM((LEFT + s, LANES), MM_DTYPE),
                        pltpu.VMEM((2, FOX_V_ROWS, LEFT + s), MM_DTYPE),
                        pltpu.VMEM((LANES, ts), F32)],
        compiler_params=_ATT_PARAMS,
        name="chunk_attention",
    )(qt, k, vt, g, bias)


N_DIAG = WIN_CHUNK + TQ_CHUNK


def _bias_kernel(v_ref, o_ref):
    x = jnp.broadcast_to(v_ref[...], (WIN_CHUNK, N_DIAG))
    y = pltpu.roll(x, 1, 1, stride=1, stride_axis=0)
    t = y[:, WIN_CHUNK:]
    c = lax.broadcasted_iota(jnp.int32, t.shape, 0)
    r = lax.broadcasted_iota(jnp.int32, t.shape, 1)
    band = c - (r - (r & (CHUNK - 1)))
    o_ref[...] = jnp.where((band >= 0) & (band < LEFT + CHUNK), t, NEG)


def _chunk_bias_table(rel_bias):
    kk = np.arange(N_DIAG - 1)
    rel = kk - (WIN_CHUNK - 1) + LEFT
    vec = rel_bias[:, np.clip(rel, -REL_CLIP, REL_CLIP) + REL_CLIP].astype(F32)
    vec = jnp.pad(vec, ((0, 0), (0, 1)))[:, None, :]
    h = vec.shape[0]
    return pl.pallas_call(
        _bias_kernel,
        grid=(h,),
        in_specs=[pl.BlockSpec((None, 1, N_DIAG), lambda i: (i, 0, 0))],
        out_specs=pl.BlockSpec((None, WIN_CHUNK, TQ_CHUNK), lambda i: (i, 0, 0)),
        out_shape=jax.ShapeDtypeStruct((h, WIN_CHUNK, TQ_CHUNK), F32),
        compiler_params=pltpu.CompilerParams(dimension_semantics=("arbitrary",)),
        name="chunk_bias",
    )(vec)


def _fox_kernel(qt_ref, k_ref, vt_ref, g_ref, c_ref, o_ref, kaug, vaug, m_ref, acc_ref):
    i = pl.program_id(2)
    p = pl.program_id(1)
    t = qt_ref.shape[1]
    aug0 = [HEAD_DIM * (1 - j) for j in range(2)]
    n_ones = vaug.shape[1] - HEAD_DIM

    @pl.when(i == 0)
    def _():
        k = k_ref[...].astype(F32)
        vt = vt_ref[...].astype(F32)
        pieces = jnp.concatenate(_split3(-c_ref[...]), axis=1)
        row = lax.broadcasted_iota(jnp.int32, (3 * LANES, 2 * LANES), 0)
        col = lax.broadcasted_iota(jnp.int32, (3 * LANES, 2 * LANES), 1)
        place = None
        for j in range(2):
            for c in range(3):
                hit = (row == c * LANES + 2 * p + j) & (col == j * LANES + aug0[j] + c)
                place = hit if place is None else place | hit
        aug = _dot(pieces, jnp.where(place, 1.0, 0.0).astype(MM_DTYPE))
        for j in range(2):
            kaug[j] = jnp.where(_head_mask(k.shape, j, 1), k,
                                aug[:, j * LANES:(j + 1) * LANES]).astype(kaug.dtype)
            vaug[j] = jnp.concatenate(
                [vt[j * HEAD_DIM:(j + 1) * HEAD_DIM], jnp.ones((n_ones, vt.shape[1]), F32)],
                axis=0).astype(vaug.dtype)

    q = qt_ref[...].astype(F32)
    row = lax.broadcasted_iota(jnp.int32, q.shape, 0)
    qa = []
    for j in range(2):
        ones = (row >= aug0[j]) & (row < aug0[j] + 3)
        qa.append(jnp.where(_head_mask(q.shape, j, 0), q, jnp.where(ones, 1.0, 0.0)).astype(MM_DTYPE))
    m_ref[...] = jnp.full(m_ref.shape, NEG, F32)
    acc_ref[...] = jnp.zeros_like(acc_ref)

    tk = TK_SB
    n_sub = t // tk

    def unit(j, k0, qc, diag):
        cols = slice(qc * tk, (qc + 1) * tk)
        z = _dot(kaug[j, pl.ds(k0, tk), :], qa[j][:, cols])
        for _ in range(FOX_SKEW):
            yield
        if diag:
            causal = (lax.broadcasted_iota(jnp.int32, z.shape, 0)
                      <= lax.broadcasted_iota(jnp.int32, z.shape, 1))
            z = jnp.where(causal, z, NEG)
        m = m_ref[j, :, cols]
        m_new = jnp.maximum(m, jnp.max(z, axis=0, keepdims=True))
        pr = jnp.exp(z - m_new).astype(MM_DTYPE)
        alpha = jnp.exp(m - m_new)
        m_ref[j, :, cols] = m_new
        yield
        pv = _dot(vaug[j, :, pl.ds(k0, tk)], pr)
        acc_ref[j, :, cols] = alpha * acc_ref[j, :, cols] + pv

    def units(base, n_blocks, diag):
        out = []
        for c in range(n_blocks):
            for qc in range(c if diag else 0, n_sub):
                for j in range(2):
                    out.append(unit(j, pl.multiple_of(base + c * tk, tk), qc, diag and qc == c))
        return out

    trip = min(FOX_KEYS_PER_TRIP, t)

    def body(kj, _):
        _run_skewed(units(kj * trip, trip // tk, False))
        return 0

    lax.fori_loop(0, i * (t // trip), body, 0)
    _run_skewed(units(i * t, n_sub, True))
    outs = [acc_ref[j, :HEAD_DIM] / acc_ref[j, HEAD_DIM:HEAD_DIM + 1] for j in range(2)]
    o_ref[...] = _gated_t(g_ref[...], outs[0], outs[1]).astype(o_ref.dtype)


def _fox_attention(qt, k, vt, g, cum):
    _, b, _, s = qt.shape
    t = min(TQ_ATT, s)
    return pl.pallas_call(
        _fox_kernel,
        grid=(b, W_GROUP // LANES, s // t),
        in_specs=_t_specs(2, t, s) + [
            pl.BlockSpec((None, s, F_PAD), lambda b_, p, i: (b_, 0, 0))],
        out_specs=pl.BlockSpec((None, t, LANES), _nat_tile(t)),
        out_shape=jax.ShapeDtypeStruct((b, s, W_GROUP), MM_DTYPE),
        scratch_shapes=[pltpu.VMEM((2, s, LANES), MM_DTYPE),
                        pltpu.VMEM((2, FOX_V_ROWS, s), MM_DTYPE),
                        pltpu.VMEM((2, 1, t), F32), pltpu.VMEM((2, FOX_V_ROWS, t), F32)],
        compiler_params=_ATT_PARAMS,
        name="fox_attention",
    )(qt, k, vt, g, cum)


def _out_kernel(x_ref, a_ref, b_ref, c_ref, w_ref, o_ref):
    acc = _dot(a_ref[...], w_ref[0]) + _dot(b_ref[...], w_ref[1]) + _dot(c_ref[...], w_ref[2])
    o_ref[...] = x_ref[...] + acc


def _out_projection(x, ma, mb, mc, w):
    b, s, d = x.shape
    tm = min(TM_PROJ, s)
    mix = pl.BlockSpec((None, tm, W_GROUP), lambda bi, i: (bi, i, 0))
    return pl.pallas_call(
        _out_kernel,
        grid=(b, s // tm),
        in_specs=[pl.BlockSpec((None, tm, d), lambda bi, i: (bi, i, 0)), mix, mix, mix,
                  pl.BlockSpec((N_GROUPS, W_GROUP, d), lambda bi, i: (0, 0, 0))],
        out_specs=pl.BlockSpec((None, tm, d), lambda bi, i: (bi, i, 0)),
        out_shape=jax.ShapeDtypeStruct(x.shape, x.dtype),
        compiler_params=pltpu.CompilerParams(
            dimension_semantics=("arbitrary", "arbitrary"), vmem_limit_bytes=VMEM_LIMIT),
        name="out_proj",
    )(x, ma, mb, mc, w)


def _constants():
    r = np.arange(MXU_DIM)
    bd = (r[:, None] // HEAD_DIM == r[None, :] // HEAD_DIM).astype(np.float32)
    r = np.arange(TK_SB)
    u_sb = -(r[None, :] >= r[:, None]).astype(np.float32)
    r = np.arange(CUM_BLK)
    low = (r[None, :] <= r[:, None]).astype(np.float32)
    return (jnp.asarray(bd, MM_DTYPE), jnp.asarray(u_sb, MM_DTYPE), jnp.asarray(low, MM_DTYPE))


def _proj_weights(ng, w_in, qn_ch, kn_ch, qn_fox, kn_fox, bd):
    d = w_in.shape[0]
    w_in = w_in.astype(MM_DTYPE)
    w4 = w_in[:, :N_GROUPS * 4 * W_GROUP].reshape(d, N_GROUPS, 4, W_GROUP)
    w_nat = jnp.transpose(w4[:, :, (1, 3), :], (2, 1, 0, 3)).reshape(2 * N_GROUPS, d, W_GROUP)
    w_tr = jnp.transpose(w4[:, :, (0, 2), :], (2, 1, 3, 0)).reshape(2 * N_GROUPS, W_GROUP, d)
    w_f = jnp.pad(w_in[:, N_GROUPS * 4 * W_GROUP:], ((0, 0), (0, F_PAD - H_GROUP)))
    grow = jnp.stack([jnp.tile(kn_ch, H_GROUP), jnp.tile(kn_fox, H_GROUP)])[:, None, :].astype(F32)
    gcol = jnp.stack([qn_ch, qn_fox])[:, :, None].astype(F32)
    return (ng[None, :], w_nat, w_tr, w_f, grow, gcol, bd)


def _mixers(projected, b_forget, rel_bias, u_sb, low):
    qt, k, vt, g, f = projected
    bias_row = jnp.pad(b_forget.astype(F32), (0, F_PAD - H_GROUP))[None, :]
    cum = _cum_forget(f, bias_row, low)
    m_sb = _sb_attention(qt, k, vt, g, u_sb)
    m_ch = _chunk_attention(qt, k, vt, g, _chunk_bias_table(rel_bias))
    m_fx = _fox_attention(qt, k, vt, g, cum)
    return m_sb, m_ch, m_fx


def kernel(x, norm_g, w_in, b_forget, q_norm_ch, k_norm_ch, q_norm_fox, k_norm_fox, rel_bias, w_out):
    bd, u_sb, low = _constants()
    depth, d = norm_g.shape
    weights = [_proj_weights(norm_g[l], w_in[l], q_norm_ch[l], k_norm_ch[l],
                             q_norm_fox[l], k_norm_fox[l], bd) for l in range(depth)]
    w_o = [w_out[l].reshape(N_GROUPS, W_GROUP, d).astype(MM_DTYPE) for l in range(depth)]
    projected = _projection(x, weights[0])
    for l in range(depth):
        mixed = _mixers(projected, b_forget[l], rel_bias[l], u_sb, low)
        if l + 1 < depth:
            x, *projected = _projection(x, weights[l + 1], prev=(*mixed, w_o[l]))
        else:
            x = _out_projection(x, *mixed, w_o[l])
    return x
```

```python
import jax
import jax.numpy as jnp
import numpy as np
from jax import lax
from jax.experimental import pallas as pl
from jax.experimental.pallas import tpu as pltpu

D_MODEL = 1024
HEAD_DIM = 64
H_GROUP = 8
W_GROUP = H_GROUP * HEAD_DIM
N_GROUPS = 3
CHUNK = 64
N_LEFT_CHUNKS = 8
LEFT = N_LEFT_CHUNKS * CHUNK
REL_CLIP = 128
EPS = 1e-6
SCALE = HEAD_DIM ** -0.5

LANES = 128
MXU_DIM = 256
F_PAD = LANES
NEG = -1e30

MM_DTYPE = jnp.bfloat16
F32 = jnp.float32

TM_PROJ = 512
TQ_ATT = 4096
FOX_KEYS_PER_TRIP = 1024
BF16_ROWS = 16
FOX_V_ROWS = HEAD_DIM + BF16_ROWS
FOX_SKEW = 5
SB_SKEW = 3
SB_STATIC_DIAGS = 2
SB_DEAD_LOG = -104.0
TK_SB = MXU_DIM
TQ_CHUNK = MXU_DIM
WIN_CHUNK = LEFT + TQ_CHUNK
TS_CHUNK = 4096
CUM_BLK = 512
VMEM_LIMIT = 56 * 1024 * 1024


def _dot(a, b):
    return jnp.dot(a, b, preferred_element_type=F32)


def _dot_nt(a, b):
    return lax.dot_general(a, b, (((1,), (1,)), ((), ())), preferred_element_type=F32)


def _split2(x):
    hi = x.astype(MM_DTYPE)
    lo = (x - hi.astype(F32)).astype(MM_DTYPE)
    return hi, lo


def _split3(x):
    hi = x.astype(MM_DTYPE)
    mid, lo = _split2(x - hi.astype(F32))
    return hi, mid, lo


def _head_mask(shape, j, axis):
    idx = lax.broadcasted_iota(jnp.int32, shape, axis)
    return (idx >= j * HEAD_DIM) & (idx < (j + 1) * HEAD_DIM)


def _norm_rows(y, gain_row, bd):
    sq = (y * y).astype(MM_DTYPE)
    parts = []
    for c in range(W_GROUP // MXU_DIM):
        sl = slice(c * MXU_DIM, (c + 1) * MXU_DIM)
        parts.append(_dot(sq[:, sl], bd))
    ssq = jnp.concatenate(parts, axis=1)
    return y * lax.rsqrt(ssq * (1.0 / HEAD_DIM) + EPS) * gain_row


def _norm_cols(yt, gain_col):
    y3 = yt.reshape(H_GROUP, HEAD_DIM, yt.shape[1])
    ssq = jnp.sum(y3 * y3, axis=1, keepdims=True)
    y3 = y3 * lax.rsqrt(ssq * (1.0 / HEAD_DIM) + EPS) * gain_col
    return y3.reshape(yt.shape)


def _proj_kernel(x_ref, *refs):
    _project(x_ref[...], *refs)


def _out_proj_kernel(x_ref, a_ref, b_ref, c_ref, wo_ref, *refs):
    *proj_refs, xo_ref = refs
    x = x_ref[...] + (_dot(a_ref[...], wo_ref[0]) + _dot(b_ref[...], wo_ref[1])
                      + _dot(c_ref[...], wo_ref[2]))
    xo_ref[...] = x
    _project(x, *proj_refs)


def _project(x, ng_ref, wn_ref, wt_ref, wf_ref, grow_ref, gcol_ref, bd_ref,
             qt_ref, k_ref, vt_ref, g_ref, f_ref):
    h = x * lax.rsqrt(jnp.mean(x * x, axis=-1, keepdims=True) + EPS) * ng_ref[...]
    hb = h.astype(MM_DTYPE)
    bd = bd_ref[...]
    dt = qt_ref.dtype
    for grp in range(N_GROUPS):
        qt = _dot_nt(wt_ref[grp], hb)
        k = _dot(hb, wn_ref[grp])
        if grp > 0:
            qt = _norm_cols(qt, gcol_ref[grp - 1])
            k = _norm_rows(k, grow_ref[grp - 1], bd)
        qt_ref[grp] = (qt * SCALE).astype(dt)
        k_ref[grp] = k.astype(dt)
        vt_ref[grp] = _dot_nt(wt_ref[N_GROUPS + grp], hb).astype(dt)
        g_ref[grp] = _dot(hb, wn_ref[N_GROUPS + grp])
    f_ref[...] = _dot(hb, wf_ref[...])


def _projection(x, proj_weights, prev=None):
    b, s, d = x.shape
    tm = min(TM_PROJ, s)
    const = dict(pipeline_mode=pl.Buffered(1))
    row_spec = pl.BlockSpec((None, tm, d), lambda bi, i: (bi, i, 0))
    nat_spec = pl.BlockSpec((N_GROUPS, None, tm, W_GROUP), lambda bi, i: (0, bi, i, 0))
    tr_spec = pl.BlockSpec((N_GROUPS, None, W_GROUP, tm), lambda bi, i: (0, bi, 0, i))
    nat_shape = jax.ShapeDtypeStruct((N_GROUPS, b, s, W_GROUP), MM_DTYPE)
    tr_shape = jax.ShapeDtypeStruct((N_GROUPS, b, W_GROUP, s), MM_DTYPE)
    prev_specs, extra_out_specs, extra_out_shapes = [], [], []
    if prev is not None:
        mix = pl.BlockSpec((None, tm, W_GROUP), lambda bi, i: (bi, i, 0))
        prev_specs = [mix, mix, mix,
                      pl.BlockSpec((N_GROUPS, W_GROUP, d), lambda bi, i: (0, 0, 0), **const)]
        extra_out_specs = [row_spec]
        extra_out_shapes = [jax.ShapeDtypeStruct(x.shape, x.dtype)]
    outs = pl.pallas_call(
        _proj_kernel if prev is None else _out_proj_kernel,
        grid=(b, s // tm),
        in_specs=[row_spec] + prev_specs + [
            pl.BlockSpec((1, d), lambda bi, i: (0, 0)),
            pl.BlockSpec((2 * N_GROUPS, d, W_GROUP), lambda bi, i: (0, 0, 0), **const),
            pl.BlockSpec((2 * N_GROUPS, W_GROUP, d), lambda bi, i: (0, 0, 0), **const),
            pl.BlockSpec((d, F_PAD), lambda bi, i: (0, 0), **const),
            pl.BlockSpec((2, 1, W_GROUP), lambda bi, i: (0, 0, 0)),
            pl.BlockSpec((2, HEAD_DIM, 1), lambda bi, i: (0, 0, 0)),
            pl.BlockSpec((MXU_DIM, MXU_DIM), lambda bi, i: (0, 0)),
        ],
        out_specs=[
            tr_spec, nat_spec, tr_spec,
            pl.BlockSpec((N_GROUPS, None, tm, W_GROUP), lambda bi, i: (0, bi, i, 0)),
            pl.BlockSpec((None, tm, F_PAD), lambda bi, i: (bi, i, 0)),
        ] + extra_out_specs,
        out_shape=[
            tr_shape, nat_shape, tr_shape,
            jax.ShapeDtypeStruct((N_GROUPS, b, s, W_GROUP), F32),
            jax.ShapeDtypeStruct((b, s, F_PAD), F32),
        ] + extra_out_shapes,
        compiler_params=pltpu.CompilerParams(
            dimension_semantics=("arbitrary", "arbitrary"), vmem_limit_bytes=VMEM_LIMIT),
        name="proj" if prev is None else "out_proj_proj",
    )(x, *(prev or ()), *proj_weights)
    return outs if prev is None else (outs[-1], *outs[:-1])


def _cum_kernel(f_ref, b_ref, l_ref, o_ref):
    z = f_ref[...] + b_ref[...]
    lf = jnp.minimum(z, 0.0) - jnp.log1p(jnp.exp(-jnp.abs(z)))
    low = l_ref[...]
    carry = jnp.zeros((1, F_PAD), F32)
    for c in range(lf.shape[0] // CUM_BLK):
        rows = slice(c * CUM_BLK, (c + 1) * CUM_BLK)
        hi, mid, lo = _split3(lf[rows])
        cs = (_dot(low, hi) + _dot(low, mid)) + _dot(low, lo) + carry
        o_ref[rows] = cs
        carry = cs[CUM_BLK - 1:CUM_BLK, :]


def _cum_forget(f, bias_row, low):
    b, s, _ = f.shape
    return pl.pallas_call(
        _cum_kernel,
        grid=(b,),
        in_specs=[
            pl.BlockSpec((None, s, F_PAD), lambda bi: (bi, 0, 0)),
            pl.BlockSpec((1, F_PAD), lambda bi: (0, 0)),
            pl.BlockSpec((CUM_BLK, CUM_BLK), lambda bi: (0, 0)),
        ],
        out_specs=pl.BlockSpec((None, s, F_PAD), lambda bi: (bi, 0, 0)),
        out_shape=jax.ShapeDtypeStruct((b, s, F_PAD), F32),
        compiler_params=pltpu.CompilerParams(dimension_semantics=("arbitrary",)),
        name="cum_forget",
    )(f, bias_row, low)


def _gated_t(g, ot0, ot1):
    o = jnp.concatenate([ot0, ot1], axis=0).T
    return o * (g * jax.nn.sigmoid(g))


def _out_tile(b, p, i):
    return (b, i, p)


def _t_specs(grp, tq, s):
    return [
        pl.BlockSpec((None, None, LANES, tq), lambda b, p, i: (grp, b, p, i)),
        pl.BlockSpec((None, None, s, LANES), lambda b, p, i: (grp, b, 0, p)),
        pl.BlockSpec((None, None, LANES, s), lambda b, p, i: (grp, b, p, 0)),
        pl.BlockSpec((None, None, tq, LANES), lambda b, p, i: (grp, b, i, p)),
    ]


_ATT_PARAMS = pltpu.CompilerParams(
    dimension_semantics=("arbitrary", "arbitrary", "arbitrary"), vmem_limit_bytes=VMEM_LIMIT)


def _neg_abs(x):
    bits = lax.bitcast_convert_type(x, jnp.uint32) | jnp.uint32(0x80000000)
    return lax.bitcast_convert_type(bits, F32)


def _run_skewed(units):
    pending = list(units)
    active = []
    while pending or active:
        if pending:
            active.append(pending.pop(0))
        for g in list(active):
            try:
                next(g)
            except StopIteration:
                active.remove(g)


def _masked_heads_t(qt):
    q = qt.astype(F32)
    return [jnp.where(_head_mask(q.shape, j, 0), q, 0.0).astype(MM_DTYPE) for j in range(2)]


def _sb_kernel(qt_ref, k_ref, vt_ref, g_ref, u_ref, o_ref, acc_ref, carry_ref):
    t = qt_ref.shape[1]
    tk = u_ref.shape[0]
    n_sub = t // tk
    qm = _masked_heads_t(qt_ref[...])
    u = u_ref[...]
    acc_ref[...] = jnp.zeros_like(acc_ref)
    carry_ref[...] = jnp.zeros_like(carry_ref)

    def unit(j, kb, qc, diag):
        cols = slice(qc * tk, (qc + 1) * tk)
        static = isinstance(kb, int)
        k0 = kb * tk if static else pl.multiple_of(jnp.maximum(kb, 0) * tk, tk)
        z = _dot(k_ref[pl.ds(k0, tk), :], qm[j][:, cols])
        for _ in range(SB_SKEW):
            yield
        sp = jnp.maximum(z, 0.0) + jnp.log(1.0 + jnp.exp(_neg_abs(z)))
        if diag:
            strict = (lax.broadcasted_iota(jnp.int32, z.shape, 0)
                      < lax.broadcasted_iota(jnp.int32, z.shape, 1))
            sp = jnp.where(strict, sp, 0.0)
        spb = sp.astype(MM_DTYPE)
        yield
        incl = _dot(u, spb)
        for _ in range(SB_SKEW):
            yield
        w = jnp.exp(z + incl)
        if diag:
            w = jnp.where(strict, w, 0.0)
        wb = w.astype(MM_DTYPE)
        yield
        pv = _dot(vt_ref[j * HEAD_DIM:(j + 1) * HEAD_DIM, pl.ds(k0, tk)], wb)
        carry = carry_ref[j, :, cols]
        scale = jnp.exp(carry)
        step = incl[:1]
        if not static:
            scale = jnp.where(kb >= 0, scale, 0.0)
            step = jnp.where(kb >= 0, step, 0.0)
        acc_ref[j, :, cols] += scale * pv
        carry_ref[j, :, cols] = carry + step

    def diagonal(d, first_qc, masked=False):
        return [unit(j, qc - d, qc, masked) for qc in range(first_qc, n_sub) for j in range(2)]

    def live(d):
        lane = lax.broadcasted_iota(jnp.int32, carry_ref.shape, 2)
        return jnp.max(jnp.where(lane >= d * tk, carry_ref[...], NEG)) >= SB_DEAD_LOG

    n_static = min(SB_STATIC_DIAGS, n_sub)
    _run_skewed([un for d in range(n_static) for un in diagonal(d, d, masked=d == 0)])

    def body(state):
        d, _ = state
        _run_skewed(diagonal(d, n_static))
        return d + 1, live(d + 1)

    lax.while_loop(lambda st: (st[0] < n_sub) & st[1], body, (n_static, live(n_static)))
    o_ref[...] = _gated_t(g_ref[...], acc_ref[0], acc_ref[1]).astype(o_ref.dtype)


def _sb_attention(qt, k, vt, g, u):
    _, b, _, s = qt.shape
    t = s
    tk = u.shape[0]
    return pl.pallas_call(
        _sb_kernel,
        grid=(b, W_GROUP // LANES, s // t),
        in_specs=_t_specs(0, t, s) + [pl.BlockSpec((tk, tk), lambda b_, p, i: (0, 0))],
        out_specs=pl.BlockSpec((None, t, LANES), _out_tile),
        out_shape=jax.ShapeDtypeStruct((b, s, W_GROUP), MM_DTYPE),
        scratch_shapes=[pltpu.VMEM((2, HEAD_DIM, t), F32), pltpu.VMEM((2, 1, t), F32)],
        compiler_params=_ATT_PARAMS,
        name="sb_attention",
    )(qt, k, vt, g, u)


def _chunk_kernel(qt_ref, k_ref, vt_ref, g_ref, bias_ref, o_ref, kpad, vaug, ot_ref):
    i = pl.program_id(2)
    ts = qt_ref.shape[1]
    win, tq = bias_ref.shape[1:]
    s = k_ref.shape[0]
    n_ones = vaug.shape[1] - HEAD_DIM

    @pl.when(i == 0)
    def _():
        kpad[:LEFT] = jnp.zeros((LEFT, LANES), kpad.dtype)
        kpad[LEFT:] = k_ref[...]
        vt = vt_ref[...].astype(F32)
        for j in range(2):
            vaug[j, :, :LEFT] = jnp.zeros((vaug.shape[1], LEFT), vaug.dtype)
            vaug[j, :, LEFT:] = jnp.concatenate(
                [vt[j * HEAD_DIM:(j + 1) * HEAD_DIM], jnp.ones((n_ones, s), F32)],
                axis=0).astype(vaug.dtype)

    qm = _masked_heads_t(qt_ref[...])

    def unit(j, r, first_step):
        cols = slice(r * tq, (r + 1) * tq)
        r0 = pl.multiple_of(i * ts + r * tq, tq)
        zs = []
        for kb in range(win // tq):
            k0 = pl.multiple_of(r0 + kb * tq, tq)
            z = _dot(kpad[pl.ds(k0, tq), :], qm[j][:, cols]) + bias_ref[j, kb * tq:(kb + 1) * tq]
            lo_key = LEFT - r * tq - kb * tq
            if first_step and lo_key > 0:
                key = lax.broadcasted_iota(jnp.int32, z.shape, 0)
                z = jnp.where(key >= lo_key, z, NEG)
            zs.append(z)
            yield
        m = zs[0].max(axis=0, keepdims=True)
        for z in zs[1:]:
            m = jnp.maximum(m, z.max(axis=0, keepdims=True))
        ps = []
        for z in zs:
            ps.append(jnp.exp(z - m).astype(MM_DTYPE))
            yield
        pv = None
        for kb, p in enumerate(ps):
            k0 = pl.multiple_of(r0 + kb * tq, tq)
            term = _dot(vaug[j, :, pl.ds(k0, tq)], p)
            pv = term if pv is None else pv + term
        ot_ref[j * HEAD_DIM:(j + 1) * HEAD_DIM, cols] = pv[:HEAD_DIM] / pv[HEAD_DIM:HEAD_DIM + 1]

    def run(first_step):
        _run_skewed([unit(j, r, first_step) for r in range(ts // tq) for j in range(2)])

    if s == ts:
        run(True)
    else:
        pl.when(i == 0)(lambda: run(True))
        pl.when(i > 0)(lambda: run(False))

    o_ref[...] = _gated_t(g_ref[...], ot_ref[:HEAD_DIM], ot_ref[HEAD_DIM:]).astype(o_ref.dtype)


def _chunk_attention(qt, k, vt, g, bias):
    _, b, _, s = qt.shape
    ts = min(TS_CHUNK, s)
    return pl.pallas_call(
        _chunk_kernel,
        grid=(b, W_GROUP // LANES, s // ts),
        in_specs=_t_specs(1, ts, s) + [
            pl.BlockSpec((2, WIN_CHUNK, TQ_CHUNK), lambda b_, p, i: (p, 0, 0))],
        out_specs=pl.BlockSpec((None, ts, LANES), _out_tile),
        out_shape=jax.ShapeDtypeStruct((b, s, W_GROUP), MM_DTYPE),
        scratch_shapes=[pltpu.VMEM((LEFT + s, LANES), MM_DTYPE),
                        pltpu.VMEM((2, FOX_V_ROWS, LEFT + s), MM_DTYPE),
                        pltpu.VMEM((LANES, ts), F32)],
        compiler_params=_ATT_PARAMS,
        name="chunk_attention",
    )(qt, k, vt, g, bias)


N_DIAG = WIN_CHUNK + TQ_CHUNK


def _bias_kernel(v_ref, o_ref):
    x = jnp.broadcast_to(v_ref[...], (WIN_CHUNK, N_DIAG))
    y = pltpu.roll(x, 1, 1, stride=1, stride_axis=0)
    t = y[:, WIN_CHUNK:]
    c = lax.broadcasted_iota(jnp.int32, t.shape, 0)
    r = lax.broadcasted_iota(jnp.int32, t.shape, 1)
    band = c - (r - (r & (CHUNK - 1)))
    o_ref[...] = jnp.where((band >= 0) & (band < LEFT + CHUNK), t, NEG)


def _chunk_bias_table(rel_bias):
    kk = np.arange(N_DIAG - 1)
    rel = kk - (WIN_CHUNK - 1) + LEFT
    vec = rel_bias[:, np.clip(rel, -REL_CLIP, REL_CLIP) + REL_CLIP].astype(F32)
    vec = jnp.pad(vec, ((0, 0), (0, 1)))[:, None, :]
    h = vec.shape[0]
    return pl.pallas_call(
        _bias_kernel,
        grid=(h,),
        in_specs=[pl.BlockSpec((None, 1, N_DIAG), lambda i: (i, 0, 0))],
        out_specs=pl.BlockSpec((None, WIN_CHUNK, TQ_CHUNK), lambda i: (i, 0, 0)),
        out_shape=jax.ShapeDtypeStruct((h, WIN_CHUNK, TQ_CHUNK), F32),
        compiler_params=pltpu.CompilerParams(dimension_semantics=("arbitrary",)),
        name="chunk_bias",
    )(vec)


def _fox_kernel(qt_ref, k_ref, vt_ref, g_ref, c_ref, o_ref, kaug, vaug, m_ref, acc_ref):
    i = pl.program_id(2)
    p = pl.program_id(1)
    t = qt_ref.shape[1]
    aug0 = [HEAD_DIM * (1 - j) for j in range(2)]
    n_ones = vaug.shape[1] - HEAD_DIM

    @pl.when(i == 0)
    def _():
        k = k_ref[...].astype(F32)
        vt = vt_ref[...].astype(F32)
        pieces = jnp.concatenate(_split3(-c_ref[...]), axis=1)
        row = lax.broadcasted_iota(jnp.int32, (3 * LANES, 2 * LANES), 0)
        col = lax.broadcasted_iota(jnp.int32, (3 * LANES, 2 * LANES), 1)
        place = None
        for j in range(2):
            for c in range(3):
                hit = (row == c * LANES + 2 * p + j) & (col == j * LANES + aug0[j] + c)
                place = hit if place is None else place | hit
        aug = _dot(pieces, jnp.where(place, 1.0, 0.0).astype(MM_DTYPE))
        for j in range(2):
            kaug[j] = jnp.where(_head_mask(k.shape, j, 1), k,
                                aug[:, j * LANES:(j + 1) * LANES]).astype(kaug.dtype)
            vaug[j] = jnp.concatenate(
                [vt[j * HEAD_DIM:(j + 1) * HEAD_DIM], jnp.ones((n_ones, vt.shape[1]), F32)],
                axis=0).astype(vaug.dtype)

    q = qt_ref[...].astype(F32)
    row = lax.broadcasted_iota(jnp.int32, q.shape, 0)
    qa = []
    for j in range(2):
        ones = (row >= aug0[j]) & (row < aug0[j] + 3)
        qa.append(jnp.where(_head_mask(q.shape, j, 0), q, jnp.where(ones, 1.0, 0.0)).astype(MM_DTYPE))
    m_ref[...] = jnp.full(m_ref.shape, NEG, F32)
    acc_ref[...] = jnp.zeros_like(acc_ref)

    tk = TK_SB
    n_sub = t // tk

    def unit(j, k0, qc, diag):
        cols = slice(qc * tk, (qc + 1) * tk)
        z = _dot(kaug[j, pl.ds(k0, tk), :], qa[j][:, cols])
        for _ in range(FOX_SKEW):
            yield
        if diag:
            causal = (lax.broadcasted_iota(jnp.int32, z.shape, 0)
                      <= lax.broadcasted_iota(jnp.int32, z.shape, 1))
            z = jnp.where(causal, z, NEG)
        m = m_ref[j, :, cols]
        m_new = jnp.maximum(m, jnp.max(z, axis=0, keepdims=True))
        pr = jnp.exp(z - m_new).astype(MM_DTYPE)
        alpha = jnp.exp(m - m_new)
        m_ref[j, :, cols] = m_new
        yield
        pv = _dot(vaug[j, :, pl.ds(k0, tk)], pr)
        acc_ref[j, :, cols] = alpha * acc_ref[j, :, cols] + pv

    def units(base, n_blocks, diag):
        out = []
        for c in range(n_blocks):
            for qc in range(c if diag else 0, n_sub):
                for j in range(2):
                    out.append(unit(j, pl.multiple_of(base + c * tk, tk), qc, diag and qc == c))
        return out

    trip = min(FOX_KEYS_PER_TRIP, t)

    def body(kj, _):
        _run_skewed(units(kj * trip, trip // tk, False))
        return 0

    if kaug.shape[1] > t:
        lax.fori_loop(0, i * (t // trip), body, 0)
    _run_skewed(units(i * t, n_sub, True))
    outs = [acc_ref[j, :HEAD_DIM] / acc_ref[j, HEAD_DIM:HEAD_DIM + 1] for j in range(2)]
    o_ref[...] = _gated_t(g_ref[...], outs[0], outs[1]).astype(o_ref.dtype)


def _fox_attention(qt, k, vt, g, cum):
    _, b, _, s = qt.shape
    t = min(TQ_ATT, s)
    return pl.pallas_call(
        _fox_kernel,
        grid=(b, W_GROUP // LANES, s // t),
        in_specs=_t_specs(2, t, s) + [
            pl.BlockSpec((None, s, F_PAD), lambda b_, p, i: (b_, 0, 0))],
        out_specs=pl.BlockSpec((None, t, LANES), _out_tile),
        out_shape=jax.ShapeDtypeStruct((b, s, W_GROUP), MM_DTYPE),
        scratch_shapes=[pltpu.VMEM((2, s, LANES), MM_DTYPE),
                        pltpu.VMEM((2, FOX_V_ROWS, s), MM_DTYPE),
                        pltpu.VMEM((2, 1, t), F32), pltpu.VMEM((2, FOX_V_ROWS, t), F32)],
        compiler_params=_ATT_PARAMS,
        name="fox_attention",
    )(qt, k, vt, g, cum)


def _out_kernel(x_ref, a_ref, b_ref, c_ref, w_ref, o_ref):
    acc = _dot(a_ref[...], w_ref[0]) + _dot(b_ref[...], w_ref[1]) + _dot(c_ref[...], w_ref[2])
    o_ref[...] = x_ref[...] + acc


def _out_projection(x, ma, mb, mc, w):
    b, s, d = x.shape
    tm = min(TM_PROJ, s)
    mix = pl.BlockSpec((None, tm, W_GROUP), lambda bi, i: (bi, i, 0))
    return pl.pallas_call(
        _out_kernel,
        grid=(b, s // tm),
        in_specs=[pl.BlockSpec((None, tm, d), lambda bi, i: (bi, i, 0)), mix, mix, mix,
                  pl.BlockSpec((N_GROUPS, W_GROUP, d), lambda bi, i: (0, 0, 0))],
        out_specs=pl.BlockSpec((None, tm, d), lambda bi, i: (bi, i, 0)),
        out_shape=jax.ShapeDtypeStruct(x.shape, x.dtype),
        compiler_params=pltpu.CompilerParams(
            dimension_semantics=("arbitrary", "arbitrary"), vmem_limit_bytes=VMEM_LIMIT),
        name="out_proj",
    )(x, ma, mb, mc, w)


def _constants():
    r = np.arange(MXU_DIM)
    bd = (r[:, None] // HEAD_DIM == r[None, :] // HEAD_DIM).astype(np.float32)
    r = np.arange(TK_SB)
    u_sb = -(r[None, :] >= r[:, None]).astype(np.float32)
    r = np.arange(CUM_BLK)
    low = (r[None, :] <= r[:, None]).astype(np.float32)
    return (jnp.asarray(bd, MM_DTYPE), jnp.asarray(u_sb, MM_DTYPE), jnp.asarray(low, MM_DTYPE))


def _proj_weights(ng, w_in, qn_ch, kn_ch, qn_fox, kn_fox, bd):
    d = w_in.shape[0]
    w_in = w_in.astype(MM_DTYPE)
    w4 = w_in[:, :N_GROUPS * 4 * W_GROUP].reshape(d, N_GROUPS, 4, W_GROUP)
    w_nat = jnp.transpose(w4[:, :, (1, 3), :], (2, 1, 0, 3)).reshape(2 * N_GROUPS, d, W_GROUP)
    w_tr = jnp.transpose(w4[:, :, (0, 2), :], (2, 1, 3, 0)).reshape(2 * N_GROUPS, W_GROUP, d)
    w_f = jnp.pad(w_in[:, N_GROUPS * 4 * W_GROUP:], ((0, 0), (0, F_PAD - H_GROUP)))
    grow = jnp.stack([jnp.tile(kn_ch, H_GROUP), jnp.tile(kn_fox, H_GROUP)])[:, None, :].astype(F32)
    gcol = jnp.stack([qn_ch, qn_fox])[:, :, None].astype(F32)
    return (ng[None, :], w_nat, w_tr, w_f, grow, gcol, bd)


def _mixers(projected, b_forget, rel_bias, u_sb, low):
    qt, k, vt, g, f = projected
    bias_row = jnp.pad(b_forget.astype(F32), (0, F_PAD - H_GROUP))[None, :]
    cum = _cum_forget(f, bias_row, low)
    m_sb = _sb_attention(qt, k, vt, g, u_sb)
    m_ch = _chunk_attention(qt, k, vt, g, _chunk_bias_table(rel_bias))
    m_fx = _fox_attention(qt, k, vt, g, cum)
    return m_sb, m_ch, m_fx


def kernel(x, norm_g, w_in, b_forget, q_norm_ch, k_norm_ch, q_norm_fox, k_norm_fox, rel_bias, w_out):
    bd, u_sb, low = _constants()
    depth, d = norm_g.shape
    weights = [_proj_weights(norm_g[l], w_in[l], q_norm_ch[l], k_norm_ch[l],
                             q_norm_fox[l], k_norm_fox[l], bd) for l in range(depth)]
    w_o = [w_out[l].reshape(N_GROUPS, W_GROUP, d).astype(MM_DTYPE) for l in range(depth)]
    projected = _projection(x, weights[0])
    for l in range(depth):
        mixed = _mixers(projected, b_forget[l], rel_bias[l], u_sb, low)
        if l + 1 < depth:
            x, *projected = _projection(x, weights[l + 1], prev=(*mixed, w_o[l]))
        else:
            x = _out_projection(x, *mixed, w_o[l])
    return x
```

```python
import jax
import jax.numpy as jnp
import numpy as np
from jax import lax
from jax.experimental import pallas as pl
from jax.experimental.pallas import tpu as pltpu

D_MODEL = 1024
HEAD_DIM = 64
H_GROUP = 8
W_GROUP = H_GROUP * HEAD_DIM
N_GROUPS = 3
CHUNK = 64
N_LEFT_CHUNKS = 8
LEFT = N_LEFT_CHUNKS * CHUNK
REL_CLIP = 128
EPS = 1e-6
SCALE = HEAD_DIM ** -0.5

LANES = 128
MXU_DIM = 256
F_PAD = LANES
NEG = -1e30

MM_DTYPE = jnp.bfloat16
F32 = jnp.float32

TM_PROJ = 512
TQ_ATT = 4096
FOX_KEYS_PER_TRIP = 1024
BF16_ROWS = 16
FOX_V_ROWS = HEAD_DIM + BF16_ROWS
FOX_SKEW = 5
SB_SKEW = 2
SB_STATIC_DIAGS = 2
SB_DEAD_LOG = -104.0
TK_SB = MXU_DIM
TQ_CHUNK = MXU_DIM
WIN_CHUNK = LEFT + TQ_CHUNK
TS_CHUNK = 4096
CUM_BLK = 512
VMEM_LIMIT = 56 * 1024 * 1024


def _dot(a, b):
    return jnp.dot(a, b, preferred_element_type=F32)


def _dot_nt(a, b):
    return lax.dot_general(a, b, (((1,), (1,)), ((), ())), preferred_element_type=F32)


def _split2(x):
    hi = x.astype(MM_DTYPE)
    lo = (x - hi.astype(F32)).astype(MM_DTYPE)
    return hi, lo


def _split3(x):
    hi = x.astype(MM_DTYPE)
    mid, lo = _split2(x - hi.astype(F32))
    return hi, mid, lo


def _head_mask(shape, j, axis):
    idx = lax.broadcasted_iota(jnp.int32, shape, axis)
    return (idx >= j * HEAD_DIM) & (idx < (j + 1) * HEAD_DIM)


def _norm_rows(y, gain_row, bd):
    sq = (y * y).astype(MM_DTYPE)
    parts = []
    for c in range(W_GROUP // MXU_DIM):
        sl = slice(c * MXU_DIM, (c + 1) * MXU_DIM)
        parts.append(_dot(sq[:, sl], bd))
    ssq = jnp.concatenate(parts, axis=1)
    return y * lax.rsqrt(ssq * (1.0 / HEAD_DIM) + EPS) * gain_row


def _norm_cols(yt, gain_col):
    y3 = yt.reshape(H_GROUP, HEAD_DIM, yt.shape[1])
    ssq = jnp.sum(y3 * y3, axis=1, keepdims=True)
    y3 = y3 * lax.rsqrt(ssq * (1.0 / HEAD_DIM) + EPS) * gain_col
    return y3.reshape(yt.shape)


def _proj_kernel(x_ref, *refs):
    _project(x_ref[...], *refs)


def _out_proj_kernel(x_ref, a_ref, b_ref, c_ref, wo_ref, *refs):
    *proj_refs, xo_ref = refs
    x = x_ref[...] + (_dot(a_ref[...], wo_ref[0]) + _dot(b_ref[...], wo_ref[1])
                      + _dot(c_ref[...], wo_ref[2]))
    xo_ref[...] = x
    _project(x, *proj_refs)


def _project(x, ng_ref, wn_ref, wt_ref, wf_ref, grow_ref, gcol_ref, bd_ref,
             qt_ref, k_ref, vt_ref, g_ref, f_ref):
    h = x * lax.rsqrt(jnp.mean(x * x, axis=-1, keepdims=True) + EPS) * ng_ref[...]
    hb = h.astype(MM_DTYPE)
    bd = bd_ref[...]
    dt = qt_ref.dtype
    for grp in range(N_GROUPS):
        qt = _dot_nt(wt_ref[grp], hb)
        k = _dot(hb, wn_ref[grp])
        if grp > 0:
            qt = _norm_cols(qt, gcol_ref[grp - 1])
            k = _norm_rows(k, grow_ref[grp - 1], bd)
        qt_ref[grp] = (qt * SCALE).astype(dt)
        k_ref[grp] = k.astype(dt)
        vt_ref[grp] = _dot_nt(wt_ref[N_GROUPS + grp], hb).astype(dt)
        g_ref[grp] = _dot(hb, wn_ref[N_GROUPS + grp])
    f_ref[...] = _dot(hb, wf_ref[...])


def _projection(x, proj_weights, prev=None):
    b, s, d = x.shape
    tm = min(TM_PROJ, s)
    const = dict(pipeline_mode=pl.Buffered(1))
    row_spec = pl.BlockSpec((None, tm, d), lambda bi, i: (bi, i, 0))
    nat_spec = pl.BlockSpec((N_GROUPS, None, tm, W_GROUP), lambda bi, i: (0, bi, i, 0))
    tr_spec = pl.BlockSpec((N_GROUPS, None, W_GROUP, tm), lambda bi, i: (0, bi, 0, i))
    nat_shape = jax.ShapeDtypeStruct((N_GROUPS, b, s, W_GROUP), MM_DTYPE)
    tr_shape = jax.ShapeDtypeStruct((N_GROUPS, b, W_GROUP, s), MM_DTYPE)
    prev_specs, extra_out_specs, extra_out_shapes = [], [], []
    if prev is not None:
        mix = pl.BlockSpec((None, tm, W_GROUP), lambda bi, i: (bi, i, 0))
        prev_specs = [mix, mix, mix,
                      pl.BlockSpec((N_GROUPS, W_GROUP, d), lambda bi, i: (0, 0, 0), **const)]
        extra_out_specs = [row_spec]
        extra_out_shapes = [jax.ShapeDtypeStruct(x.shape, x.dtype)]
    outs = pl.pallas_call(
        _proj_kernel if prev is None else _out_proj_kernel,
        grid=(b, s // tm),
        in_specs=[row_spec] + prev_specs + [
            pl.BlockSpec((1, d), lambda bi, i: (0, 0)),
            pl.BlockSpec((2 * N_GROUPS, d, W_GROUP), lambda bi, i: (0, 0, 0), **const),
            pl.BlockSpec((2 * N_GROUPS, W_GROUP, d), lambda bi, i: (0, 0, 0), **const),
            pl.BlockSpec((d, F_PAD), lambda bi, i: (0, 0), **const),
            pl.BlockSpec((2, 1, W_GROUP), lambda bi, i: (0, 0, 0)),
            pl.BlockSpec((2, HEAD_DIM, 1), lambda bi, i: (0, 0, 0)),
            pl.BlockSpec((MXU_DIM, MXU_DIM), lambda bi, i: (0, 0)),
        ],
        out_specs=[
            tr_spec, nat_spec, tr_spec,
            pl.BlockSpec((N_GROUPS, None, tm, W_GROUP), lambda bi, i: (0, bi, i, 0)),
            pl.BlockSpec((None, tm, F_PAD), lambda bi, i: (bi, i, 0)),
        ] + extra_out_specs,
        out_shape=[
            tr_shape, nat_shape, tr_shape,
            jax.ShapeDtypeStruct((N_GROUPS, b, s, W_GROUP), F32),
            jax.ShapeDtypeStruct((b, s, F_PAD), F32),
        ] + extra_out_shapes,
        compiler_params=pltpu.CompilerParams(
            dimension_semantics=("arbitrary", "arbitrary"), vmem_limit_bytes=VMEM_LIMIT),
        name="proj" if prev is None else "out_proj_proj",
    )(x, *(prev or ()), *proj_weights)
    return outs if prev is None else (outs[-1], *outs[:-1])


def _cum_kernel(f_ref, b_ref, l_ref, o_ref):
    z = f_ref[...] + b_ref[...]
    lf = jnp.minimum(z, 0.0) - jnp.log1p(jnp.exp(-jnp.abs(z)))
    low = l_ref[...]
    carry = jnp.zeros((1, F_PAD), F32)
    for c in range(lf.shape[0] // CUM_BLK):
        rows = slice(c * CUM_BLK, (c + 1) * CUM_BLK)
        hi, mid, lo = _split3(lf[rows])
        cs = (_dot(low, hi) + _dot(low, mid)) + _dot(low, lo) + carry
        o_ref[rows] = cs
        carry = cs[CUM_BLK - 1:CUM_BLK, :]


def _cum_forget(f, bias_row, low):
    b, s, _ = f.shape
    return pl.pallas_call(
        _cum_kernel,
        grid=(b,),
        in_specs=[
            pl.BlockSpec((None, s, F_PAD), lambda bi: (bi, 0, 0)),
            pl.BlockSpec((1, F_PAD), lambda bi: (0, 0)),
            pl.BlockSpec((CUM_BLK, CUM_BLK), lambda bi: (0, 0)),
        ],
        out_specs=pl.BlockSpec((None, s, F_PAD), lambda bi: (bi, 0, 0)),
        out_shape=jax.ShapeDtypeStruct((b, s, F_PAD), F32),
        compiler_params=pltpu.CompilerParams(dimension_semantics=("arbitrary",)),
        name="cum_forget",
    )(f, bias_row, low)


def _gated_t(g, ot0, ot1):
    o = jnp.concatenate([ot0, ot1], axis=0).T
    return o * (g * jax.nn.sigmoid(g))


def _out_tile(b, p, i):
    return (b, i, p)


def _t_specs(grp, tq, s):
    return [
        pl.BlockSpec((None, None, LANES, tq), lambda b, p, i: (grp, b, p, i)),
        pl.BlockSpec((None, None, s, LANES), lambda b, p, i: (grp, b, 0, p)),
        pl.BlockSpec((None, None, LANES, s), lambda b, p, i: (grp, b, p, 0)),
        pl.BlockSpec((None, None, tq, LANES), lambda b, p, i: (grp, b, i, p)),
    ]


_ATT_PARAMS = pltpu.CompilerParams(
    dimension_semantics=("arbitrary", "arbitrary", "arbitrary"), vmem_limit_bytes=VMEM_LIMIT)


def _neg_abs(x):
    bits = lax.bitcast_convert_type(x, jnp.uint32) | jnp.uint32(0x80000000)
    return lax.bitcast_convert_type(bits, F32)


def _run_skewed(units):
    pending = list(units)
    active = []
    while pending or active:
        if pending:
            active.append(pending.pop(0))
        for g in list(active):
            try:
                next(g)
            except StopIteration:
                active.remove(g)


def _masked_heads_t(qt):
    q = qt.astype(F32)
    return [jnp.where(_head_mask(q.shape, j, 0), q, 0.0).astype(MM_DTYPE) for j in range(2)]


def _sb_kernel(qt_ref, k_ref, vt_ref, g_ref, u_ref, o_ref, acc_ref, carry_ref):
    t = qt_ref.shape[1]
    tk = u_ref.shape[0]
    n_sub = t // tk
    qm = _masked_heads_t(qt_ref[...])
    u = u_ref[...]
    acc_ref[...] = jnp.zeros_like(acc_ref)
    carry_ref[...] = jnp.zeros_like(carry_ref)

    def unit(j, kb, qc, diag):
        cols = slice(qc * tk, (qc + 1) * tk)
        static = isinstance(kb, int)
        k0 = kb * tk if static else pl.multiple_of(jnp.maximum(kb, 0) * tk, tk)
        z = _dot(k_ref[pl.ds(k0, tk), :], qm[j][:, cols])
        for _ in range(SB_SKEW):
            yield
        sp = jnp.maximum(z, 0.0) + jnp.log(1.0 + jnp.exp(_neg_abs(z)))
        if diag:
            strict = (lax.broadcasted_iota(jnp.int32, z.shape, 0)
                      < lax.broadcasted_iota(jnp.int32, z.shape, 1))
            sp = jnp.where(strict, sp, 0.0)
        spb = sp.astype(MM_DTYPE)
        yield
        incl = _dot(u, spb)
        for _ in range(SB_SKEW):
            yield
        w = jnp.exp(z + incl)
        if diag:
            w = jnp.where(strict, w, 0.0)
        wb = w.astype(MM_DTYPE)
        yield
        pv = _dot(vt_ref[j * HEAD_DIM:(j + 1) * HEAD_DIM, pl.ds(k0, tk)], wb)
        carry = carry_ref[j, :, cols]
        scale = jnp.exp(carry)
        step = incl[:1]
        if not static:
            scale = jnp.where(kb >= 0, scale, 0.0)
            step = jnp.where(kb >= 0, step, 0.0)
        acc_ref[j, :, cols] += scale * pv
        carry_ref[j, :, cols] = carry + step

    def diagonal(d, first_qc, masked=False):
        return [unit(j, qc - d, qc, masked) for qc in range(first_qc, n_sub) for j in range(2)]

    def live(d):
        lane = lax.broadcasted_iota(jnp.int32, carry_ref.shape, 2)
        return jnp.max(jnp.where(lane >= d * tk, carry_ref[...], NEG)) >= SB_DEAD_LOG

    n_static = min(SB_STATIC_DIAGS, n_sub)
    _run_skewed([un for d in range(n_static) for un in diagonal(d, d, masked=d == 0)])

    def body(state):
        d, _ = state
        _run_skewed(diagonal(d, n_static))
        return d + 1, live(d + 1)

    lax.while_loop(lambda st: (st[0] < n_sub) & st[1], body, (n_static, live(n_static)))
    o_ref[...] = _gated_t(g_ref[...], acc_ref[0], acc_ref[1]).astype(o_ref.dtype)


def _sb_attention(qt, k, vt, g, u):
    _, b, _, s = qt.shape
    t = s
    tk = u.shape[0]
    return pl.pallas_call(
        _sb_kernel,
        grid=(b, W_GROUP // LANES, s // t),
        in_specs=_t_specs(0, t, s) + [pl.BlockSpec((tk, tk), lambda b_, p, i: (0, 0))],
        out_specs=pl.BlockSpec((None, t, LANES), _out_tile),
        out_shape=jax.ShapeDtypeStruct((b, s, W_GROUP), MM_DTYPE),
        scratch_shapes=[pltpu.VMEM((2, HEAD_DIM, t), F32), pltpu.VMEM((2, 1, t), F32)],
        compiler_params=_ATT_PARAMS,
        name="sb_attention",
    )(qt, k, vt, g, u)


def _chunk_kernel(qt_ref, k_ref, vt_ref, g_ref, bias_ref, o_ref, kpad, vaug, ot_ref):
    i = pl.program_id(2)
    ts = qt_ref.shape[1]
    win, tq = bias_ref.shape[1:]
    s = k_ref.shape[0]
    n_ones = vaug.shape[1] - HEAD_DIM

    @pl.when(i == 0)
    def _():
        kpad[:LEFT] = jnp.zeros((LEFT, LANES), kpad.dtype)
        kpad[LEFT:] = k_ref[...]
        vt = vt_ref[...].astype(F32)
        for j in range(2):
            vaug[j, :, :LEFT] = jnp.zeros((vaug.shape[1], LEFT), vaug.dtype)
            vaug[j, :, LEFT:] = jnp.concatenate(
                [vt[j * HEAD_DIM:(j + 1) * HEAD_DIM], jnp.ones((n_ones, s), F32)],
                axis=0).astype(vaug.dtype)

    qm = _masked_heads_t(qt_ref[...])

    def unit(j, r, first_step):
        cols = slice(r * tq, (r + 1) * tq)
        r0 = pl.multiple_of(i * ts + r * tq, tq)
        zs = []
        for kb in range(win // tq):
            k0 = pl.multiple_of(r0 + kb * tq, tq)
            z = _dot(kpad[pl.ds(k0, tq), :], qm[j][:, cols]) + bias_ref[j, kb * tq:(kb + 1) * tq]
            lo_key = LEFT - r * tq - kb * tq
            if first_step and lo_key > 0:
                key = lax.broadcasted_iota(jnp.int32, z.shape, 0)
                z = jnp.where(key >= lo_key, z, NEG)
            zs.append(z)
            yield
        m = zs[0].max(axis=0, keepdims=True)
        for z in zs[1:]:
            m = jnp.maximum(m, z.max(axis=0, keepdims=True))
        ps = []
        for z in zs:
            ps.append(jnp.exp(z - m).astype(MM_DTYPE))
            yield
        pv = None
        for kb, p in enumerate(ps):
            k0 = pl.multiple_of(r0 + kb * tq, tq)
            term = _dot(vaug[j, :, pl.ds(k0, tq)], p)
            pv = term if pv is None else pv + term
        ot_ref[j * HEAD_DIM:(j + 1) * HEAD_DIM, cols] = pv[:HEAD_DIM] / pv[HEAD_DIM:HEAD_DIM + 1]

    def run(first_step):
        _run_skewed([unit(j, r, first_step) for r in range(ts // tq) for j in range(2)])

    if s == ts:
        run(True)
    else:
        pl.when(i == 0)(lambda: run(True))
        pl.when(i > 0)(lambda: run(False))

    o_ref[...] = _gated_t(g_ref[...], ot_ref[:HEAD_DIM], ot_ref[HEAD_DIM:]).astype(o_ref.dtype)


def _chunk_attention(qt, k, vt, g, bias):
    _, b, _, s = qt.shape
    ts = min(TS_CHUNK, s)
    return pl.pallas_call(
        _chunk_kernel,
        grid=(b, W_GROUP // LANES, s // ts),
        in_specs=_t_specs(1, ts, s) + [
            pl.BlockSpec((2, WIN_CHUNK, TQ_CHUNK), lambda b_, p, i: (p, 0, 0))],
        out_specs=pl.BlockSpec((None, ts, LANES), _out_tile),
        out_shape=jax.ShapeDtypeStruct((b, s, W_GROUP), MM_DTYPE),
        scratch_shapes=[pltpu.VMEM((LEFT + s, LANES), MM_DTYPE),
                        pltpu.VMEM((2, FOX_V_ROWS, LEFT + s), MM_DTYPE),
                        pltpu.VMEM((LANES, ts), F32)],
        compiler_params=_ATT_PARAMS,
        name="chunk_attention",
    )(qt, k, vt, g, bias)


N_DIAG = WIN_CHUNK + TQ_CHUNK


def _bias_kernel(v_ref, o_ref):
    x = jnp.broadcast_to(v_ref[...], (WIN_CHUNK, N_DIAG))
    y = pltpu.roll(x, 1, 1, stride=1, stride_axis=0)
    t = y[:, WIN_CHUNK:]
    c = lax.broadcasted_iota(jnp.int32, t.shape, 0)
    r = lax.broadcasted_iota(jnp.int32, t.shape, 1)
    band = c - (r - (r & (CHUNK - 1)))
    o_ref[...] = jnp.where((band >= 0) & (band < LEFT + CHUNK), t, NEG)


def _chunk_bias_table(rel_bias):
    kk = np.arange(N_DIAG - 1)
    rel = kk - (WIN_CHUNK - 1) + LEFT
    vec = rel_bias[:, np.clip(rel, -REL_CLIP, REL_CLIP) + REL_CLIP].astype(F32)
    vec = jnp.pad(vec, ((0, 0), (0, 1)))[:, None, :]
    h = vec.shape[0]
    return pl.pallas_call(
        _bias_kernel,
        grid=(h,),
        in_specs=[pl.BlockSpec((None, 1, N_DIAG), lambda i: (i, 0, 0))],
        out_specs=pl.BlockSpec((None, WIN_CHUNK, TQ_CHUNK), lambda i: (i, 0, 0)),
        out_shape=jax.ShapeDtypeStruct((h, WIN_CHUNK, TQ_CHUNK), F32),
        compiler_params=pltpu.CompilerParams(dimension_semantics=("arbitrary",)),
        name="chunk_bias",
    )(vec)


def _fox_kernel(qt_ref, k_ref, vt_ref, g_ref, c_ref, o_ref, kaug, vaug, m_ref, acc_ref):
    i = pl.program_id(2)
    p = pl.program_id(1)
    t = qt_ref.shape[1]
    aug0 = [HEAD_DIM * (1 - j) for j in range(2)]
    n_ones = vaug.shape[1] - HEAD_DIM

    @pl.when(i == 0)
    def _():
        k = k_ref[...].astype(F32)
        vt = vt_ref[...].astype(F32)
        pieces = jnp.concatenate(_split3(-c_ref[...]), axis=1)
        row = lax.broadcasted_iota(jnp.int32, (3 * LANES, 2 * LANES), 0)
        col = lax.broadcasted_iota(jnp.int32, (3 * LANES, 2 * LANES), 1)
        place = None
        for j in range(2):
            for c in range(3):
                hit = (row == c * LANES + 2 * p + j) & (col == j * LANES + aug0[j] + c)
                place = hit if place is None else place | hit
        aug = _dot(pieces, jnp.where(place, 1.0, 0.0).astype(MM_DTYPE))
        for j in range(2):
            kaug[j] = jnp.where(_head_mask(k.shape, j, 1), k,
                                aug[:, j * LANES:(j + 1) * LANES]).astype(kaug.dtype)
            vaug[j] = jnp.concatenate(
                [vt[j * HEAD_DIM:(j + 1) * HEAD_DIM], jnp.ones((n_ones, vt.shape[1]), F32)],
                axis=0).astype(vaug.dtype)

    q = qt_ref[...].astype(F32)
    row = lax.broadcasted_iota(jnp.int32, q.shape, 0)
    qa = []
    for j in range(2):
        ones = (row >= aug0[j]) & (row < aug0[j] + 3)
        qa.append(jnp.where(_head_mask(q.shape, j, 0), q, jnp.where(ones, 1.0, 0.0)).astype(MM_DTYPE))
    m_ref[...] = jnp.full(m_ref.shape, NEG, F32)
    acc_ref[...] = jnp.zeros_like(acc_ref)

    tk = TK_SB
    n_sub = t // tk

    def unit(j, k0, qc, diag):
        cols = slice(qc * tk, (qc + 1) * tk)
        z = _dot(kaug[j, pl.ds(k0, tk), :], qa[j][:, cols])
        for _ in range(FOX_SKEW):
            yield
        if diag:
            causal = (lax.broadcasted_iota(jnp.int32, z.shape, 0)
                      <= lax.broadcasted_iota(jnp.int32, z.shape, 1))
            z = jnp.where(causal, z, NEG)
        m = m_ref[j, :, cols]
        m_new = jnp.maximum(m, jnp.max(z, axis=0, keepdims=True))
        pr = jnp.exp(z - m_new).astype(MM_DTYPE)
        alpha = jnp.exp(m - m_new)
        m_ref[j, :, cols] = m_new
        yield
        pv = _dot(vaug[j, :, pl.ds(k0, tk)], pr)
        acc_ref[j, :, cols] = alpha * acc_ref[j, :, cols] + pv

    def units(base, n_blocks, diag):
        out = []
        for c in range(n_blocks):
            for qc in range(c if diag else 0, n_sub):
                for j in range(2):
                    out.append(unit(j, pl.multiple_of(base + c * tk, tk), qc, diag and qc == c))
        return out

    trip = min(FOX_KEYS_PER_TRIP, t)

    def body(kj, _):
        _run_skewed(units(kj * trip, trip // tk, False))
        return 0

    if kaug.shape[1] > t:
        lax.fori_loop(0, i * (t // trip), body, 0)
    _run_skewed(units(i * t, n_sub, True))
    outs = [acc_ref[j, :HEAD_DIM] / acc_ref[j, HEAD_DIM:HEAD_DIM + 1] for j in range(2)]
    o_ref[...] = _gated_t(g_ref[...], outs[0], outs[1]).astype(o_ref.dtype)


def _fox_attention(qt, k, vt, g, cum):
    _, b, _, s = qt.shape
    t = min(TQ_ATT, s)
    return pl.pallas_call(
        _fox_kernel,
        grid=(b, W_GROUP // LANES, s // t),
        in_specs=_t_specs(2, t, s) + [
            pl.BlockSpec((None, s, F_PAD), lambda b_, p, i: (b_, 0, 0))],
        out_specs=pl.BlockSpec((None, t, LANES), _out_tile),
        out_shape=jax.ShapeDtypeStruct((b, s, W_GROUP), MM_DTYPE),
        scratch_shapes=[pltpu.VMEM((2, s, LANES), MM_DTYPE),
                        pltpu.VMEM((2, FOX_V_ROWS, s), MM_DTYPE),
                        pltpu.VMEM((2, 1, t), F32), pltpu.VMEM((2, FOX_V_ROWS, t), F32)],
        compiler_params=_ATT_PARAMS,
        name="fox_attention",
    )(qt, k, vt, g, cum)


def _out_kernel(x_ref, a_ref, b_ref, c_ref, w_ref, o_ref):
    acc = _dot(a_ref[...], w_ref[0]) + _dot(b_ref[...], w_ref[1]) + _dot(c_ref[...], w_ref[2])
    o_ref[...] = x_ref[...] + acc


def _out_projection(x, ma, mb, mc, w):
    b, s, d = x.shape
    tm = min(TM_PROJ, s)
    mix = pl.BlockSpec((None, tm, W_GROUP), lambda bi, i: (bi, i, 0))
    return pl.pallas_call(
        _out_kernel,
        grid=(b, s // tm),
        in_specs=[pl.BlockSpec((None, tm, d), lambda bi, i: (bi, i, 0)), mix, mix, mix,
                  pl.BlockSpec((N_GROUPS, W_GROUP, d), lambda bi, i: (0, 0, 0))],
        out_specs=pl.BlockSpec((None, tm, d), lambda bi, i: (bi, i, 0)),
        out_shape=jax.ShapeDtypeStruct(x.shape, x.dtype),
        compiler_params=pltpu.CompilerParams(
            dimension_semantics=("arbitrary", "arbitrary"), vmem_limit_bytes=VMEM_LIMIT),
        name="out_proj",
    )(x, ma, mb, mc, w)


def _constants():
    r = np.arange(MXU_DIM)
    bd = (r[:, None] // HEAD_DIM == r[None, :] // HEAD_DIM).astype(np.float32)
    r = np.arange(TK_SB)
    u_sb = -(r[None, :] >= r[:, None]).astype(np.float32)
    r = np.arange(CUM_BLK)
    low = (r[None, :] <= r[:, None]).astype(np.float32)
    return (jnp.asarray(bd, MM_DTYPE), jnp.asarray(u_sb, MM_DTYPE), jnp.asarray(low, MM_DTYPE))


def _proj_weights(ng, w_in, qn_ch, kn_ch, qn_fox, kn_fox, bd):
    d = w_in.shape[0]
    w_in = w_in.astype(MM_DTYPE)
    w4 = w_in[:, :N_GROUPS * 4 * W_GROUP].reshape(d, N_GROUPS, 4, W_GROUP)
    w_nat = jnp.transpose(w4[:, :, (1, 3), :], (2, 1, 0, 3)).reshape(2 * N_GROUPS, d, W_GROUP)
    w_tr = jnp.transpose(w4[:, :, (0, 2), :], (2, 1, 3, 0)).reshape(2 * N_GROUPS, W_GROUP, d)
    w_f = jnp.pad(w_in[:, N_GROUPS * 4 * W_GROUP:], ((0, 0), (0, F_PAD - H_GROUP)))
    grow = jnp.stack([jnp.tile(kn_ch, H_GROUP), jnp.tile(kn_fox, H_GROUP)])[:, None, :].astype(F32)
    gcol = jnp.stack([qn_ch, qn_fox])[:, :, None].astype(F32)
    return (ng[None, :], w_nat, w_tr, w_f, grow, gcol, bd)


def _mixers(projected, b_forget, rel_bias, u_sb, low):
    qt, k, vt, g, f = projected
    bias_row = jnp.pad(b_forget.astype(F32), (0, F_PAD - H_GROUP))[None, :]
    cum = _cum_forget(f, bias_row, low)
    m_sb = _sb_attention(qt, k, vt, g, u_sb)
    m_ch = _chunk_attention(qt, k, vt, g, _chunk_bias_table(rel_bias))
    m_fx = _fox_attention(qt, k, vt, g, cum)
    return m_sb, m_ch, m_fx


def kernel(x, norm_g, w_in, b_forget, q_norm_ch, k_norm_ch, q_norm_fox, k_norm_fox, rel_bias, w_out):
    bd, u_sb, low = _constants()
    depth, d = norm_g.shape
    weights = [_proj_weights(norm_g[l], w_in[l], q_norm_ch[l], k_norm_ch[l],
                             q_norm_fox[l], k_norm_fox[l], bd) for l in range(depth)]
    w_o = [w_out[l].reshape(N_GROUPS, W_GROUP, d).astype(MM_DTYPE) for l in range(depth)]
    projected = _projection(x, weights[0])
    for l in range(depth):
        mixed = _mixers(projected, b_forget[l], rel_bias[l], u_sb, low)
        if l + 1 < depth:
            x, *projected = _projection(x, weights[l + 1], prev=(*mixed, w_o[l]))
        else:
            x = _out_projection(x, *mixed, w_o[l])
    return x
```

```python
import jax
import jax.numpy as jnp
import numpy as np
from jax import lax
from jax.experimental import pallas as pl
from jax.experimental.pallas import tpu as pltpu

D_MODEL = 1024
HEAD_DIM = 64
H_GROUP = 8
W_GROUP = H_GROUP * HEAD_DIM
N_GROUPS = 3
CHUNK = 64
N_LEFT_CHUNKS = 8
LEFT = N_LEFT_CHUNKS * CHUNK
REL_CLIP = 128
EPS = 1e-6
SCALE = HEAD_DIM ** -0.5

LANES = 128
MXU_DIM = 256
F_PAD = LANES
NEG = -1e30

MM_DTYPE = jnp.bfloat16
F32 = jnp.float32

TM_PROJ = 512
TQ_ATT = 4096
FOX_KEYS_PER_TRIP = 1024
BF16_ROWS = 16
FOX_V_ROWS = HEAD_DIM + BF16_ROWS
FOX_SKEW = 5
SB_SKEW = 2
SB_STATIC_DIAGS = 2
SB_DEAD_LOG = -104.0
TK_SB = MXU_DIM
TQ_CHUNK = MXU_DIM
WIN_CHUNK = LEFT + TQ_CHUNK
TS_CHUNK = 4096
CUM_BLK = 512
VMEM_LIMIT = 56 * 1024 * 1024


def _dot(a, b):
    return jnp.dot(a, b, preferred_element_type=F32)


def _dot_nt(a, b):
    return lax.dot_general(a, b, (((1,), (1,)), ((), ())), preferred_element_type=F32)


def _split2(x):
    hi = x.astype(MM_DTYPE)
    lo = (x - hi.astype(F32)).astype(MM_DTYPE)
    return hi, lo


def _split3(x):
    hi = x.astype(MM_DTYPE)
    mid, lo = _split2(x - hi.astype(F32))
    return hi, mid, lo


def _head_mask(shape, j, axis):
    idx = lax.broadcasted_iota(jnp.int32, shape, axis)
    return (idx >= j * HEAD_DIM) & (idx < (j + 1) * HEAD_DIM)


def _norm_rows(y, gain_row, bd):
    sq = (y * y).astype(MM_DTYPE)
    parts = []
    for c in range(W_GROUP // MXU_DIM):
        sl = slice(c * MXU_DIM, (c + 1) * MXU_DIM)
        parts.append(_dot(sq[:, sl], bd))
    ssq = jnp.concatenate(parts, axis=1)
    return y * lax.rsqrt(ssq * (1.0 / HEAD_DIM) + EPS) * gain_row


def _norm_cols(yt, gain_col):
    y3 = yt.reshape(H_GROUP, HEAD_DIM, yt.shape[1])
    ssq = jnp.sum(y3 * y3, axis=1, keepdims=True)
    y3 = y3 * lax.rsqrt(ssq * (1.0 / HEAD_DIM) + EPS) * gain_col
    return y3.reshape(yt.shape)


def _proj_kernel(x_ref, *refs):
    _project(x_ref[...], *refs)


def _out_proj_kernel(x_ref, a_ref, b_ref, c_ref, wo_ref, *refs):
    *proj_refs, xo_ref = refs
    tm = x_ref.shape[0]
    parts = []
    for rows in (slice(0, tm // 2), slice(tm // 2, tm)):
        x = x_ref[rows] + (_dot(a_ref[rows], wo_ref[0]) + _dot(b_ref[rows], wo_ref[1])
                           + _dot(c_ref[rows], wo_ref[2]))
        xo_ref[rows] = x
        parts.append(x)
    _project(jnp.concatenate(parts, axis=0), *proj_refs)


def _project(x, ng_ref, wn_ref, wt_ref, wf_ref, grow_ref, gcol_ref, bd_ref,
             qt_ref, k_ref, vt_ref, g_ref, f_ref):
    h = x * lax.rsqrt(jnp.mean(x * x, axis=-1, keepdims=True) + EPS) * ng_ref[...]
    hb = h.astype(MM_DTYPE)
    bd = bd_ref[...]
    dt = qt_ref.dtype
    for grp in range(N_GROUPS):
        qt = _dot_nt(wt_ref[grp], hb)
        k = _dot(hb, wn_ref[grp])
        if grp > 0:
            qt = _norm_cols(qt, gcol_ref[grp - 1])
            k = _norm_rows(k, grow_ref[grp - 1], bd)
        qt_ref[grp] = (qt * SCALE).astype(dt)
        k_ref[grp] = k.astype(dt)
        vt_ref[grp] = _dot_nt(wt_ref[N_GROUPS + grp], hb).astype(dt)
        g_ref[grp] = _dot(hb, wn_ref[N_GROUPS + grp])
    f_ref[...] = _dot(hb, wf_ref[...])


def _projection(x, proj_weights, prev=None):
    b, s, d = x.shape
    tm = min(TM_PROJ, s)
    const = dict(pipeline_mode=pl.Buffered(1))
    row_spec = pl.BlockSpec((None, tm, d), lambda bi, i: (bi, i, 0))
    nat_spec = pl.BlockSpec((N_GROUPS, None, tm, W_GROUP), lambda bi, i: (0, bi, i, 0))
    tr_spec = pl.BlockSpec((N_GROUPS, None, W_GROUP, tm), lambda bi, i: (0, bi, 0, i))
    nat_shape = jax.ShapeDtypeStruct((N_GROUPS, b, s, W_GROUP), MM_DTYPE)
    tr_shape = jax.ShapeDtypeStruct((N_GROUPS, b, W_GROUP, s), MM_DTYPE)
    prev_specs, extra_out_specs, extra_out_shapes = [], [], []
    if prev is not None:
        mix = pl.BlockSpec((None, tm, W_GROUP), lambda bi, i: (bi, i, 0))
        prev_specs = [mix, mix, mix,
                      pl.BlockSpec((N_GROUPS, W_GROUP, d), lambda bi, i: (0, 0, 0), **const)]
        extra_out_specs = [row_spec]
        extra_out_shapes = [jax.ShapeDtypeStruct(x.shape, x.dtype)]
    outs = pl.pallas_call(
        _proj_kernel if prev is None else _out_proj_kernel,
        grid=(b, s // tm),
        in_specs=[row_spec] + prev_specs + [
            pl.BlockSpec((1, d), lambda bi, i: (0, 0)),
            pl.BlockSpec((2 * N_GROUPS, d, W_GROUP), lambda bi, i: (0, 0, 0), **const),
            pl.BlockSpec((2 * N_GROUPS, W_GROUP, d), lambda bi, i: (0, 0, 0), **const),
            pl.BlockSpec((d, F_PAD), lambda bi, i: (0, 0), **const),
            pl.BlockSpec((2, 1, W_GROUP), lambda bi, i: (0, 0, 0)),
            pl.BlockSpec((2, HEAD_DIM, 1), lambda bi, i: (0, 0, 0)),
            pl.BlockSpec((MXU_DIM, MXU_DIM), lambda bi, i: (0, 0)),
        ],
        out_specs=[
            tr_spec, nat_spec, tr_spec,
            pl.BlockSpec((N_GROUPS, None, tm, W_GROUP), lambda bi, i: (0, bi, i, 0)),
            pl.BlockSpec((None, tm, F_PAD), lambda bi, i: (bi, i, 0)),
        ] + extra_out_specs,
        out_shape=[
            tr_shape, nat_shape, tr_shape,
            jax.ShapeDtypeStruct((N_GROUPS, b, s, W_GROUP), F32),
            jax.ShapeDtypeStruct((b, s, F_PAD), F32),
        ] + extra_out_shapes,
        compiler_params=pltpu.CompilerParams(
            dimension_semantics=("arbitrary", "arbitrary"), vmem_limit_bytes=VMEM_LIMIT),
        name="proj" if prev is None else "out_proj_proj",
    )(x, *(prev or ()), *proj_weights)
    return outs if prev is None else (outs[-1], *outs[:-1])


def _cum_kernel(f_ref, b_ref, l_ref, o_ref):
    z = f_ref[...] + b_ref[...]
    lf = jnp.minimum(z, 0.0) - jnp.log1p(jnp.exp(-jnp.abs(z)))
    low = l_ref[...]
    carry = jnp.zeros((1, F_PAD), F32)
    for c in range(lf.shape[0] // CUM_BLK):
        rows = slice(c * CUM_BLK, (c + 1) * CUM_BLK)
        hi, mid, lo = _split3(lf[rows])
        cs = (_dot(low, hi) + _dot(low, mid)) + _dot(low, lo) + carry
        o_ref[rows] = cs
        carry = cs[CUM_BLK - 1:CUM_BLK, :]


def _cum_forget(f, bias_row, low):
    b, s, _ = f.shape
    return pl.pallas_call(
        _cum_kernel,
        grid=(b,),
        in_specs=[
            pl.BlockSpec((None, s, F_PAD), lambda bi: (bi, 0, 0)),
            pl.BlockSpec((1, F_PAD), lambda bi: (0, 0)),
            pl.BlockSpec((CUM_BLK, CUM_BLK), lambda bi: (0, 0)),
        ],
        out_specs=pl.BlockSpec((None, s, F_PAD), lambda bi: (bi, 0, 0)),
        out_shape=jax.ShapeDtypeStruct((b, s, F_PAD), F32),
        compiler_params=pltpu.CompilerParams(dimension_semantics=("arbitrary",)),
        name="cum_forget",
    )(f, bias_row, low)


def _gated_t(g, ot0, ot1):
    o = jnp.concatenate([ot0, ot1], axis=0).T
    return o * (g * jax.nn.sigmoid(g))


def _out_tile(b, p, i):
    return (b, i, p)


def _t_specs(grp, tq, s):
    return [
        pl.BlockSpec((None, None, LANES, tq), lambda b, p, i: (grp, b, p, i)),
        pl.BlockSpec((None, None, s, LANES), lambda b, p, i: (grp, b, 0, p)),
        pl.BlockSpec((None, None, LANES, s), lambda b, p, i: (grp, b, p, 0)),
        pl.BlockSpec((None, None, tq, LANES), lambda b, p, i: (grp, b, i, p)),
    ]


_ATT_PARAMS = pltpu.CompilerParams(
    dimension_semantics=("arbitrary", "arbitrary", "arbitrary"), vmem_limit_bytes=VMEM_LIMIT)


def _neg_abs(x):
    bits = lax.bitcast_convert_type(x, jnp.uint32) | jnp.uint32(0x80000000)
    return lax.bitcast_convert_type(bits, F32)


def _run_skewed(units):
    pending = list(units)
    active = []
    while pending or active:
        if pending:
            active.append(pending.pop(0))
        for g in list(active):
            try:
                next(g)
            except StopIteration:
                active.remove(g)


def _masked_heads_t(qt):
    q = qt.astype(F32)
    return [jnp.where(_head_mask(q.shape, j, 0), q, 0.0).astype(MM_DTYPE) for j in range(2)]


def _sb_kernel(qt_ref, k_ref, vt_ref, g_ref, u_ref, o_ref, acc_ref, carry_ref):
    t = qt_ref.shape[1]
    tk = u_ref.shape[0]
    n_sub = t // tk
    qm = _masked_heads_t(qt_ref[...])
    u = u_ref[...]
    acc_ref[...] = jnp.zeros_like(acc_ref)
    carry_ref[...] = jnp.zeros_like(carry_ref)

    def unit(j, kb, qc, diag):
        cols = slice(qc * tk, (qc + 1) * tk)
        static = isinstance(kb, int)
        k0 = kb * tk if static else pl.multiple_of(jnp.maximum(kb, 0) * tk, tk)
        z = _dot(k_ref[pl.ds(k0, tk), :], qm[j][:, cols])
        for _ in range(SB_SKEW):
            yield
        sp = jnp.maximum(z, 0.0) + jnp.log(1.0 + jnp.exp(_neg_abs(z)))
        if diag:
            strict = (lax.broadcasted_iota(jnp.int32, z.shape, 0)
                      < lax.broadcasted_iota(jnp.int32, z.shape, 1))
            sp = jnp.where(strict, sp, 0.0)
        spb = sp.astype(MM_DTYPE)
        yield
        incl = _dot(u, spb)
        for _ in range(SB_SKEW):
            yield
        w = jnp.exp(z + incl)
        if diag:
            w = jnp.where(strict, w, 0.0)
        wb = w.astype(MM_DTYPE)
        yield
        pv = _dot(vt_ref[j * HEAD_DIM:(j + 1) * HEAD_DIM, pl.ds(k0, tk)], wb)
        carry = carry_ref[j, :, cols]
        scale = jnp.exp(carry)
        step = incl[:1]
        if not static:
            scale = jnp.where(kb >= 0, scale, 0.0)
            step = jnp.where(kb >= 0, step, 0.0)
        acc_ref[j, :, cols] += scale * pv
        carry_ref[j, :, cols] = carry + step

    def diagonal(d, first_qc, masked=False):
        return [unit(j, qc - d, qc, masked) for qc in range(first_qc, n_sub) for j in range(2)]

    def live(d):
        lane = lax.broadcasted_iota(jnp.int32, carry_ref.shape, 2)
        return jnp.max(jnp.where(lane >= d * tk, carry_ref[...], NEG)) >= SB_DEAD_LOG

    n_static = min(SB_STATIC_DIAGS, n_sub)
    _run_skewed([un for d in range(n_static) for un in diagonal(d, d, masked=d == 0)])

    def body(state):
        d, _ = state
        _run_skewed(diagonal(d, n_static))
        return d + 1, live(d + 1)

    lax.while_loop(lambda st: (st[0] < n_sub) & st[1], body, (n_static, live(n_static)))
    o_ref[...] = _gated_t(g_ref[...], acc_ref[0], acc_ref[1]).astype(o_ref.dtype)


def _sb_attention(qt, k, vt, g, u):
    _, b, _, s = qt.shape
    t = s
    tk = u.shape[0]
    return pl.pallas_call(
        _sb_kernel,
        grid=(b, W_GROUP // LANES, s // t),
        in_specs=_t_specs(0, t, s) + [pl.BlockSpec((tk, tk), lambda b_, p, i: (0, 0))],
        out_specs=pl.BlockSpec((None, t, LANES), _out_tile),
        out_shape=jax.ShapeDtypeStruct((b, s, W_GROUP), MM_DTYPE),
        scratch_shapes=[pltpu.VMEM((2, HEAD_DIM, t), F32), pltpu.VMEM((2, 1, t), F32)],
        compiler_params=_ATT_PARAMS,
        name="sb_attention",
    )(qt, k, vt, g, u)


def _chunk_kernel(qt_ref, k_ref, vt_ref, g_ref, bias_ref, o_ref, kpad, vaug, ot_ref):
    i = pl.program_id(2)
    ts = qt_ref.shape[1]
    win, tq = bias_ref.shape[1:]
    s = k_ref.shape[0]
    n_ones = vaug.shape[1] - HEAD_DIM

    @pl.when(i == 0)
    def _():
        kpad[:LEFT] = jnp.zeros((LEFT, LANES), kpad.dtype)
        kpad[LEFT:] = k_ref[...]
        vt = vt_ref[...].astype(F32)
        for j in range(2):
            vaug[j, :, :LEFT] = jnp.zeros((vaug.shape[1], LEFT), vaug.dtype)
            vaug[j, :, LEFT:] = jnp.concatenate(
                [vt[j * HEAD_DIM:(j + 1) * HEAD_DIM], jnp.ones((n_ones, s), F32)],
                axis=0).astype(vaug.dtype)

    qm = _masked_heads_t(qt_ref[...])

    def unit(j, r, first_step):
        cols = slice(r * tq, (r + 1) * tq)
        r0 = pl.multiple_of(i * ts + r * tq, tq)
        zs = []
        for kb in range(win // tq):
            k0 = pl.multiple_of(r0 + kb * tq, tq)
            z = _dot(kpad[pl.ds(k0, tq), :], qm[j][:, cols]) + bias_ref[j, kb * tq:(kb + 1) * tq]
            lo_key = LEFT - r * tq - kb * tq
            if first_step and lo_key > 0:
                key = lax.broadcasted_iota(jnp.int32, z.shape, 0)
                z = jnp.where(key >= lo_key, z, NEG)
            zs.append(z)
            yield
        m = zs[0].max(axis=0, keepdims=True)
        for z in zs[1:]:
            m = jnp.maximum(m, z.max(axis=0, keepdims=True))
        ps = []
        for z in zs:
            ps.append(jnp.exp(z - m).astype(MM_DTYPE))
            yield
        pv = None
        for kb, p in enumerate(ps):
            k0 = pl.multiple_of(r0 + kb * tq, tq)
            term = _dot(vaug[j, :, pl.ds(k0, tq)], p)
            pv = term if pv is None else pv + term
        ot_ref[j * HEAD_DIM:(j + 1) * HEAD_DIM, cols] = pv[:HEAD_DIM] / pv[HEAD_DIM:HEAD_DIM + 1]

    def run(first_step):
        _run_skewed([unit(j, r, first_step) for r in range(ts // tq) for j in range(2)])

    if s == ts:
        run(True)
    else:
        pl.when(i == 0)(lambda: run(True))
        pl.when(i > 0)(lambda: run(False))

    o_ref[...] = _gated_t(g_ref[...], ot_ref[:HEAD_DIM], ot_ref[HEAD_DIM:]).astype(o_ref.dtype)


def _chunk_attention(qt, k, vt, g, bias):
    _, b, _, s = qt.shape
    ts = min(TS_CHUNK, s)
    return pl.pallas_call(
        _chunk_kernel,
        grid=(b, W_GROUP // LANES, s // ts),
        in_specs=_t_specs(1, ts, s) + [
            pl.BlockSpec((2, WIN_CHUNK, TQ_CHUNK), lambda b_, p, i: (p, 0, 0))],
        out_specs=pl.BlockSpec((None, ts, LANES), _out_tile),
        out_shape=jax.ShapeDtypeStruct((b, s, W_GROUP), MM_DTYPE),
        scratch_shapes=[pltpu.VMEM((LEFT + s, LANES), MM_DTYPE),
                        pltpu.VMEM((2, FOX_V_ROWS, LEFT + s), MM_DTYPE),
                        pltpu.VMEM((LANES, ts), F32)],
        compiler_params=_ATT_PARAMS,
        name="chunk_attention",
    )(qt, k, vt, g, bias)


N_DIAG = WIN_CHUNK + TQ_CHUNK


def _bias_kernel(v_ref, o_ref):
    x = jnp.broadcast_to(v_ref[...], (WIN_CHUNK, N_DIAG))
    y = pltpu.roll(x, 1, 1, stride=1, stride_axis=0)
    t = y[:, WIN_CHUNK:]
    c = lax.broadcasted_iota(jnp.int32, t.shape, 0)
    r = lax.broadcasted_iota(jnp.int32, t.shape, 1)
    band = c - (r - (r & (CHUNK - 1)))
    o_ref[...] = jnp.where((band >= 0) & (band < LEFT + CHUNK), t, NEG)


def _chunk_bias_table(rel_bias):
    kk = np.arange(N_DIAG - 1)
    rel = kk - (WIN_CHUNK - 1) + LEFT
    vec = rel_bias[:, np.clip(rel, -REL_CLIP, REL_CLIP) + REL_CLIP].astype(F32)
    vec = jnp.pad(vec, ((0, 0), (0, 1)))[:, None, :]
    h = vec.shape[0]
    return pl.pallas_call(
        _bias_kernel,
        grid=(h,),
        in_specs=[pl.BlockSpec((None, 1, N_DIAG), lambda i: (i, 0, 0))],
        out_specs=pl.BlockSpec((None, WIN_CHUNK, TQ_CHUNK), lambda i: (i, 0, 0)),
        out_shape=jax.ShapeDtypeStruct((h, WIN_CHUNK, TQ_CHUNK), F32),
        compiler_params=pltpu.CompilerParams(dimension_semantics=("arbitrary",)),
        name="chunk_bias",
    )(vec)


def _fox_kernel(qt_ref, k_ref, vt_ref, g_ref, c_ref, o_ref, kaug, vaug, m_ref, acc_ref):
    i = pl.program_id(2)
    p = pl.program_id(1)
    t = qt_ref.shape[1]
    aug0 = [HEAD_DIM * (1 - j) for j in range(2)]
    n_ones = vaug.shape[1] - HEAD_DIM

    @pl.when(i == 0)
    def _():
        k = k_ref[...].astype(F32)
        vt = vt_ref[...].astype(F32)
        pieces = jnp.concatenate(_split3(-c_ref[...]), axis=1)
        row = lax.broadcasted_iota(jnp.int32, (3 * LANES, 2 * LANES), 0)
        col = lax.broadcasted_iota(jnp.int32, (3 * LANES, 2 * LANES), 1)
        place = None
        for j in range(2):
            for c in range(3):
                hit = (row == c * LANES + 2 * p + j) & (col == j * LANES + aug0[j] + c)
                place = hit if place is None else place | hit
        aug = _dot(pieces, jnp.where(place, 1.0, 0.0).astype(MM_DTYPE))
        for j in range(2):
            kaug[j] = jnp.where(_head_mask(k.shape, j, 1), k,
                                aug[:, j * LANES:(j + 1) * LANES]).astype(kaug.dtype)
            vaug[j] = jnp.concatenate(
                [vt[j * HEAD_DIM:(j + 1) * HEAD_DIM], jnp.ones((n_ones, vt.shape[1]), F32)],
                axis=0).astype(vaug.dtype)

    q = qt_ref[...].astype(F32)
    row = lax.broadcasted_iota(jnp.int32, q.shape, 0)
    qa = []
    for j in range(2):
        ones = (row >= aug0[j]) & (row < aug0[j] + 3)
        qa.append(jnp.where(_head_mask(q.shape, j, 0), q, jnp.where(ones, 1.0, 0.0)).astype(MM_DTYPE))
    m_ref[...] = jnp.full(m_ref.shape, NEG, F32)
    acc_ref[...] = jnp.zeros_like(acc_ref)

    tk = TK_SB
    n_sub = t // tk

    def unit(j, k0, qc, diag):
        cols = slice(qc * tk, (qc + 1) * tk)
        z = _dot(kaug[j, pl.ds(k0, tk), :], qa[j][:, cols])
        for _ in range(FOX_SKEW):
            yield
        if diag:
            causal = (lax.broadcasted_iota(jnp.int32, z.shape, 0)
                      <= lax.broadcasted_iota(jnp.int32, z.shape, 1))
            z = jnp.where(causal, z, NEG)
        m = m_ref[j, :, cols]
        m_new = jnp.maximum(m, jnp.max(z, axis=0, keepdims=True))
        pr = jnp.exp(z - m_new).astype(MM_DTYPE)
        alpha = jnp.exp(m - m_new)
        m_ref[j, :, cols] = m_new
        yield
        pv = _dot(vaug[j, :, pl.ds(k0, tk)], pr)
        acc_ref[j, :, cols] = alpha * acc_ref[j, :, cols] + pv

    def units(base, n_blocks, diag):
        out = []
        for c in range(n_blocks):
            for qc in range(c if diag else 0, n_sub):
                for j in range(2):
                    out.append(unit(j, pl.multiple_of(base + c * tk, tk), qc, diag and qc == c))
        return out

    trip = min(FOX_KEYS_PER_TRIP, t)

    def body(kj, _):
        _run_skewed(units(kj * trip, trip // tk, False))
        return 0

    if kaug.shape[1] > t:
        lax.fori_loop(0, i * (t // trip), body, 0)
    _run_skewed(units(i * t, n_sub, True))
    outs = [acc_ref[j, :HEAD_DIM] / acc_ref[j, HEAD_DIM:HEAD_DIM + 1] for j in range(2)]
    o_ref[...] = _gated_t(g_ref[...], outs[0], outs[1]).astype(o_ref.dtype)


def _fox_attention(qt, k, vt, g, cum):
    _, b, _, s = qt.shape
    t = min(TQ_ATT, s)
    return pl.pallas_call(
        _fox_kernel,
        grid=(b, W_GROUP // LANES, s // t),
        in_specs=_t_specs(2, t, s) + [
            pl.BlockSpec((None, s, F_PAD), lambda b_, p, i: (b_, 0, 0))],
        out_specs=pl.BlockSpec((None, t, LANES), _out_tile),
        out_shape=jax.ShapeDtypeStruct((b, s, W_GROUP), MM_DTYPE),
        scratch_shapes=[pltpu.VMEM((2, s, LANES), MM_DTYPE),
                        pltpu.VMEM((2, FOX_V_ROWS, s), MM_DTYPE),
                        pltpu.VMEM((2, 1, t), F32), pltpu.VMEM((2, FOX_V_ROWS, t), F32)],
        compiler_params=_ATT_PARAMS,
        name="fox_attention",
    )(qt, k, vt, g, cum)


def _out_kernel(x_ref, a_ref, b_ref, c_ref, w_ref, o_ref):
    acc = _dot(a_ref[...], w_ref[0]) + _dot(b_ref[...], w_ref[1]) + _dot(c_ref[...], w_ref[2])
    o_ref[...] = x_ref[...] + acc


def _out_projection(x, ma, mb, mc, w):
    b, s, d = x.shape
    tm = min(TM_PROJ, s)
    mix = pl.BlockSpec((None, tm, W_GROUP), lambda bi, i: (bi, i, 0))
    return pl.pallas_call(
        _out_kernel,
        grid=(b, s // tm),
        in_specs=[pl.BlockSpec((None, tm, d), lambda bi, i: (bi, i, 0)), mix, mix, mix,
                  pl.BlockSpec((N_GROUPS, W_GROUP, d), lambda bi, i: (0, 0, 0))],
        out_specs=pl.BlockSpec((None, tm, d), lambda bi, i: (bi, i, 0)),
        out_shape=jax.ShapeDtypeStruct(x.shape, x.dtype),
        compiler_params=pltpu.CompilerParams(
            dimension_semantics=("arbitrary", "arbitrary"), vmem_limit_bytes=VMEM_LIMIT),
        name="out_proj",
    )(x, ma, mb, mc, w)


def _constants():
    r = np.arange(MXU_DIM)
    bd = (r[:, None] // HEAD_DIM == r[None, :] // HEAD_DIM).astype(np.float32)
    r = np.arange(TK_SB)
    u_sb = -(r[None, :] >= r[:, None]).astype(np.float32)
    r = np.arange(CUM_BLK)
    low = (r[None, :] <= r[:, None]).astype(np.float32)
    return (jnp.asarray(bd, MM_DTYPE), jnp.asarray(u_sb, MM_DTYPE), jnp.asarray(low, MM_DTYPE))


def _proj_weights(ng, w_in, qn_ch, kn_ch, qn_fox, kn_fox, bd):
    d = w_in.shape[0]
    w_in = w_in.astype(MM_DTYPE)
    w4 = w_in[:, :N_GROUPS * 4 * W_GROUP].reshape(d, N_GROUPS, 4, W_GROUP)
    w_nat = jnp.transpose(w4[:, :, (1, 3), :], (2, 1, 0, 3)).reshape(2 * N_GROUPS, d, W_GROUP)
    w_tr = jnp.transpose(w4[:, :, (0, 2), :], (2, 1, 3, 0)).reshape(2 * N_GROUPS, W_GROUP, d)
    w_f = jnp.pad(w_in[:, N_GROUPS * 4 * W_GROUP:], ((0, 0), (0, F_PAD - H_GROUP)))
    grow = jnp.stack([jnp.tile(kn_ch, H_GROUP), jnp.tile(kn_fox, H_GROUP)])[:, None, :].astype(F32)
    gcol = jnp.stack([qn_ch, qn_fox])[:, :, None].astype(F32)
    return (ng[None, :], w_nat, w_tr, w_f, grow, gcol, bd)


def _mixers(projected, b_forget, rel_bias, u_sb, low):
    qt, k, vt, g, f = projected
    bias_row = jnp.pad(b_forget.astype(F32), (0, F_PAD - H_GROUP))[None, :]
    cum = _cum_forget(f, bias_row, low)
    m_sb = _sb_attention(qt, k, vt, g, u_sb)
    m_ch = _chunk_attention(qt, k, vt, g, _chunk_bias_table(rel_bias))
    m_fx = _fox_attention(qt, k, vt, g, cum)
    return m_sb, m_ch, m_fx


def kernel(x, norm_g, w_in, b_forget, q_norm_ch, k_norm_ch, q_norm_fox, k_norm_fox, rel_bias, w_out):
    bd, u_sb, low = _constants()
    depth, d = norm_g.shape
    weights = [_proj_weights(norm_g[l], w_in[l], q_norm_ch[l], k_norm_ch[l],
                             q_norm_fox[l], k_norm_fox[l], bd) for l in range(depth)]
    w_o = [w_out[l].reshape(N_GROUPS, W_GROUP, d).astype(MM_DTYPE) for l in range(depth)]
    projected = _projection(x, weights[0])
    for l in range(depth):
        mixed = _mixers(projected, b_forget[l], rel_bias[l], u_sb, low)
        if l + 1 < depth:
            x, *projected = _projection(x, weights[l + 1], prev=(*mixed, w_o[l]))
        else:
            x = _out_projection(x, *mixed, w_o[l])
    return x
```

```python
import jax
import jax.numpy as jnp
import numpy as np
from jax import lax
from jax.experimental import pallas as pl
from jax.experimental.pallas import tpu as pltpu

D_MODEL = 1024
HEAD_DIM = 64
H_GROUP = 8
W_GROUP = H_GROUP * HEAD_DIM
N_GROUPS = 3
CHUNK = 64
N_LEFT_CHUNKS = 8
LEFT = N_LEFT_CHUNKS * CHUNK
REL_CLIP = 128
EPS = 1e-6
SCALE = HEAD_DIM ** -0.5

LANES = 128
MXU_DIM = 256
F_PAD = LANES
NEG = -1e30

MM_DTYPE = jnp.bfloat16
F32 = jnp.float32

TM_PROJ = 512
TQ_ATT = 4096
FOX_KEYS_PER_TRIP = 1024
BF16_ROWS = 16
FOX_V_ROWS = HEAD_DIM + BF16_ROWS
FOX_SKEW = 5
SB_SKEW = 2
SB_STATIC_DIAGS = 2
SB_DEAD_LOG = -104.0
TK_SB = MXU_DIM
TQ_CHUNK = MXU_DIM
WIN_CHUNK = LEFT + TQ_CHUNK
TS_CHUNK = 4096
CUM_BLK = 512
VMEM_LIMIT = 56 * 1024 * 1024


def _dot(a, b):
    return jnp.dot(a, b, preferred_element_type=F32)


def _dot_nt(a, b):
    return lax.dot_general(a, b, (((1,), (1,)), ((), ())), preferred_element_type=F32)


def _split2(x):
    hi = x.astype(MM_DTYPE)
    lo = (x - hi.astype(F32)).astype(MM_DTYPE)
    return hi, lo


def _split3(x):
    hi = x.astype(MM_DTYPE)
    mid, lo = _split2(x - hi.astype(F32))
    return hi, mid, lo


def _head_mask(shape, j, axis):
    idx = lax.broadcasted_iota(jnp.int32, shape, axis)
    return (idx >= j * HEAD_DIM) & (idx < (j + 1) * HEAD_DIM)


def _norm_rows(y, gain_row, bd):
    sq = (y * y).astype(MM_DTYPE)
    parts = []
    for c in range(W_GROUP // MXU_DIM):
        sl = slice(c * MXU_DIM, (c + 1) * MXU_DIM)
        parts.append(_dot(sq[:, sl], bd))
    ssq = jnp.concatenate(parts, axis=1)
    return y * lax.rsqrt(ssq * (1.0 / HEAD_DIM) + EPS) * gain_row


def _norm_cols(yt, gain_col):
    y3 = yt.reshape(H_GROUP, HEAD_DIM, yt.shape[1])
    ssq = jnp.sum(y3 * y3, axis=1, keepdims=True)
    y3 = y3 * lax.rsqrt(ssq * (1.0 / HEAD_DIM) + EPS) * gain_col
    return y3.reshape(yt.shape)


def _proj_kernel(x_ref, *refs):
    _project(x_ref[...], *refs)


def _out_proj_kernel(x_ref, a_ref, b_ref, c_ref, wo_ref, *refs):
    *proj_refs, xo_ref = refs
    tm = x_ref.shape[0]
    parts = []
    for rows in (slice(0, tm // 2), slice(tm // 2, tm)):
        x = x_ref[rows] + (_dot(a_ref[rows], wo_ref[0]) + _dot(b_ref[rows], wo_ref[1])
                           + _dot(c_ref[rows], wo_ref[2]))
        xo_ref[rows] = x
        parts.append(x)
    _project(jnp.concatenate(parts, axis=0), *proj_refs)


def _project(x, ng_ref, wn_ref, wt_ref, wf_ref, grow_ref, gcol_ref, bd_ref,
             qt_ref, k_ref, vt_ref, g_ref, f_ref):
    h = x * lax.rsqrt(jnp.mean(x * x, axis=-1, keepdims=True) + EPS) * ng_ref[...]
    hb = h.astype(MM_DTYPE)
    bd = bd_ref[...]
    dt = qt_ref.dtype
    for grp in range(N_GROUPS):
        qt = _dot_nt(wt_ref[grp], hb)
        k = _dot(hb, wn_ref[grp])
        if grp > 0:
            qt = _norm_cols(qt, gcol_ref[grp - 1])
            k = _norm_rows(k, grow_ref[grp - 1], bd)
        qt_ref[grp] = (qt * SCALE).astype(dt)
        k_ref[grp] = k.astype(dt)
        vt_ref[grp] = _dot_nt(wt_ref[N_GROUPS + grp], hb).astype(dt)
        g_ref[grp] = _dot(hb, wn_ref[N_GROUPS + grp])
    f_ref[...] = _dot(hb, wf_ref[...])


def _projection(x, proj_weights, prev=None):
    b, s, d = x.shape
    tm = min(TM_PROJ, s)
    const = dict(pipeline_mode=pl.Buffered(1))
    row_spec = pl.BlockSpec((None, tm, d), lambda bi, i: (bi, i, 0))
    nat_spec = pl.BlockSpec((N_GROUPS, None, tm, W_GROUP), lambda bi, i: (0, bi, i, 0))
    tr_spec = pl.BlockSpec((N_GROUPS, None, W_GROUP, tm), lambda bi, i: (0, bi, 0, i))
    nat_shape = jax.ShapeDtypeStruct((N_GROUPS, b, s, W_GROUP), MM_DTYPE)
    tr_shape = jax.ShapeDtypeStruct((N_GROUPS, b, W_GROUP, s), MM_DTYPE)
    prev_specs, extra_out_specs, extra_out_shapes = [], [], []
    if prev is not None:
        mix = pl.BlockSpec((None, tm, W_GROUP), lambda bi, i: (bi, i, 0))
        prev_specs = [mix, mix, mix,
                      pl.BlockSpec((N_GROUPS, W_GROUP, d), lambda bi, i: (0, 0, 0), **const)]
        extra_out_specs = [row_spec]
        extra_out_shapes = [jax.ShapeDtypeStruct(x.shape, x.dtype)]
    outs = pl.pallas_call(
        _proj_kernel if prev is None else _out_proj_kernel,
        grid=(b, s // tm),
        in_specs=[row_spec] + prev_specs + [
            pl.BlockSpec((1, d), lambda bi, i: (0, 0)),
            pl.BlockSpec((2 * N_GROUPS, d, W_GROUP), lambda bi, i: (0, 0, 0), **const),
            pl.BlockSpec((2 * N_GROUPS, W_GROUP, d), lambda bi, i: (0, 0, 0), **const),
            pl.BlockSpec((d, F_PAD), lambda bi, i: (0, 0), **const),
            pl.BlockSpec((2, 1, W_GROUP), lambda bi, i: (0, 0, 0)),
            pl.BlockSpec((2, HEAD_DIM, 1), lambda bi, i: (0, 0, 0)),
            pl.BlockSpec((MXU_DIM, MXU_DIM), lambda bi, i: (0, 0)),
        ],
        out_specs=[
            tr_spec, nat_spec, tr_spec,
            pl.BlockSpec((N_GROUPS, None, tm, W_GROUP), lambda bi, i: (0, bi, i, 0)),
            pl.BlockSpec((None, tm, F_PAD), lambda bi, i: (bi, i, 0)),
        ] + extra_out_specs,
        out_shape=[
            tr_shape, nat_shape, tr_shape,
            jax.ShapeDtypeStruct((N_GROUPS, b, s, W_GROUP), F32),
            jax.ShapeDtypeStruct((b, s, F_PAD), F32),
        ] + extra_out_shapes,
        compiler_params=pltpu.CompilerParams(
            dimension_semantics=("arbitrary", "arbitrary"), vmem_limit_bytes=VMEM_LIMIT),
        name="proj" if prev is None else "out_proj_proj",
    )(x, *(prev or ()), *proj_weights)
    return outs if prev is None else (outs[-1], *outs[:-1])


def _cum_kernel(f_ref, b_ref, l_ref, o_ref):
    z = f_ref[...] + b_ref[...]
    lf = jnp.minimum(z, 0.0) - jnp.log1p(jnp.exp(-jnp.abs(z)))
    low = l_ref[...]
    carry = jnp.zeros((1, F_PAD), F32)
    for c in range(lf.shape[0] // CUM_BLK):
        rows = slice(c * CUM_BLK, (c + 1) * CUM_BLK)
        hi, mid, lo = _split3(lf[rows])
        cs = (_dot(low, hi) + _dot(low, mid)) + _dot(low, lo) + carry
        o_ref[rows] = cs
        carry = cs[CUM_BLK - 1:CUM_BLK, :]


def _cum_forget(f, bias_row, low):
    b, s, _ = f.shape
    return pl.pallas_call(
        _cum_kernel,
        grid=(b,),
        in_specs=[
            pl.BlockSpec((None, s, F_PAD), lambda bi: (bi, 0, 0)),
            pl.BlockSpec((1, F_PAD), lambda bi: (0, 0)),
            pl.BlockSpec((CUM_BLK, CUM_BLK), lambda bi: (0, 0)),
        ],
        out_specs=pl.BlockSpec((None, s, F_PAD), lambda bi: (bi, 0, 0)),
        out_shape=jax.ShapeDtypeStruct((b, s, F_PAD), F32),
        compiler_params=pltpu.CompilerParams(dimension_semantics=("arbitrary",)),
        name="cum_forget",
    )(f, bias_row, low)


def _gated_t(g, ot0, ot1):
    o = jnp.concatenate([ot0, ot1], axis=0).T
    return o * (g * jax.nn.sigmoid(g))


def _out_tile(b, p, i):
    return (b, i, p)


def _t_specs(grp, tq, s):
    return [
        pl.BlockSpec((None, None, LANES, tq), lambda b, p, i: (grp, b, p, i)),
        pl.BlockSpec((None, None, s, LANES), lambda b, p, i: (grp, b, 0, p)),
        pl.BlockSpec((None, None, LANES, s), lambda b, p, i: (grp, b, p, 0)),
        pl.BlockSpec((None, None, tq, LANES), lambda b, p, i: (grp, b, i, p)),
    ]


_ATT_PARAMS = pltpu.CompilerParams(
    dimension_semantics=("arbitrary", "arbitrary", "arbitrary"), vmem_limit_bytes=VMEM_LIMIT)


def _neg_abs(x):
    bits = lax.bitcast_convert_type(x, jnp.uint32) | jnp.uint32(0x80000000)
    return lax.bitcast_convert_type(bits, F32)


def _run_skewed(units):
    pending = list(units)
    active = []
    while pending or active:
        if pending:
            active.append(pending.pop(0))
        for g in list(active):
            try:
                next(g)
            except StopIteration:
                active.remove(g)


def _masked_heads_t(qt, other=None):
    if other is None:
        other = jnp.zeros((HEAD_DIM, qt.shape[1]), qt.dtype)
    return [jnp.concatenate([qt[:HEAD_DIM], other], axis=0),
            jnp.concatenate([other, qt[HEAD_DIM:]], axis=0)]


def _sb_kernel(qt_ref, k_ref, vt_ref, g_ref, u_ref, o_ref, acc_ref, carry_ref):
    t = qt_ref.shape[1]
    tk = u_ref.shape[0]
    n_sub = t // tk
    qm = _masked_heads_t(qt_ref[...])
    u = u_ref[...]
    acc_ref[...] = jnp.zeros_like(acc_ref)
    carry_ref[...] = jnp.zeros_like(carry_ref)

    def unit(j, kb, qc, diag):
        cols = slice(qc * tk, (qc + 1) * tk)
        static = isinstance(kb, int)
        k0 = kb * tk if static else pl.multiple_of(jnp.maximum(kb, 0) * tk, tk)
        z = _dot(k_ref[pl.ds(k0, tk), :], qm[j][:, cols])
        for _ in range(SB_SKEW):
            yield
        sp = jnp.maximum(z, 0.0) + jnp.log(1.0 + jnp.exp(_neg_abs(z)))
        if diag:
            strict = (lax.broadcasted_iota(jnp.int32, z.shape, 0)
                      < lax.broadcasted_iota(jnp.int32, z.shape, 1))
            sp = jnp.where(strict, sp, 0.0)
        spb = sp.astype(MM_DTYPE)
        yield
        incl = _dot(u, spb)
        for _ in range(SB_SKEW):
            yield
        w = jnp.exp(z + incl)
        if diag:
            w = jnp.where(strict, w, 0.0)
        wb = w.astype(MM_DTYPE)
        yield
        pv = _dot(vt_ref[j * HEAD_DIM:(j + 1) * HEAD_DIM, pl.ds(k0, tk)], wb)
        carry = carry_ref[j, :, cols]
        scale = jnp.exp(carry)
        step = incl[:1]
        if not static:
            scale = jnp.where(kb >= 0, scale, 0.0)
            step = jnp.where(kb >= 0, step, 0.0)
        acc_ref[j, :, cols] += scale * pv
        carry_ref[j, :, cols] = carry + step

    def diagonal(d, first_qc, masked=False):
        return [unit(j, qc - d, qc, masked) for qc in range(first_qc, n_sub) for j in range(2)]

    def live(d):
        lane = lax.broadcasted_iota(jnp.int32, carry_ref.shape, 2)
        return jnp.max(jnp.where(lane >= d * tk, carry_ref[...], NEG)) >= SB_DEAD_LOG

    n_static = min(SB_STATIC_DIAGS, n_sub)
    _run_skewed([un for d in range(n_static) for un in diagonal(d, d, masked=d == 0)])

    def body(state):
        d, _ = state
        _run_skewed(diagonal(d, n_static))
        return d + 1, live(d + 1)

    lax.while_loop(lambda st: (st[0] < n_sub) & st[1], body, (n_static, live(n_static)))
    o_ref[...] = _gated_t(g_ref[...], acc_ref[0], acc_ref[1]).astype(o_ref.dtype)


def _sb_attention(qt, k, vt, g, u):
    _, b, _, s = qt.shape
    t = s
    tk = u.shape[0]
    return pl.pallas_call(
        _sb_kernel,
        grid=(b, W_GROUP // LANES, s // t),
        in_specs=_t_specs(0, t, s) + [pl.BlockSpec((tk, tk), lambda b_, p, i: (0, 0))],
        out_specs=pl.BlockSpec((None, t, LANES), _out_tile),
        out_shape=jax.ShapeDtypeStruct((b, s, W_GROUP), MM_DTYPE),
        scratch_shapes=[pltpu.VMEM((2, HEAD_DIM, t), F32), pltpu.VMEM((2, 1, t), F32)],
        compiler_params=_ATT_PARAMS,
        name="sb_attention",
    )(qt, k, vt, g, u)


def _chunk_kernel(qt_ref, k_ref, vt_ref, g_ref, bias_ref, o_ref, kpad, vaug, ot_ref):
    i = pl.program_id(2)
    ts = qt_ref.shape[1]
    win, tq = bias_ref.shape[1:]
    s = k_ref.shape[0]
    n_ones = vaug.shape[1] - HEAD_DIM

    @pl.when(i == 0)
    def _():
        kpad[:LEFT] = jnp.zeros((LEFT, LANES), kpad.dtype)
        kpad[LEFT:] = k_ref[...]
        vt = vt_ref[...].astype(F32)
        for j in range(2):
            vaug[j, :, :LEFT] = jnp.zeros((vaug.shape[1], LEFT), vaug.dtype)
            vaug[j, :, LEFT:] = jnp.concatenate(
                [vt[j * HEAD_DIM:(j + 1) * HEAD_DIM], jnp.ones((n_ones, s), F32)],
                axis=0).astype(vaug.dtype)

    qm = _masked_heads_t(qt_ref[...])

    def unit(j, r, first_step):
        cols = slice(r * tq, (r + 1) * tq)
        r0 = pl.multiple_of(i * ts + r * tq, tq)
        zs = []
        for kb in range(win // tq):
            k0 = pl.multiple_of(r0 + kb * tq, tq)
            z = _dot(kpad[pl.ds(k0, tq), :], qm[j][:, cols]) + bias_ref[j, kb * tq:(kb + 1) * tq]
            lo_key = LEFT - r * tq - kb * tq
            if first_step and lo_key > 0:
                key = lax.broadcasted_iota(jnp.int32, z.shape, 0)
                z = jnp.where(key >= lo_key, z, NEG)
            zs.append(z)
            yield
        m = zs[0].max(axis=0, keepdims=True)
        for z in zs[1:]:
            m = jnp.maximum(m, z.max(axis=0, keepdims=True))
        ps = []
        for z in zs:
            ps.append(jnp.exp(z - m).astype(MM_DTYPE))
            yield
        pv = None
        for kb, p in enumerate(ps):
            k0 = pl.multiple_of(r0 + kb * tq, tq)
            term = _dot(vaug[j, :, pl.ds(k0, tq)], p)
            pv = term if pv is None else pv + term
        ot_ref[j * HEAD_DIM:(j + 1) * HEAD_DIM, cols] = pv[:HEAD_DIM] / pv[HEAD_DIM:HEAD_DIM + 1]

    n_stages = 2 * (win // tq) + 1

    def finish(r):
        for _ in range(n_stages):
            yield
        cols = slice(r * tq, (r + 1) * tq)
        o_ref[cols] = _gated_t(g_ref[cols], ot_ref[:HEAD_DIM, cols],
                               ot_ref[HEAD_DIM:, cols]).astype(o_ref.dtype)

    def run(first_step):
        _run_skewed([un for r in range(ts // tq)
                     for un in (unit(0, r, first_step), unit(1, r, first_step), finish(r))])

    if s == ts:
        run(True)
    else:
        pl.when(i == 0)(lambda: run(True))
        pl.when(i > 0)(lambda: run(False))


def _chunk_attention(qt, k, vt, g, bias):
    _, b, _, s = qt.shape
    ts = min(TS_CHUNK, s)
    return pl.pallas_call(
        _chunk_kernel,
        grid=(b, W_GROUP // LANES, s // ts),
        in_specs=_t_specs(1, ts, s) + [
            pl.BlockSpec((2, WIN_CHUNK, TQ_CHUNK), lambda b_, p, i: (p, 0, 0))],
        out_specs=pl.BlockSpec((None, ts, LANES), _out_tile),
        out_shape=jax.ShapeDtypeStruct((b, s, W_GROUP), MM_DTYPE),
        scratch_shapes=[pltpu.VMEM((LEFT + s, LANES), MM_DTYPE),
                        pltpu.VMEM((2, FOX_V_ROWS, LEFT + s), MM_DTYPE),
                        pltpu.VMEM((LANES, ts), F32)],
        compiler_params=_ATT_PARAMS,
        name="chunk_attention",
    )(qt, k, vt, g, bias)


N_DIAG = WIN_CHUNK + TQ_CHUNK


def _bias_kernel(v_ref, o_ref):
    x = jnp.broadcast_to(v_ref[...], (WIN_CHUNK, N_DIAG))
    y = pltpu.roll(x, 1, 1, stride=1, stride_axis=0)
    t = y[:, WIN_CHUNK:]
    c = lax.broadcasted_iota(jnp.int32, t.shape, 0)
    r = lax.broadcasted_iota(jnp.int32, t.shape, 1)
    band = c - (r - (r & (CHUNK - 1)))
    o_ref[...] = jnp.where((band >= 0) & (band < LEFT + CHUNK), t, NEG)


def _chunk_bias_table(rel_bias):
    kk = np.arange(N_DIAG - 1)
    rel = kk - (WIN_CHUNK - 1) + LEFT
    vec = rel_bias[:, np.clip(rel, -REL_CLIP, REL_CLIP) + REL_CLIP].astype(F32)
    vec = jnp.pad(vec, ((0, 0), (0, 1)))[:, None, :]
    h = vec.shape[0]
    return pl.pallas_call(
        _bias_kernel,
        grid=(h,),
        in_specs=[pl.BlockSpec((None, 1, N_DIAG), lambda i: (i, 0, 0))],
        out_specs=pl.BlockSpec((None, WIN_CHUNK, TQ_CHUNK), lambda i: (i, 0, 0)),
        out_shape=jax.ShapeDtypeStruct((h, WIN_CHUNK, TQ_CHUNK), F32),
        compiler_params=pltpu.CompilerParams(dimension_semantics=("arbitrary",)),
        name="chunk_bias",
    )(vec)


def _fox_kernel(qt_ref, k_ref, vt_ref, g_ref, c_ref, o_ref, kaug, vaug, m_ref, acc_ref):
    i = pl.program_id(2)
    p = pl.program_id(1)
    t = qt_ref.shape[1]
    aug0 = [HEAD_DIM * (1 - j) for j in range(2)]
    n_ones = vaug.shape[1] - HEAD_DIM

    @pl.when(i == 0)
    def _():
        k = k_ref[...].astype(F32)
        vt = vt_ref[...].astype(F32)
        pieces = jnp.concatenate(_split3(-c_ref[...]), axis=1)
        row = lax.broadcasted_iota(jnp.int32, (3 * LANES, 2 * LANES), 0)
        col = lax.broadcasted_iota(jnp.int32, (3 * LANES, 2 * LANES), 1)
        place = None
        for j in range(2):
            for c in range(3):
                hit = (row == c * LANES + 2 * p + j) & (col == j * LANES + aug0[j] + c)
                place = hit if place is None else place | hit
        aug = _dot(pieces, jnp.where(place, 1.0, 0.0).astype(MM_DTYPE))
        for j in range(2):
            kaug[j] = jnp.where(_head_mask(k.shape, j, 1), k,
                                aug[:, j * LANES:(j + 1) * LANES]).astype(kaug.dtype)
            vaug[j] = jnp.concatenate(
                [vt[j * HEAD_DIM:(j + 1) * HEAD_DIM], jnp.ones((n_ones, vt.shape[1]), F32)],
                axis=0).astype(vaug.dtype)

    row = lax.broadcasted_iota(jnp.int32, (HEAD_DIM, LANES), 0)
    ones3 = jnp.tile(jnp.where(row < 3, 1.0, 0.0).astype(MM_DTYPE), (1, t // LANES))
    qa = _masked_heads_t(qt_ref[...], ones3)
    m_ref[...] = jnp.full(m_ref.shape, NEG, F32)
    acc_ref[...] = jnp.zeros_like(acc_ref)

    tk = TK_SB
    n_sub = t // tk

    def unit(j, k0, qc, diag):
        cols = slice(qc * tk, (qc + 1) * tk)
        z = _dot(kaug[j, pl.ds(k0, tk), :], qa[j][:, cols])
        for _ in range(FOX_SKEW):
            yield
        if diag:
            causal = (lax.broadcasted_iota(jnp.int32, z.shape, 0)
                      <= lax.broadcasted_iota(jnp.int32, z.shape, 1))
            z = jnp.where(causal, z, NEG)
        m = m_ref[j, :, cols]
        m_new = jnp.maximum(m, jnp.max(z, axis=0, keepdims=True))
        pr = jnp.exp(z - m_new).astype(MM_DTYPE)
        alpha = jnp.exp(m - m_new)
        m_ref[j, :, cols] = m_new
        yield
        pv = _dot(vaug[j, :, pl.ds(k0, tk)], pr)
        acc_ref[j, :, cols] = alpha * acc_ref[j, :, cols] + pv

    def units(base, n_blocks, diag):
        out = []
        for c in range(n_blocks):
            for qc in range(c if diag else 0, n_sub):
                for j in range(2):
                    out.append(unit(j, pl.multiple_of(base + c * tk, tk), qc, diag and qc == c))
        return out

    trip = min(FOX_KEYS_PER_TRIP, t)

    def body(kj, _):
        _run_skewed(units(kj * trip, trip // tk, False))
        return 0

    if kaug.shape[1] > t:
        lax.fori_loop(0, i * (t // trip), body, 0)
    _run_skewed(units(i * t, n_sub, True))
    outs = [acc_ref[j, :HEAD_DIM] / acc_ref[j, HEAD_DIM:HEAD_DIM + 1] for j in range(2)]
    o_ref[...] = _gated_t(g_ref[...], outs[0], outs[1]).astype(o_ref.dtype)


def _fox_attention(qt, k, vt, g, cum):
    _, b, _, s = qt.shape
    t = min(TQ_ATT, s)
    return pl.pallas_call(
        _fox_kernel,
        grid=(b, W_GROUP // LANES, s // t),
        in_specs=_t_specs(2, t, s) + [
            pl.BlockSpec((None, s, F_PAD), lambda b_, p, i: (b_, 0, 0))],
        out_specs=pl.BlockSpec((None, t, LANES), _out_tile),
        out_shape=jax.ShapeDtypeStruct((b, s, W_GROUP), MM_DTYPE),
        scratch_shapes=[pltpu.VMEM((2, s, LANES), MM_DTYPE),
                        pltpu.VMEM((2, FOX_V_ROWS, s), MM_DTYPE),
                        pltpu.VMEM((2, 1, t), F32), pltpu.VMEM((2, FOX_V_ROWS, t), F32)],
        compiler_params=_ATT_PARAMS,
        name="fox_attention",
    )(qt, k, vt, g, cum)


def _out_kernel(x_ref, a_ref, b_ref, c_ref, w_ref, o_ref):
    acc = _dot(a_ref[...], w_ref[0]) + _dot(b_ref[...], w_ref[1]) + _dot(c_ref[...], w_ref[2])
    o_ref[...] = x_ref[...] + acc


def _out_projection(x, ma, mb, mc, w):
    b, s, d = x.shape
    tm = min(TM_PROJ, s)
    mix = pl.BlockSpec((None, tm, W_GROUP), lambda bi, i: (bi, i, 0))
    return pl.pallas_call(
        _out_kernel,
        grid=(b, s // tm),
        in_specs=[pl.BlockSpec((None, tm, d), lambda bi, i: (bi, i, 0)), mix, mix, mix,
                  pl.BlockSpec((N_GROUPS, W_GROUP, d), lambda bi, i: (0, 0, 0))],
        out_specs=pl.BlockSpec((None, tm, d), lambda bi, i: (bi, i, 0)),
        out_shape=jax.ShapeDtypeStruct(x.shape, x.dtype),
        compiler_params=pltpu.CompilerParams(
            dimension_semantics=("arbitrary", "arbitrary"), vmem_limit_bytes=VMEM_LIMIT),
        name="out_proj",
    )(x, ma, mb, mc, w)


def _constants():
    r = np.arange(MXU_DIM)
    bd = (r[:, None] // HEAD_DIM == r[None, :] // HEAD_DIM).astype(np.float32)
    r = np.arange(TK_SB)
    u_sb = -(r[None, :] >= r[:, None]).astype(np.float32)
    r = np.arange(CUM_BLK)
    low = (r[None, :] <= r[:, None]).astype(np.float32)
    return (jnp.asarray(bd, MM_DTYPE), jnp.asarray(u_sb, MM_DTYPE), jnp.asarray(low, MM_DTYPE))


def _proj_weights(ng, w_in, qn_ch, kn_ch, qn_fox, kn_fox, bd):
    d = w_in.shape[0]
    w_in = w_in.astype(MM_DTYPE)
    w4 = w_in[:, :N_GROUPS * 4 * W_GROUP].reshape(d, N_GROUPS, 4, W_GROUP)
    w_nat = jnp.transpose(w4[:, :, (1, 3), :], (2, 1, 0, 3)).reshape(2 * N_GROUPS, d, W_GROUP)
    w_tr = jnp.transpose(w4[:, :, (0, 2), :], (2, 1, 3, 0)).reshape(2 * N_GROUPS, W_GROUP, d)
    w_f = jnp.pad(w_in[:, N_GROUPS * 4 * W_GROUP:], ((0, 0), (0, F_PAD - H_GROUP)))
    grow = jnp.stack([jnp.tile(kn_ch, H_GROUP), jnp.tile(kn_fox, H_GROUP)])[:, None, :].astype(F32)
    gcol = jnp.stack([qn_ch, qn_fox])[:, :, None].astype(F32)
    return (ng[None, :], w_nat, w_tr, w_f, grow, gcol, bd)


def _mixers(projected, b_forget, rel_bias, u_sb, low):
    qt, k, vt, g, f = projected
    bias_row = jnp.pad(b_forget.astype(F32), (0, F_PAD - H_GROUP))[None, :]
    cum = _cum_forget(f, bias_row, low)
    m_sb = _sb_attention(qt, k, vt, g, u_sb)
    m_ch = _chunk_attention(qt, k, vt, g, _chunk_bias_table(rel_bias))
    m_fx = _fox_attention(qt, k, vt, g, cum)
    return m_sb, m_ch, m_fx


def kernel(x, norm_g, w_in, b_forget, q_norm_ch, k_norm_ch, q_norm_fox, k_norm_fox, rel_bias, w_out):
    bd, u_sb, low = _constants()
    depth, d = norm_g.shape
    weights = [_proj_weights(norm_g[l], w_in[l], q_norm_ch[l], k_norm_ch[l],
                             q_norm_fox[l], k_norm_fox[l], bd) for l in range(depth)]
    w_o = [w_out[l].reshape(N_GROUPS, W_GROUP, d).astype(MM_DTYPE) for l in range(depth)]
    projected = _projection(x, weights[0])
    for l in range(depth):
        mixed = _mixers(projected, b_forget[l], rel_bias[l], u_sb, low)
        if l + 1 < depth:
            x, *projected = _projection(x, weights[l + 1], prev=(*mixed, w_o[l]))
        else:
            x = _out_projection(x, *mixed, w_o[l])
    return x
```

```python
import jax
import jax.numpy as jnp
import numpy as np
from jax import lax
from jax.experimental import pallas as pl
from jax.experimental.pallas import tpu as pltpu

D_MODEL = 1024
HEAD_DIM = 64
H_GROUP = 8
W_GROUP = H_GROUP * HEAD_DIM
N_GROUPS = 3
CHUNK = 64
N_LEFT_CHUNKS = 8
LEFT = N_LEFT_CHUNKS * CHUNK
REL_CLIP = 128
EPS = 1e-6
SCALE = HEAD_DIM ** -0.5

LANES = 128
MXU_DIM = 256
F_PAD = LANES
NEG = -1e30

MM_DTYPE = jnp.bfloat16
F32 = jnp.float32

TM_PROJ = 512
TM_OUT = 1024
TQ_ATT = 4096
FOX_KEYS_PER_TRIP = 1024
BF16_ROWS = 16
FOX_V_ROWS = HEAD_DIM + BF16_ROWS
FOX_SKEW = 5
SB_SKEW = 2
SB_STATIC_DIAGS = 2
SB_DEAD_LOG = -104.0
TK_SB = MXU_DIM
TQ_CHUNK = MXU_DIM
WIN_CHUNK = LEFT + TQ_CHUNK
TS_CHUNK = 4096
CUM_BLK = 512
VMEM_LIMIT = 56 * 1024 * 1024


def _dot(a, b):
    return jnp.dot(a, b, preferred_element_type=F32)


def _dot_nt(a, b):
    return lax.dot_general(a, b, (((1,), (1,)), ((), ())), preferred_element_type=F32)


def _split2(x):
    hi = x.astype(MM_DTYPE)
    lo = (x - hi.astype(F32)).astype(MM_DTYPE)
    return hi, lo


def _split3(x):
    hi = x.astype(MM_DTYPE)
    mid, lo = _split2(x - hi.astype(F32))
    return hi, mid, lo


def _head_mask(shape, j, axis):
    idx = lax.broadcasted_iota(jnp.int32, shape, axis)
    return (idx >= j * HEAD_DIM) & (idx < (j + 1) * HEAD_DIM)


def _norm_rows(y, gain_row, bd):
    sq = (y * y).astype(MM_DTYPE)
    parts = []
    for c in range(W_GROUP // MXU_DIM):
        sl = slice(c * MXU_DIM, (c + 1) * MXU_DIM)
        parts.append(_dot(sq[:, sl], bd))
    ssq = jnp.concatenate(parts, axis=1)
    return y * lax.rsqrt(ssq * (1.0 / HEAD_DIM) + EPS) * gain_row


def _norm_cols(yt, gain_col):
    y3 = yt.reshape(H_GROUP, HEAD_DIM, yt.shape[1])
    ssq = jnp.sum(y3 * y3, axis=1, keepdims=True)
    y3 = y3 * lax.rsqrt(ssq * (1.0 / HEAD_DIM) + EPS) * gain_col
    return y3.reshape(yt.shape)


def _proj_kernel(x_ref, *refs):
    _project(x_ref[...], *refs)


def _out_proj_kernel(x_ref, a_ref, b_ref, c_ref, wo_ref, *refs):
    *proj_refs, xo_ref = refs
    tm = x_ref.shape[0]
    parts = []
    for rows in (slice(0, tm // 2), slice(tm // 2, tm)):
        x = x_ref[rows] + (_dot(a_ref[rows], wo_ref[0]) + _dot(b_ref[rows], wo_ref[1])
                           + _dot(c_ref[rows], wo_ref[2]))
        xo_ref[rows] = x
        parts.append(x)
    _project(jnp.concatenate(parts, axis=0), *proj_refs)


def _project(x, ng_ref, wn_ref, wt_ref, wf_ref, grow_ref, gcol_ref, bd_ref,
             qt_ref, k_ref, vt_ref, g_ref, f_ref):
    h = x * lax.rsqrt(jnp.mean(x * x, axis=-1, keepdims=True) + EPS) * ng_ref[...]
    hb = h.astype(MM_DTYPE)
    bd = bd_ref[...]
    dt = qt_ref.dtype
    for grp in range(N_GROUPS):
        qt = _dot_nt(wt_ref[grp], hb)
        k = _dot(hb, wn_ref[grp])
        if grp > 0:
            qt = _norm_cols(qt, gcol_ref[grp - 1])
            k = _norm_rows(k, grow_ref[grp - 1], bd)
        qt_ref[grp] = (qt * SCALE).astype(dt)
        k_ref[grp] = k.astype(dt)
        vt_ref[grp] = _dot_nt(wt_ref[N_GROUPS + grp], hb).astype(dt)
        g_ref[grp] = _dot(hb, wn_ref[N_GROUPS + grp])
    f_ref[...] = _dot(hb, wf_ref[...])


def _projection(x, proj_weights, prev=None):
    b, s, d = x.shape
    tm = min(TM_PROJ, s)
    const = dict(pipeline_mode=pl.Buffered(1))
    row_spec = pl.BlockSpec((None, tm, d), lambda bi, i: (bi, i, 0))
    nat_spec = pl.BlockSpec((N_GROUPS, None, tm, W_GROUP), lambda bi, i: (0, bi, i, 0))
    tr_spec = pl.BlockSpec((N_GROUPS, None, W_GROUP, tm), lambda bi, i: (0, bi, 0, i))
    nat_shape = jax.ShapeDtypeStruct((N_GROUPS, b, s, W_GROUP), MM_DTYPE)
    tr_shape = jax.ShapeDtypeStruct((N_GROUPS, b, W_GROUP, s), MM_DTYPE)
    prev_specs, extra_out_specs, extra_out_shapes = [], [], []
    if prev is not None:
        mix = pl.BlockSpec((None, tm, W_GROUP), lambda bi, i: (bi, i, 0))
        prev_specs = [mix, mix, mix,
                      pl.BlockSpec((N_GROUPS, W_GROUP, d), lambda bi, i: (0, 0, 0), **const)]
        extra_out_specs = [row_spec]
        extra_out_shapes = [jax.ShapeDtypeStruct(x.shape, x.dtype)]
    outs = pl.pallas_call(
        _proj_kernel if prev is None else _out_proj_kernel,
        grid=(b, s // tm),
        in_specs=[row_spec] + prev_specs + [
            pl.BlockSpec((1, d), lambda bi, i: (0, 0)),
            pl.BlockSpec((2 * N_GROUPS, d, W_GROUP), lambda bi, i: (0, 0, 0), **const),
            pl.BlockSpec((2 * N_GROUPS, W_GROUP, d), lambda bi, i: (0, 0, 0), **const),
            pl.BlockSpec((d, F_PAD), lambda bi, i: (0, 0), **const),
            pl.BlockSpec((2, 1, W_GROUP), lambda bi, i: (0, 0, 0)),
            pl.BlockSpec((2, HEAD_DIM, 1), lambda bi, i: (0, 0, 0)),
            pl.BlockSpec((MXU_DIM, MXU_DIM), lambda bi, i: (0, 0)),
        ],
        out_specs=[
            tr_spec, nat_spec, tr_spec,
            pl.BlockSpec((N_GROUPS, None, tm, W_GROUP), lambda bi, i: (0, bi, i, 0)),
            pl.BlockSpec((None, tm, F_PAD), lambda bi, i: (bi, i, 0)),
        ] + extra_out_specs,
        out_shape=[
            tr_shape, nat_shape, tr_shape,
            jax.ShapeDtypeStruct((N_GROUPS, b, s, W_GROUP), F32),
            jax.ShapeDtypeStruct((b, s, F_PAD), F32),
        ] + extra_out_shapes,
        compiler_params=pltpu.CompilerParams(
            dimension_semantics=("arbitrary", "arbitrary"), vmem_limit_bytes=VMEM_LIMIT),
        name="proj" if prev is None else "out_proj_proj",
    )(x, *(prev or ()), *proj_weights)
    return outs if prev is None else (outs[-1], *outs[:-1])


def _cum_kernel(f_ref, b_ref, l_ref, o_ref):
    z = f_ref[...] + b_ref[...]
    lf = jnp.minimum(z, 0.0) - jnp.log1p(jnp.exp(-jnp.abs(z)))
    low = l_ref[...]
    carry = jnp.zeros((1, F_PAD), F32)
    for c in range(lf.shape[0] // CUM_BLK):
        rows = slice(c * CUM_BLK, (c + 1) * CUM_BLK)
        hi, mid, lo = _split3(lf[rows])
        cs = (_dot(low, hi) + _dot(low, mid)) + _dot(low, lo) + carry
        o_ref[rows] = cs
        carry = cs[CUM_BLK - 1:CUM_BLK, :]


def _cum_forget(f, bias_row, low):
    b, s, _ = f.shape
    return pl.pallas_call(
        _cum_kernel,
        grid=(b,),
        in_specs=[
            pl.BlockSpec((None, s, F_PAD), lambda bi: (bi, 0, 0)),
            pl.BlockSpec((1, F_PAD), lambda bi: (0, 0)),
            pl.BlockSpec((CUM_BLK, CUM_BLK), lambda bi: (0, 0)),
        ],
        out_specs=pl.BlockSpec((None, s, F_PAD), lambda bi: (bi, 0, 0)),
        out_shape=jax.ShapeDtypeStruct((b, s, F_PAD), F32),
        compiler_params=pltpu.CompilerParams(dimension_semantics=("arbitrary",)),
        name="cum_forget",
    )(f, bias_row, low)


def _gated_t(g, ot0, ot1):
    o = jnp.concatenate([ot0, ot1], axis=0).T
    return o * (g * jax.nn.sigmoid(g))


def _out_tile(b, p, i):
    return (b, i, p)


def _t_specs(grp, tq, s):
    return [
        pl.BlockSpec((None, None, LANES, tq), lambda b, p, i: (grp, b, p, i)),
        pl.BlockSpec((None, None, s, LANES), lambda b, p, i: (grp, b, 0, p)),
        pl.BlockSpec((None, None, LANES, s), lambda b, p, i: (grp, b, p, 0)),
        pl.BlockSpec((None, None, tq, LANES), lambda b, p, i: (grp, b, i, p)),
    ]


_ATT_PARAMS = pltpu.CompilerParams(
    dimension_semantics=("arbitrary", "arbitrary", "arbitrary"), vmem_limit_bytes=VMEM_LIMIT)


def _neg_abs(x):
    bits = lax.bitcast_convert_type(x, jnp.uint32) | jnp.uint32(0x80000000)
    return lax.bitcast_convert_type(bits, F32)


def _run_skewed(units):
    pending = list(units)
    active = []
    while pending or active:
        if pending:
            active.append(pending.pop(0))
        for g in list(active):
            try:
                next(g)
            except StopIteration:
                active.remove(g)


def _masked_heads_t(qt, other=None):
    if other is None:
        other = jnp.zeros((HEAD_DIM, qt.shape[1]), qt.dtype)
    return [jnp.concatenate([qt[:HEAD_DIM], other], axis=0),
            jnp.concatenate([other, qt[HEAD_DIM:]], axis=0)]


def _sb_kernel(qt_ref, k_ref, vt_ref, g_ref, u_ref, o_ref, acc_ref, carry_ref):
    t = qt_ref.shape[1]
    tk = u_ref.shape[0]
    n_sub = t // tk
    qm = _masked_heads_t(qt_ref[...])
    u = u_ref[...]
    acc_ref[...] = jnp.zeros_like(acc_ref)
    carry_ref[...] = jnp.zeros_like(carry_ref)

    def unit(j, kb, qc, diag):
        cols = slice(qc * tk, (qc + 1) * tk)
        static = isinstance(kb, int)
        k0 = kb * tk if static else pl.multiple_of(jnp.maximum(kb, 0) * tk, tk)
        z = _dot(k_ref[pl.ds(k0, tk), :], qm[j][:, cols])
        for _ in range(SB_SKEW):
            yield
        sp = jnp.maximum(z, 0.0) + jnp.log(1.0 + jnp.exp(_neg_abs(z)))
        if diag:
            strict = (lax.broadcasted_iota(jnp.int32, z.shape, 0)
                      < lax.broadcasted_iota(jnp.int32, z.shape, 1))
            sp = jnp.where(strict, sp, 0.0)
        spb = sp.astype(MM_DTYPE)
        yield
        incl = _dot(u, spb)
        for _ in range(SB_SKEW):
            yield
        w = jnp.exp(z + incl)
        if diag:
            w = jnp.where(strict, w, 0.0)
        wb = w.astype(MM_DTYPE)
        yield
        pv = _dot(vt_ref[j * HEAD_DIM:(j + 1) * HEAD_DIM, pl.ds(k0, tk)], wb)
        carry = carry_ref[j, :, cols]
        scale = jnp.exp(carry)
        step = incl[:1]
        if not static:
            scale = jnp.where(kb >= 0, scale, 0.0)
            step = jnp.where(kb >= 0, step, 0.0)
        acc_ref[j, :, cols] += scale * pv
        carry_ref[j, :, cols] = carry + step

    def diagonal(d, first_qc, masked=False):
        return [unit(j, qc - d, qc, masked) for qc in range(first_qc, n_sub) for j in range(2)]

    def live(d):
        lane = lax.broadcasted_iota(jnp.int32, carry_ref.shape, 2)
        return jnp.max(jnp.where(lane >= d * tk, carry_ref[...], NEG)) >= SB_DEAD_LOG

    def store(rows):
        o_ref[rows] = _gated_t(g_ref[rows], acc_ref[0, :, rows], acc_ref[1, :, rows]).astype(o_ref.dtype)

    def finish(qc):
        for _ in range(2 * SB_SKEW + 3):
            yield
        store(slice(qc * tk, (qc + 1) * tk))

    n_static = min(SB_STATIC_DIAGS, n_sub)
    static_units = []
    for d in range(n_static):
        for qc in range(d, n_sub):
            static_units += [unit(j, qc - d, qc, d == 0) for j in range(2)]
            if d == min(qc, n_static - 1):
                static_units.append(finish(qc))
    _run_skewed(static_units)

    def body(state):
        d, _ = state
        _run_skewed(diagonal(d, n_static))
        return d + 1, live(d + 1)

    if n_static < n_sub:
        d_end, _ = lax.while_loop(lambda st: (st[0] < n_sub) & st[1], body,
                                  (n_static, live(n_static)))
        pl.when(d_end > n_static)(lambda: store(slice(n_static * tk, t)))


def _sb_attention(qt, k, vt, g, u):
    _, b, _, s = qt.shape
    t = s
    tk = u.shape[0]
    return pl.pallas_call(
        _sb_kernel,
        grid=(b, W_GROUP // LANES, s // t),
        in_specs=_t_specs(0, t, s) + [pl.BlockSpec((tk, tk), lambda b_, p, i: (0, 0))],
        out_specs=pl.BlockSpec((None, t, LANES), _out_tile),
        out_shape=jax.ShapeDtypeStruct((b, s, W_GROUP), MM_DTYPE),
        scratch_shapes=[pltpu.VMEM((2, HEAD_DIM, t), F32), pltpu.VMEM((2, 1, t), F32)],
        compiler_params=_ATT_PARAMS,
        name="sb_attention",
    )(qt, k, vt, g, u)


def _chunk_kernel(qt_ref, k_ref, vt_ref, g_ref, bias_ref, o_ref, kpad, vaug, ot_ref):
    i = pl.program_id(2)
    ts = qt_ref.shape[1]
    win, tq = bias_ref.shape[1:]
    s = k_ref.shape[0]
    n_ones = vaug.shape[1] - HEAD_DIM

    @pl.when(i == 0)
    def _():
        kpad[:LEFT] = jnp.zeros((LEFT, LANES), kpad.dtype)
        kpad[LEFT:] = k_ref[...]
        vt = vt_ref[...].astype(F32)
        for j in range(2):
            vaug[j, :, :LEFT] = jnp.zeros((vaug.shape[1], LEFT), vaug.dtype)
            vaug[j, :, LEFT:] = jnp.concatenate(
                [vt[j * HEAD_DIM:(j + 1) * HEAD_DIM], jnp.ones((n_ones, s), F32)],
                axis=0).astype(vaug.dtype)

    qm = _masked_heads_t(qt_ref[...])

    def unit(j, r, first_step):
        cols = slice(r * tq, (r + 1) * tq)
        r0 = pl.multiple_of(i * ts + r * tq, tq)
        zs = []
        for kb in range(win // tq):
            k0 = pl.multiple_of(r0 + kb * tq, tq)
            z = _dot(kpad[pl.ds(k0, tq), :], qm[j][:, cols]) + bias_ref[j, kb * tq:(kb + 1) * tq]
            lo_key = LEFT - r * tq - kb * tq
            if first_step and lo_key > 0:
                key = lax.broadcasted_iota(jnp.int32, z.shape, 0)
                z = jnp.where(key >= lo_key, z, NEG)
            zs.append(z)
            yield
        m = zs[0].max(axis=0, keepdims=True)
        for z in zs[1:]:
            m = jnp.maximum(m, z.max(axis=0, keepdims=True))
        ps = []
        for z in zs:
            ps.append(jnp.exp(z - m).astype(MM_DTYPE))
            yield
        pv = None
        for kb, p in enumerate(ps):
            k0 = pl.multiple_of(r0 + kb * tq, tq)
            term = _dot(vaug[j, :, pl.ds(k0, tq)], p)
            pv = term if pv is None else pv + term
        ot_ref[j * HEAD_DIM:(j + 1) * HEAD_DIM, cols] = pv[:HEAD_DIM] / pv[HEAD_DIM:HEAD_DIM + 1]

    n_stages = 2 * (win // tq) + 1

    def finish(r):
        for _ in range(n_stages):
            yield
        cols = slice(r * tq, (r + 1) * tq)
        o_ref[cols] = _gated_t(g_ref[cols], ot_ref[:HEAD_DIM, cols],
                               ot_ref[HEAD_DIM:, cols]).astype(o_ref.dtype)

    def run(first_step):
        _run_skewed([un for r in range(ts // tq)
                     for un in (unit(0, r, first_step), unit(1, r, first_step), finish(r))])

    if s == ts:
        run(True)
    else:
        pl.when(i == 0)(lambda: run(True))
        pl.when(i > 0)(lambda: run(False))


def _chunk_attention(qt, k, vt, g, bias):
    _, b, _, s = qt.shape
    ts = min(TS_CHUNK, s)
    return pl.pallas_call(
        _chunk_kernel,
        grid=(b, W_GROUP // LANES, s // ts),
        in_specs=_t_specs(1, ts, s) + [
            pl.BlockSpec((2, WIN_CHUNK, TQ_CHUNK), lambda b_, p, i: (p, 0, 0))],
        out_specs=pl.BlockSpec((None, ts, LANES), _out_tile),
        out_shape=jax.ShapeDtypeStruct((b, s, W_GROUP), MM_DTYPE),
        scratch_shapes=[pltpu.VMEM((LEFT + s, LANES), MM_DTYPE),
                        pltpu.VMEM((2, FOX_V_ROWS, LEFT + s), MM_DTYPE),
                        pltpu.VMEM((LANES, ts), F32)],
        compiler_params=_ATT_PARAMS,
        name="chunk_attention",
    )(qt, k, vt, g, bias)


N_DIAG = WIN_CHUNK + TQ_CHUNK


def _bias_kernel(v_ref, o_ref):
    x = jnp.broadcast_to(v_ref[...], (WIN_CHUNK, N_DIAG))
    y = pltpu.roll(x, 1, 1, stride=1, stride_axis=0)
    t = y[:, WIN_CHUNK:]
    c = lax.broadcasted_iota(jnp.int32, t.shape, 0)
    r = lax.broadcasted_iota(jnp.int32, t.shape, 1)
    band = c - (r - (r & (CHUNK - 1)))
    o_ref[...] = jnp.where((band >= 0) & (band < LEFT + CHUNK), t, NEG)


def _chunk_bias_table(rel_bias):
    kk = np.arange(N_DIAG - 1)
    rel = kk - (WIN_CHUNK - 1) + LEFT
    vec = rel_bias[:, np.clip(rel, -REL_CLIP, REL_CLIP) + REL_CLIP].astype(F32)
    vec = jnp.pad(vec, ((0, 0), (0, 1)))[:, None, :]
    h = vec.shape[0]
    return pl.pallas_call(
        _bias_kernel,
        grid=(h,),
        in_specs=[pl.BlockSpec((None, 1, N_DIAG), lambda i: (i, 0, 0))],
        out_specs=pl.BlockSpec((None, WIN_CHUNK, TQ_CHUNK), lambda i: (i, 0, 0)),
        out_shape=jax.ShapeDtypeStruct((h, WIN_CHUNK, TQ_CHUNK), F32),
        compiler_params=pltpu.CompilerParams(dimension_semantics=("arbitrary",)),
        name="chunk_bias",
    )(vec)


def _fox_kernel(qt_ref, k_ref, vt_ref, g_ref, c_ref, o_ref, kaug, vaug, m_ref, acc_ref):
    i = pl.program_id(2)
    p = pl.program_id(1)
    t = qt_ref.shape[1]
    aug0 = [HEAD_DIM * (1 - j) for j in range(2)]
    n_ones = vaug.shape[1] - HEAD_DIM

    @pl.when(i == 0)
    def _():
        k = k_ref[...].astype(F32)
        vt = vt_ref[...].astype(F32)
        pieces = jnp.concatenate(_split3(-c_ref[...]), axis=1)
        row = lax.broadcasted_iota(jnp.int32, (3 * LANES, 2 * LANES), 0)
        col = lax.broadcasted_iota(jnp.int32, (3 * LANES, 2 * LANES), 1)
        place = None
        for j in range(2):
            for c in range(3):
                hit = (row == c * LANES + 2 * p + j) & (col == j * LANES + aug0[j] + c)
                place = hit if place is None else place | hit
        aug = _dot(pieces, jnp.where(place, 1.0, 0.0).astype(MM_DTYPE))
        for j in range(2):
            kaug[j] = jnp.where(_head_mask(k.shape, j, 1), k,
                                aug[:, j * LANES:(j + 1) * LANES]).astype(kaug.dtype)
            vaug[j] = jnp.concatenate(
                [vt[j * HEAD_DIM:(j + 1) * HEAD_DIM], jnp.ones((n_ones, vt.shape[1]), F32)],
                axis=0).astype(vaug.dtype)

    row = lax.broadcasted_iota(jnp.int32, (HEAD_DIM, LANES), 0)
    ones3 = jnp.tile(jnp.where(row < 3, 1.0, 0.0).astype(MM_DTYPE), (1, t // LANES))
    qa = _masked_heads_t(qt_ref[...], ones3)
    m_ref[...] = jnp.full(m_ref.shape, NEG, F32)
    acc_ref[...] = jnp.zeros_like(acc_ref)

    tk = TK_SB
    n_sub = t // tk

    def unit(j, k0, qc, diag):
        cols = slice(qc * tk, (qc + 1) * tk)
        z = _dot(kaug[j, pl.ds(k0, tk), :], qa[j][:, cols])
        for _ in range(FOX_SKEW):
            yield
        if diag:
            causal = (lax.broadcasted_iota(jnp.int32, z.shape, 0)
                      <= lax.broadcasted_iota(jnp.int32, z.shape, 1))
            z = jnp.where(causal, z, NEG)
        m = m_ref[j, :, cols]
        m_new = jnp.maximum(m, jnp.max(z, axis=0, keepdims=True))
        pr = jnp.exp(z - m_new).astype(MM_DTYPE)
        alpha = jnp.exp(m - m_new)
        m_ref[j, :, cols] = m_new
        yield
        pv = _dot(vaug[j, :, pl.ds(k0, tk)], pr)
        acc_ref[j, :, cols] = alpha * acc_ref[j, :, cols] + pv

    def units(base, n_blocks, diag):
        out = []
        for c in range(n_blocks):
            for qc in range(c if diag else 0, n_sub):
                for j in range(2):
                    out.append(unit(j, pl.multiple_of(base + c * tk, tk), qc, diag and qc == c))
        return out

    trip = min(FOX_KEYS_PER_TRIP, t)

    def body(kj, _):
        _run_skewed(units(kj * trip, trip // tk, False))
        return 0

    if kaug.shape[1] > t:
        lax.fori_loop(0, i * (t // trip), body, 0)
    _run_skewed(units(i * t, n_sub, True))
    outs = [acc_ref[j, :HEAD_DIM] / acc_ref[j, HEAD_DIM:HEAD_DIM + 1] for j in range(2)]
    o_ref[...] = _gated_t(g_ref[...], outs[0], outs[1]).astype(o_ref.dtype)


def _fox_attention(qt, k, vt, g, cum):
    _, b, _, s = qt.shape
    t = min(TQ_ATT, s)
    return pl.pallas_call(
        _fox_kernel,
        grid=(b, W_GROUP // LANES, s // t),
        in_specs=_t_specs(2, t, s) + [
            pl.BlockSpec((None, s, F_PAD), lambda b_, p, i: (b_, 0, 0))],
        out_specs=pl.BlockSpec((None, t, LANES), _out_tile),
        out_shape=jax.ShapeDtypeStruct((b, s, W_GROUP), MM_DTYPE),
        scratch_shapes=[pltpu.VMEM((2, s, LANES), MM_DTYPE),
                        pltpu.VMEM((2, FOX_V_ROWS, s), MM_DTYPE),
                        pltpu.VMEM((2, 1, t), F32), pltpu.VMEM((2, FOX_V_ROWS, t), F32)],
        compiler_params=_ATT_PARAMS,
        name="fox_attention",
    )(qt, k, vt, g, cum)


def _out_kernel(x_ref, a_ref, b_ref, c_ref, w_ref, o_ref):
    acc = _dot(a_ref[...], w_ref[0]) + _dot(b_ref[...], w_ref[1]) + _dot(c_ref[...], w_ref[2])
    o_ref[...] = x_ref[...] + acc


def _out_projection(x, ma, mb, mc, w):
    b, s, d = x.shape
    tm = min(TM_OUT, s)
    mix = pl.BlockSpec((None, tm, W_GROUP), lambda bi, i: (bi, i, 0))
    return pl.pallas_call(
        _out_kernel,
        grid=(b, s // tm),
        in_specs=[pl.BlockSpec((None, tm, d), lambda bi, i: (bi, i, 0)), mix, mix, mix,
                  pl.BlockSpec((N_GROUPS, W_GROUP, d), lambda bi, i: (0, 0, 0))],
        out_specs=pl.BlockSpec((None, tm, d), lambda bi, i: (bi, i, 0)),
        out_shape=jax.ShapeDtypeStruct(x.shape, x.dtype),
        compiler_params=pltpu.CompilerParams(
            dimension_semantics=("arbitrary", "arbitrary"), vmem_limit_bytes=VMEM_LIMIT),
        name="out_proj",
    )(x, ma, mb, mc, w)


def _constants():
    r = np.arange(MXU_DIM)
    bd = (r[:, None] // HEAD_DIM == r[None, :] // HEAD_DIM).astype(np.float32)
    r = np.arange(TK_SB)
    u_sb = -(r[None, :] >= r[:, None]).astype(np.float32)
    r = np.arange(CUM_BLK)
    low = (r[None, :] <= r[:, None]).astype(np.float32)
    return (jnp.asarray(bd, MM_DTYPE), jnp.asarray(u_sb, MM_DTYPE), jnp.asarray(low, MM_DTYPE))


def _proj_weights(ng, w_in, qn_ch, kn_ch, qn_fox, kn_fox, bd):
    d = w_in.shape[0]
    w_in = w_in.astype(MM_DTYPE)
    w4 = w_in[:, :N_GROUPS * 4 * W_GROUP].reshape(d, N_GROUPS, 4, W_GROUP)
    w_nat = jnp.transpose(w4[:, :, (1, 3), :], (2, 1, 0, 3)).reshape(2 * N_GROUPS, d, W_GROUP)
    w_tr = jnp.transpose(w4[:, :, (0, 2), :], (2, 1, 3, 0)).reshape(2 * N_GROUPS, W_GROUP, d)
    w_f = jnp.pad(w_in[:, N_GROUPS * 4 * W_GROUP:], ((0, 0), (0, F_PAD - H_GROUP)))
    grow = jnp.stack([jnp.tile(kn_ch, H_GROUP), jnp.tile(kn_fox, H_GROUP)])[:, None, :].astype(F32)
    gcol = jnp.stack([qn_ch, qn_fox])[:, :, None].astype(F32)
    return (ng[None, :], w_nat, w_tr, w_f, grow, gcol, bd)


def _mixers(projected, b_forget, rel_bias, u_sb, low):
    qt, k, vt, g, f = projected
    bias_row = jnp.pad(b_forget.astype(F32), (0, F_PAD - H_GROUP))[None, :]
    cum = _cum_forget(f, bias_row, low)
    m_sb = _sb_attention(qt, k, vt, g, u_sb)
    m_ch = _chunk_attention(qt, k, vt, g, _chunk_bias_table(rel_bias))
    m_fx = _fox_attention(qt, k, vt, g, cum)
    return m_sb, m_ch, m_fx


def kernel(x, norm_g, w_in, b_forget, q_norm_ch, k_norm_ch, q_norm_fox, k_norm_fox, rel_bias, w_out):
    bd, u_sb, low = _constants()
    depth, d = norm_g.shape
    weights = [_proj_weights(norm_g[l], w_in[l], q_norm_ch[l], k_norm_ch[l],
                             q_norm_fox[l], k_norm_fox[l], bd) for l in range(depth)]
    w_o = [w_out[l].reshape(N_GROUPS, W_GROUP, d).astype(MM_DTYPE) for l in range(depth)]
    projected = _projection(x, weights[0])
    for l in range(depth):
        mixed = _mixers(projected, b_forget[l], rel_bias[l], u_sb, low)
        if l + 1 < depth:
            x, *projected = _projection(x, weights[l + 1], prev=(*mixed, w_o[l]))
        else:
            x = _out_projection(x, *mixed, w_o[l])
    return x
```

```python
import jax
import jax.numpy as jnp
import numpy as np
from jax import lax
from jax.experimental import pallas as pl
from jax.experimental.pallas import tpu as pltpu

D_MODEL = 1024
HEAD_DIM = 64
H_GROUP = 8
W_GROUP = H_GROUP * HEAD_DIM
N_GROUPS = 3
CHUNK = 64
N_LEFT_CHUNKS = 8
LEFT = N_LEFT_CHUNKS * CHUNK
REL_CLIP = 128
EPS = 1e-6
SCALE = HEAD_DIM ** -0.5

LANES = 128
MXU_DIM = 256
F_PAD = LANES
NEG = -1e30

MM_DTYPE = jnp.bfloat16
F32 = jnp.float32

TM_PROJ = 512
TM_OUT = 1024
TQ_ATT = 4096
FOX_KEYS_PER_TRIP = 1024
BF16_ROWS = 16
FOX_V_ROWS = HEAD_DIM + BF16_ROWS
FOX_SKEW = 5
SB_SKEW = 2
SB_STATIC_DIAGS = 2
SB_DEAD_LOG = -104.0
TK_SB = MXU_DIM
TQ_CHUNK = MXU_DIM
WIN_CHUNK = LEFT + TQ_CHUNK
TS_CHUNK = 4096
CUM_BLK = MXU_DIM
VMEM_LIMIT = 56 * 1024 * 1024


def _dot(a, b):
    return jnp.dot(a, b, preferred_element_type=F32)


def _dot_nt(a, b):
    return lax.dot_general(a, b, (((1,), (1,)), ((), ())), preferred_element_type=F32)


def _split2(x):
    hi = x.astype(MM_DTYPE)
    lo = (x - hi.astype(F32)).astype(MM_DTYPE)
    return hi, lo


def _split3(x):
    hi = x.astype(MM_DTYPE)
    mid, lo = _split2(x - hi.astype(F32))
    return hi, mid, lo


def _head_mask(shape, j, axis):
    idx = lax.broadcasted_iota(jnp.int32, shape, axis)
    return (idx >= j * HEAD_DIM) & (idx < (j + 1) * HEAD_DIM)


def _norm_rows(y, gain_row, bd):
    sq = (y * y).astype(MM_DTYPE)
    parts = []
    for c in range(W_GROUP // MXU_DIM):
        sl = slice(c * MXU_DIM, (c + 1) * MXU_DIM)
        parts.append(_dot(sq[:, sl], bd))
    ssq = jnp.concatenate(parts, axis=1)
    return y * lax.rsqrt(ssq * (1.0 / HEAD_DIM) + EPS) * gain_row


def _norm_cols(yt, gain_col):
    y3 = yt.reshape(H_GROUP, HEAD_DIM, yt.shape[1])
    ssq = jnp.sum(y3 * y3, axis=1, keepdims=True)
    y3 = y3 * lax.rsqrt(ssq * (1.0 / HEAD_DIM) + EPS) * gain_col
    return y3.reshape(yt.shape)


def _proj_kernel(x_ref, *refs):
    _project(x_ref[...], *refs)


def _out_proj_kernel(x_ref, a_ref, b_ref, c_ref, wo_ref, *refs):
    *proj_refs, xo_ref = refs
    tm = x_ref.shape[0]
    parts = []
    for rows in (slice(0, tm // 2), slice(tm // 2, tm)):
        x = x_ref[rows] + (_dot(a_ref[rows], wo_ref[0]) + _dot(b_ref[rows], wo_ref[1])
                           + _dot(c_ref[rows], wo_ref[2]))
        xo_ref[rows] = x
        parts.append(x)
    _project(jnp.concatenate(parts, axis=0), *proj_refs)


def _project(x, ng_ref, wn_ref, wt_ref, wf_ref, grow_ref, gcol_ref, bd_ref,
             qt_ref, k_ref, vt_ref, g_ref, f_ref):
    h = x * lax.rsqrt(jnp.mean(x * x, axis=-1, keepdims=True) + EPS) * ng_ref[...]
    hb = h.astype(MM_DTYPE)
    bd = bd_ref[...]
    dt = qt_ref.dtype
    for grp in range(N_GROUPS):
        qt = _dot_nt(wt_ref[grp], hb)
        k = _dot(hb, wn_ref[grp])
        if grp > 0:
            qt = _norm_cols(qt, gcol_ref[grp - 1])
            k = _norm_rows(k, grow_ref[grp - 1], bd)
        qt_ref[grp] = (qt * SCALE).astype(dt)
        k_ref[grp] = k.astype(dt)
        vt_ref[grp] = _dot_nt(wt_ref[N_GROUPS + grp], hb).astype(dt)
        g_ref[grp] = _dot(hb, wn_ref[N_GROUPS + grp])
    f_ref[...] = _dot(hb, wf_ref[...])


def _projection(x, proj_weights, prev=None):
    b, s, d = x.shape
    tm = min(TM_PROJ, s)
    const = dict(pipeline_mode=pl.Buffered(1))
    row_spec = pl.BlockSpec((None, tm, d), lambda bi, i: (bi, i, 0))
    nat_spec = pl.BlockSpec((N_GROUPS, None, tm, W_GROUP), lambda bi, i: (0, bi, i, 0))
    tr_spec = pl.BlockSpec((N_GROUPS, None, W_GROUP, tm), lambda bi, i: (0, bi, 0, i))
    nat_shape = jax.ShapeDtypeStruct((N_GROUPS, b, s, W_GROUP), MM_DTYPE)
    tr_shape = jax.ShapeDtypeStruct((N_GROUPS, b, W_GROUP, s), MM_DTYPE)
    prev_specs, extra_out_specs, extra_out_shapes = [], [], []
    if prev is not None:
        mix = pl.BlockSpec((None, tm, W_GROUP), lambda bi, i: (bi, i, 0))
        prev_specs = [mix, mix, mix,
                      pl.BlockSpec((N_GROUPS, W_GROUP, d), lambda bi, i: (0, 0, 0), **const)]
        extra_out_specs = [row_spec]
        extra_out_shapes = [jax.ShapeDtypeStruct(x.shape, x.dtype)]
    outs = pl.pallas_call(
        _proj_kernel if prev is None else _out_proj_kernel,
        grid=(b, s // tm),
        in_specs=[row_spec] + prev_specs + [
            pl.BlockSpec((1, d), lambda bi, i: (0, 0)),
            pl.BlockSpec((2 * N_GROUPS, d, W_GROUP), lambda bi, i: (0, 0, 0), **const),
            pl.BlockSpec((2 * N_GROUPS, W_GROUP, d), lambda bi, i: (0, 0, 0), **const),
            pl.BlockSpec((d, F_PAD), lambda bi, i: (0, 0), **const),
            pl.BlockSpec((2, 1, W_GROUP), lambda bi, i: (0, 0, 0)),
            pl.BlockSpec((2, HEAD_DIM, 1), lambda bi, i: (0, 0, 0)),
            pl.BlockSpec((MXU_DIM, MXU_DIM), lambda bi, i: (0, 0)),
        ],
        out_specs=[
            tr_spec, nat_spec, tr_spec,
            pl.BlockSpec((N_GROUPS, None, tm, W_GROUP), lambda bi, i: (0, bi, i, 0)),
            pl.BlockSpec((None, tm, F_PAD), lambda bi, i: (bi, i, 0)),
        ] + extra_out_specs,
        out_shape=[
            tr_shape, nat_shape, tr_shape,
            jax.ShapeDtypeStruct((N_GROUPS, b, s, W_GROUP), F32),
            jax.ShapeDtypeStruct((b, s, F_PAD), F32),
        ] + extra_out_shapes,
        compiler_params=pltpu.CompilerParams(
            dimension_semantics=("arbitrary", "arbitrary"), vmem_limit_bytes=VMEM_LIMIT),
        name="proj" if prev is None else "out_proj_proj",
    )(x, *(prev or ()), *proj_weights)
    return outs if prev is None else (outs[-1], *outs[:-1])


def _cum_kernel(f_ref, b_ref, l_ref, o_ref):
    z = f_ref[...] + b_ref[...]
    lf = jnp.minimum(z, 0.0) - jnp.log1p(jnp.exp(-jnp.abs(z)))
    low = l_ref[...]
    carry = jnp.zeros((1, F_PAD), F32)
    for c in range(lf.shape[0] // CUM_BLK):
        rows = slice(c * CUM_BLK, (c + 1) * CUM_BLK)
        hi, mid, lo = _split3(lf[rows])
        cs = (_dot(low, hi) + _dot(low, mid)) + _dot(low, lo) + carry
        o_ref[rows] = cs
        carry = cs[CUM_BLK - 1:CUM_BLK, :]


def _cum_forget(f, bias_row, low):
    b, s, _ = f.shape
    return pl.pallas_call(
        _cum_kernel,
        grid=(b,),
        in_specs=[
            pl.BlockSpec((None, s, F_PAD), lambda bi: (bi, 0, 0)),
            pl.BlockSpec((1, F_PAD), lambda bi: (0, 0)),
            pl.BlockSpec((CUM_BLK, CUM_BLK), lambda bi: (0, 0)),
        ],
        out_specs=pl.BlockSpec((None, s, F_PAD), lambda bi: (bi, 0, 0)),
        out_shape=jax.ShapeDtypeStruct((b, s, F_PAD), F32),
        compiler_params=pltpu.CompilerParams(dimension_semantics=("arbitrary",)),
        name="cum_forget",
    )(f, bias_row, low)


def _gated_t(g, ot0, ot1):
    o = jnp.concatenate([ot0, ot1], axis=0).T
    return o * (g * jax.nn.sigmoid(g))


def _out_tile(b, p, i):
    return (b, i, p)


def _t_specs(grp, tq, s):
    return [
        pl.BlockSpec((None, None, LANES, tq), lambda b, p, i: (grp, b, p, i)),
        pl.BlockSpec((None, None, s, LANES), lambda b, p, i: (grp, b, 0, p)),
        pl.BlockSpec((None, None, LANES, s), lambda b, p, i: (grp, b, p, 0)),
        pl.BlockSpec((None, None, tq, LANES), lambda b, p, i: (grp, b, i, p)),
    ]


_ATT_PARAMS = pltpu.CompilerParams(
    dimension_semantics=("arbitrary", "arbitrary", "arbitrary"), vmem_limit_bytes=VMEM_LIMIT)


def _neg_abs(x):
    bits = lax.bitcast_convert_type(x, jnp.uint32) | jnp.uint32(0x80000000)
    return lax.bitcast_convert_type(bits, F32)


def _run_skewed(units):
    pending = list(units)
    active = []
    while pending or active:
        if pending:
            active.append(pending.pop(0))
        for g in list(active):
            try:
                next(g)
            except StopIteration:
                active.remove(g)


def _masked_heads_t(qt, other=None):
    if other is None:
        other = jnp.zeros((HEAD_DIM, qt.shape[1]), qt.dtype)
    return [jnp.concatenate([qt[:HEAD_DIM], other], axis=0),
            jnp.concatenate([other, qt[HEAD_DIM:]], axis=0)]


def _sb_kernel(qt_ref, k_ref, vt_ref, g_ref, u_ref, o_ref, acc_ref, carry_ref):
    t = qt_ref.shape[1]
    tk = u_ref.shape[0]
    n_sub = t // tk
    qm = _masked_heads_t(qt_ref[...])
    u = u_ref[...]
    acc_ref[...] = jnp.zeros_like(acc_ref)
    carry_ref[...] = jnp.zeros_like(carry_ref)

    def unit(j, kb, qc, diag):
        cols = slice(qc * tk, (qc + 1) * tk)
        static = isinstance(kb, int)
        k0 = kb * tk if static else pl.multiple_of(jnp.maximum(kb, 0) * tk, tk)
        z = _dot(k_ref[pl.ds(k0, tk), :], qm[j][:, cols])
        for _ in range(SB_SKEW):
            yield
        sp = jnp.maximum(z, 0.0) + jnp.log(1.0 + jnp.exp(_neg_abs(z)))
        if diag:
            strict = (lax.broadcasted_iota(jnp.int32, z.shape, 0)
                      < lax.broadcasted_iota(jnp.int32, z.shape, 1))
            sp = jnp.where(strict, sp, 0.0)
        spb = sp.astype(MM_DTYPE)
        yield
        incl = _dot(u, spb)
        for _ in range(SB_SKEW):
            yield
        w = jnp.exp(z + incl)
        if diag:
            w = jnp.where(strict, w, 0.0)
        wb = w.astype(MM_DTYPE)
        yield
        pv = _dot(vt_ref[j * HEAD_DIM:(j + 1) * HEAD_DIM, pl.ds(k0, tk)], wb)
        carry = carry_ref[j, :, cols]
        scale = jnp.exp(carry)
        step = incl[:1]
        if not static:
            scale = jnp.where(kb >= 0, scale, 0.0)
            step = jnp.where(kb >= 0, step, 0.0)
        acc_ref[j, :, cols] += scale * pv
        carry_ref[j, :, cols] = carry + step

    def diagonal(d, first_qc, masked=False):
        return [unit(j, qc - d, qc, masked) for qc in range(first_qc, n_sub) for j in range(2)]

    def live(d):
        lane = lax.broadcasted_iota(jnp.int32, carry_ref.shape, 2)
        return jnp.max(jnp.where(lane >= d * tk, carry_ref[...], NEG)) >= SB_DEAD_LOG

    def store(rows):
        o_ref[rows] = _gated_t(g_ref[rows], acc_ref[0, :, rows], acc_ref[1, :, rows]).astype(o_ref.dtype)

    def finish(qc):
        for _ in range(2 * SB_SKEW + 3):
            yield
        store(slice(qc * tk, (qc + 1) * tk))

    n_static = min(SB_STATIC_DIAGS, n_sub)
    static_units = []
    for d in range(n_static):
        for qc in range(d, n_sub):
            static_units += [unit(j, qc - d, qc, d == 0) for j in range(2)]
            if d == min(qc, n_static - 1):
                static_units.append(finish(qc))
    _run_skewed(static_units)

    def body(state):
        d, _ = state
        _run_skewed(diagonal(d, n_static))
        return d + 1, live(d + 1)

    if n_static < n_sub:
        d_end, _ = lax.while_loop(lambda st: (st[0] < n_sub) & st[1], body,
                                  (n_static, live(n_static)))
        pl.when(d_end > n_static)(lambda: store(slice(n_static * tk, t)))


def _sb_attention(qt, k, vt, g, u):
    _, b, _, s = qt.shape
    t = s
    tk = u.shape[0]
    return pl.pallas_call(
        _sb_kernel,
        grid=(b, W_GROUP // LANES, s // t),
        in_specs=_t_specs(0, t, s) + [pl.BlockSpec((tk, tk), lambda b_, p, i: (0, 0))],
        out_specs=pl.BlockSpec((None, t, LANES), _out_tile),
        out_shape=jax.ShapeDtypeStruct((b, s, W_GROUP), MM_DTYPE),
        scratch_shapes=[pltpu.VMEM((2, HEAD_DIM, t), F32), pltpu.VMEM((2, 1, t), F32)],
        compiler_params=_ATT_PARAMS,
        name="sb_attention",
    )(qt, k, vt, g, u)


def _chunk_kernel(qt_ref, k_ref, vt_ref, g_ref, bias_ref, o_ref, kpad, vaug, ot_ref):
    i = pl.program_id(2)
    ts = qt_ref.shape[1]
    win, tq = bias_ref.shape[1:]
    s = k_ref.shape[0]
    n_ones = vaug.shape[1] - HEAD_DIM

    @pl.when(i == 0)
    def _():
        kpad[:LEFT] = jnp.zeros((LEFT, LANES), kpad.dtype)
        kpad[LEFT:] = k_ref[...]
        vt = vt_ref[...].astype(F32)
        for j in range(2):
            vaug[j, :, :LEFT] = jnp.zeros((vaug.shape[1], LEFT), vaug.dtype)
            vaug[j, :, LEFT:] = jnp.concatenate(
                [vt[j * HEAD_DIM:(j + 1) * HEAD_DIM], jnp.ones((n_ones, s), F32)],
                axis=0).astype(vaug.dtype)

    qm = _masked_heads_t(qt_ref[...])

    def unit(j, r, first_step):
        cols = slice(r * tq, (r + 1) * tq)
        r0 = pl.multiple_of(i * ts + r * tq, tq)
        zs = []
        for kb in range(win // tq):
            k0 = pl.multiple_of(r0 + kb * tq, tq)
            z = _dot(kpad[pl.ds(k0, tq), :], qm[j][:, cols]) + bias_ref[j, kb * tq:(kb + 1) * tq]
            lo_key = LEFT - r * tq - kb * tq
            if first_step and lo_key > 0:
                key = lax.broadcasted_iota(jnp.int32, z.shape, 0)
                z = jnp.where(key >= lo_key, z, NEG)
            zs.append(z)
            yield
        m = zs[0].max(axis=0, keepdims=True)
        for z in zs[1:]:
            m = jnp.maximum(m, z.max(axis=0, keepdims=True))
        ps = []
        for z in zs:
            ps.append(jnp.exp(z - m).astype(MM_DTYPE))
            yield
        pv = None
        for kb, p in enumerate(ps):
            k0 = pl.multiple_of(r0 + kb * tq, tq)
            term = _dot(vaug[j, :, pl.ds(k0, tq)], p)
            pv = term if pv is None else pv + term
        ot_ref[j * HEAD_DIM:(j + 1) * HEAD_DIM, cols] = pv[:HEAD_DIM] / pv[HEAD_DIM:HEAD_DIM + 1]

    n_stages = 2 * (win // tq) + 1

    def finish(r):
        for _ in range(n_stages):
            yield
        cols = slice(r * tq, (r + 1) * tq)
        o_ref[cols] = _gated_t(g_ref[cols], ot_ref[:HEAD_DIM, cols],
                               ot_ref[HEAD_DIM:, cols]).astype(o_ref.dtype)

    def run(first_step):
        _run_skewed([un for r in range(ts // tq)
                     for un in (unit(0, r, first_step), unit(1, r, first_step), finish(r))])

    if s == ts:
        run(True)
    else:
        pl.when(i == 0)(lambda: run(True))
        pl.when(i > 0)(lambda: run(False))


def _chunk_attention(qt, k, vt, g, bias):
    _, b, _, s = qt.shape
    ts = min(TS_CHUNK, s)
    return pl.pallas_call(
        _chunk_kernel,
        grid=(b, W_GROUP // LANES, s // ts),
        in_specs=_t_specs(1, ts, s) + [
            pl.BlockSpec((2, WIN_CHUNK, TQ_CHUNK), lambda b_, p, i: (p, 0, 0))],
        out_specs=pl.BlockSpec((None, ts, LANES), _out_tile),
        out_shape=jax.ShapeDtypeStruct((b, s, W_GROUP), MM_DTYPE),
        scratch_shapes=[pltpu.VMEM((LEFT + s, LANES), MM_DTYPE),
                        pltpu.VMEM((2, FOX_V_ROWS, LEFT + s), MM_DTYPE),
                        pltpu.VMEM((LANES, ts), F32)],
        compiler_params=_ATT_PARAMS,
        name="chunk_attention",
    )(qt, k, vt, g, bias)


N_DIAG = WIN_CHUNK + TQ_CHUNK


def _bias_kernel(v_ref, o_ref):
    x = jnp.broadcast_to(v_ref[...], (WIN_CHUNK, N_DIAG))
    y = pltpu.roll(x, 1, 1, stride=1, stride_axis=0)
    t = y[:, WIN_CHUNK:]
    c = lax.broadcasted_iota(jnp.int32, t.shape, 0)
    r = lax.broadcasted_iota(jnp.int32, t.shape, 1)
    band = c - (r - (r & (CHUNK - 1)))
    o_ref[...] = jnp.where((band >= 0) & (band < LEFT + CHUNK), t, NEG)


def _chunk_bias_table(rel_bias):
    kk = np.arange(N_DIAG - 1)
    rel = kk - (WIN_CHUNK - 1) + LEFT
    vec = rel_bias[:, np.clip(rel, -REL_CLIP, REL_CLIP) + REL_CLIP].astype(F32)
    vec = jnp.pad(vec, ((0, 0), (0, 1)))[:, None, :]
    h = vec.shape[0]
    return pl.pallas_call(
        _bias_kernel,
        grid=(h,),
        in_specs=[pl.BlockSpec((None, 1, N_DIAG), lambda i: (i, 0, 0))],
        out_specs=pl.BlockSpec((None, WIN_CHUNK, TQ_CHUNK), lambda i: (i, 0, 0)),
        out_shape=jax.ShapeDtypeStruct((h, WIN_CHUNK, TQ_CHUNK), F32),
        compiler_params=pltpu.CompilerParams(dimension_semantics=("arbitrary",)),
        name="chunk_bias",
    )(vec)


def _fox_kernel(qt_ref, k_ref, vt_ref, g_ref, c_ref, o_ref, kaug, vaug, m_ref, acc_ref):
    i = pl.program_id(2)
    p = pl.program_id(1)
    t = qt_ref.shape[1]
    aug0 = [HEAD_DIM * (1 - j) for j in range(2)]
    n_ones = vaug.shape[1] - HEAD_DIM

    @pl.when(i == 0)
    def _():
        k = k_ref[...].astype(F32)
        vt = vt_ref[...].astype(F32)
        pieces = jnp.concatenate(_split3(-c_ref[...]), axis=1)
        row = lax.broadcasted_iota(jnp.int32, (3 * LANES, 2 * LANES), 0)
        col = lax.broadcasted_iota(jnp.int32, (3 * LANES, 2 * LANES), 1)
        place = None
        for j in range(2):
            for c in range(3):
                hit = (row == c * LANES + 2 * p + j) & (col == j * LANES + aug0[j] + c)
                place = hit if place is None else place | hit
        aug = _dot(pieces, jnp.where(place, 1.0, 0.0).astype(MM_DTYPE))
        for j in range(2):
            kaug[j] = jnp.where(_head_mask(k.shape, j, 1), k,
                                aug[:, j * LANES:(j + 1) * LANES]).astype(kaug.dtype)
            vaug[j] = jnp.concatenate(
                [vt[j * HEAD_DIM:(j + 1) * HEAD_DIM], jnp.ones((n_ones, vt.shape[1]), F32)],
                axis=0).astype(vaug.dtype)

    row = lax.broadcasted_iota(jnp.int32, (HEAD_DIM, LANES), 0)
    ones3 = jnp.tile(jnp.where(row < 3, 1.0, 0.0).astype(MM_DTYPE), (1, t // LANES))
    qa = _masked_heads_t(qt_ref[...], ones3)
    m_ref[...] = jnp.full(m_ref.shape, NEG, F32)
    acc_ref[...] = jnp.zeros_like(acc_ref)

    tk = TK_SB
    n_sub = t // tk

    def unit(j, k0, qc, diag):
        cols = slice(qc * tk, (qc + 1) * tk)
        z = _dot(kaug[j, pl.ds(k0, tk), :], qa[j][:, cols])
        for _ in range(FOX_SKEW):
            yield
        if diag:
            causal = (lax.broadcasted_iota(jnp.int32, z.shape, 0)
                      <= lax.broadcasted_iota(jnp.int32, z.shape, 1))
            z = jnp.where(causal, z, NEG)
        m = m_ref[j, :, cols]
        m_new = jnp.maximum(m, jnp.max(z, axis=0, keepdims=True))
        pr = jnp.exp(z - m_new).astype(MM_DTYPE)
        alpha = jnp.exp(m - m_new)
        m_ref[j, :, cols] = m_new
        yield
        pv = _dot(vaug[j, :, pl.ds(k0, tk)], pr)
        acc_ref[j, :, cols] = alpha * acc_ref[j, :, cols] + pv

    def units(base, n_blocks, diag):
        out = []
        for c in range(n_blocks):
            for qc in range(c if diag else 0, n_sub):
                for j in range(2):
                    out.append(unit(j, pl.multiple_of(base + c * tk, tk), qc, diag and qc == c))
        return out

    trip = min(FOX_KEYS_PER_TRIP, t)

    def body(kj, _):
        _run_skewed(units(kj * trip, trip // tk, False))
        return 0

    if kaug.shape[1] > t:
        lax.fori_loop(0, i * (t // trip), body, 0)
    _run_skewed(units(i * t, n_sub, True))
    outs = [acc_ref[j, :HEAD_DIM] / acc_ref[j, HEAD_DIM:HEAD_DIM + 1] for j in range(2)]
    o_ref[...] = _gated_t(g_ref[...], outs[0], outs[1]).astype(o_ref.dtype)


def _fox_attention(qt, k, vt, g, cum):
    _, b, _, s = qt.shape
    t = min(TQ_ATT, s)
    return pl.pallas_call(
        _fox_kernel,
        grid=(b, W_GROUP // LANES, s // t),
        in_specs=_t_specs(2, t, s) + [
            pl.BlockSpec((None, s, F_PAD), lambda b_, p, i: (b_, 0, 0))],
        out_specs=pl.BlockSpec((None, t, LANES), _out_tile),
        out_shape=jax.ShapeDtypeStruct((b, s, W_GROUP), MM_DTYPE),
        scratch_shapes=[pltpu.VMEM((2, s, LANES), MM_DTYPE),
                        pltpu.VMEM((2, FOX_V_ROWS, s), MM_DTYPE),
                        pltpu.VMEM((2, 1, t), F32), pltpu.VMEM((2, FOX_V_ROWS, t), F32)],
        compiler_params=_ATT_PARAMS,
        name="fox_attention",
    )(qt, k, vt, g, cum)


def _out_kernel(x_ref, a_ref, b_ref, c_ref, w_ref, o_ref):
    acc = _dot(a_ref[...], w_ref[0]) + _dot(b_ref[...], w_ref[1]) + _dot(c_ref[...], w_ref[2])
    o_ref[...] = x_ref[...] + acc


def _out_projection(x, ma, mb, mc, w):
    b, s, d = x.shape
    tm = min(TM_OUT, s)
    mix = pl.BlockSpec((None, tm, W_GROUP), lambda bi, i: (bi, i, 0))
    return pl.pallas_call(
        _out_kernel,
        grid=(b, s // tm),
        in_specs=[pl.BlockSpec((None, tm, d), lambda bi, i: (bi, i, 0)), mix, mix, mix,
                  pl.BlockSpec((N_GROUPS, W_GROUP, d), lambda bi, i: (0, 0, 0))],
        out_specs=pl.BlockSpec((None, tm, d), lambda bi, i: (bi, i, 0)),
        out_shape=jax.ShapeDtypeStruct(x.shape, x.dtype),
        compiler_params=pltpu.CompilerParams(
            dimension_semantics=("arbitrary", "arbitrary"), vmem_limit_bytes=VMEM_LIMIT),
        name="out_proj",
    )(x, ma, mb, mc, w)


def _constants():
    r = np.arange(MXU_DIM)
    bd = (r[:, None] // HEAD_DIM == r[None, :] // HEAD_DIM).astype(np.float32)
    r = np.arange(TK_SB)
    u_sb = -(r[None, :] >= r[:, None]).astype(np.float32)
    r = np.arange(CUM_BLK)
    low = (r[None, :] <= r[:, None]).astype(np.float32)
    return (jnp.asarray(bd, MM_DTYPE), jnp.asarray(u_sb, MM_DTYPE), jnp.asarray(low, MM_DTYPE))


def _proj_weights(ng, w_in, qn_ch, kn_ch, qn_fox, kn_fox, bd):
    d = w_in.shape[0]
    w_in = w_in.astype(MM_DTYPE)
    w4 = w_in[:, :N_GROUPS * 4 * W_GROUP].reshape(d, N_GROUPS, 4, W_GROUP)
    w_nat = jnp.transpose(w4[:, :, (1, 3), :], (2, 1, 0, 3)).reshape(2 * N_GROUPS, d, W_GROUP)
    w_tr = jnp.transpose(w4[:, :, (0, 2), :], (2, 1, 3, 0)).reshape(2 * N_GROUPS, W_GROUP, d)
    w_f = jnp.pad(w_in[:, N_GROUPS * 4 * W_GROUP:], ((0, 0), (0, F_PAD - H_GROUP)))
    grow = jnp.stack([jnp.tile(kn_ch, H_GROUP), jnp.tile(kn_fox, H_GROUP)])[:, None, :].astype(F32)
    gcol = jnp.stack([qn_ch, qn_fox])[:, :, None].astype(F32)
    return (ng[None, :], w_nat, w_tr, w_f, grow, gcol, bd)


def _mixers(projected, b_forget, rel_bias, u_sb, low):
    qt, k, vt, g, f = projected
    bias_row = jnp.pad(b_forget.astype(F32), (0, F_PAD - H_GROUP))[None, :]
    cum = _cum_forget(f, bias_row, low)
    m_sb = _sb_attention(qt, k, vt, g, u_sb)
    m_ch = _chunk_attention(qt, k, vt, g, _chunk_bias_table(rel_bias))
    m_fx = _fox_attention(qt, k, vt, g, cum)
    return m_sb, m_ch, m_fx


def kernel(x, norm_g, w_in, b_forget, q_norm_ch, k_norm_ch, q_norm_fox, k_norm_fox, rel_bias, w_out):
    bd, u_sb, low = _constants()
    depth, d = norm_g.shape
    weights = [_proj_weights(norm_g[l], w_in[l], q_norm_ch[l], k_norm_ch[l],
                             q_norm_fox[l], k_norm_fox[l], bd) for l in range(depth)]
    w_o = [w_out[l].reshape(N_GROUPS, W_GROUP, d).astype(MM_DTYPE) for l in range(depth)]
    projected = _projection(x, weights[0])
    for l in range(depth):
        mixed = _mixers(projected, b_forget[l], rel_bias[l], u_sb, low)
        if l + 1 < depth:
            x, *projected = _projection(x, weights[l + 1], prev=(*mixed, w_o[l]))
        else:
            x = _out_projection(x, *mixed, w_o[l])
    return x
```

```python
import functools

import jax
import jax.numpy as jnp
import numpy as np
from jax import lax
from jax.experimental import pallas as pl
from jax.experimental.pallas import tpu as pltpu

D_MODEL = 1024
HEAD_DIM = 64
H_GROUP = 8
W_GROUP = H_GROUP * HEAD_DIM
N_GROUPS = 3
CHUNK = 64
N_LEFT_CHUNKS = 8
LEFT = N_LEFT_CHUNKS * CHUNK
REL_CLIP = 128
EPS = 1e-6
SCALE = HEAD_DIM ** -0.5

LANES = 128
MXU_DIM = 256
F_PAD = LANES
NEG = -1e30

MM_DTYPE = jnp.bfloat16
F32 = jnp.float32

TM_PROJ = 512
TM_OUT = 1024
TQ_ATT = 4096
FOX_KEYS_PER_TRIP = 1024
BF16_ROWS = 16
FOX_V_ROWS = HEAD_DIM + BF16_ROWS
FOX_SKEW = 5
SB_SKEW = 2
SB_STATIC_DIAGS = 2
SB_DEAD_LOG = -104.0
TK_SB = MXU_DIM
TQ_CHUNK = MXU_DIM
WIN_CHUNK = LEFT + TQ_CHUNK
TS_CHUNK = 4096
CUM_BLK = MXU_DIM
VMEM_LIMIT = 56 * 1024 * 1024


def _dot(a, b):
    return jnp.dot(a, b, preferred_element_type=F32)


def _dot_nt(a, b):
    return lax.dot_general(a, b, (((1,), (1,)), ((), ())), preferred_element_type=F32)


def _split2(x):
    hi = x.astype(MM_DTYPE)
    lo = (x - hi.astype(F32)).astype(MM_DTYPE)
    return hi, lo


def _split3(x):
    hi = x.astype(MM_DTYPE)
    mid, lo = _split2(x - hi.astype(F32))
    return hi, mid, lo


def _head_mask(shape, j, axis):
    idx = lax.broadcasted_iota(jnp.int32, shape, axis)
    return (idx >= j * HEAD_DIM) & (idx < (j + 1) * HEAD_DIM)


def _norm_rows(y, gain_row, bd):
    sq = (y * y).astype(MM_DTYPE)
    parts = []
    for c in range(W_GROUP // MXU_DIM):
        sl = slice(c * MXU_DIM, (c + 1) * MXU_DIM)
        parts.append(_dot(sq[:, sl], bd))
    ssq = jnp.concatenate(parts, axis=1)
    return y * lax.rsqrt(ssq * (1.0 / HEAD_DIM) + EPS) * gain_row


def _norm_cols(yt, gain_col):
    y3 = yt.reshape(H_GROUP, HEAD_DIM, yt.shape[1])
    ssq = jnp.sum(y3 * y3, axis=1, keepdims=True)
    y3 = y3 * lax.rsqrt(ssq * (1.0 / HEAD_DIM) + EPS) * gain_col
    return y3.reshape(yt.shape)


def _proj_kernel(x_ref, *refs):
    _project(x_ref[...], *refs)


def _out_proj_kernel(x_ref, a_ref, b_ref, c_ref, wo_ref, *refs):
    *proj_refs, xo_ref = refs
    tm = x_ref.shape[0]
    parts = []
    for rows in (slice(0, tm // 2), slice(tm // 2, tm)):
        x = x_ref[rows] + (_dot(a_ref[rows], wo_ref[0]) + _dot(b_ref[rows], wo_ref[1])
                           + _dot(c_ref[rows], wo_ref[2]))
        xo_ref[rows] = x
        parts.append(x)
    _project(jnp.concatenate(parts, axis=0), *proj_refs)


def _project(x, ng_ref, wn_ref, wt_ref, wf_ref, grow_ref, gcol_ref, bd_ref,
             qt_ref, k_ref, vt_ref, g_ref, f_ref):
    h = x * lax.rsqrt(jnp.mean(x * x, axis=-1, keepdims=True) + EPS) * ng_ref[...]
    hb = h.astype(MM_DTYPE)
    bd = bd_ref[...]
    dt = qt_ref.dtype
    for grp in range(N_GROUPS):
        qt = _dot_nt(wt_ref[grp], hb)
        k = _dot(hb, wn_ref[grp])
        if grp > 0:
            qt = _norm_cols(qt, gcol_ref[grp - 1])
            k = _norm_rows(k, grow_ref[grp - 1], bd)
        qt_ref[grp] = (qt * SCALE).astype(dt)
        k_ref[grp] = k.astype(dt)
        vt_ref[grp] = _dot_nt(wt_ref[N_GROUPS + grp], hb).astype(dt)
        g_ref[grp] = _dot(hb, wn_ref[N_GROUPS + grp])
    f_ref[...] = _dot(hb, wf_ref[...])


def _projection(x, proj_weights, layer, prev=None):
    b, s, d = x.shape
    tm = min(TM_PROJ, s)
    const = dict(pipeline_mode=pl.Buffered(1))
    row_spec = pl.BlockSpec((None, tm, d), lambda bi, i: (bi, i, 0))
    nat_spec = pl.BlockSpec((N_GROUPS, None, tm, W_GROUP), lambda bi, i: (0, bi, i, 0))
    tr_spec = pl.BlockSpec((N_GROUPS, None, W_GROUP, tm), lambda bi, i: (0, bi, 0, i))
    nat_shape = jax.ShapeDtypeStruct((N_GROUPS, b, s, W_GROUP), MM_DTYPE)
    tr_shape = jax.ShapeDtypeStruct((N_GROUPS, b, W_GROUP, s), MM_DTYPE)
    prev_specs, extra_out_specs, extra_out_shapes = [], [], []
    if prev is not None:
        mix = pl.BlockSpec((None, tm, W_GROUP), lambda bi, i: (bi, i, 0))
        prev_specs = [mix, mix, mix,
                      pl.BlockSpec((N_GROUPS, W_GROUP, d), lambda bi, i: (0, 0, 0), **const)]
        extra_out_specs = [row_spec]
        extra_out_shapes = [jax.ShapeDtypeStruct(x.shape, x.dtype)]
    outs = pl.pallas_call(
        _proj_kernel if prev is None else _out_proj_kernel,
        grid=(b, s // tm),
        in_specs=[row_spec] + prev_specs + [
            pl.BlockSpec((1, d), lambda bi, i: (0, 0)),
            pl.BlockSpec((None, 2 * N_GROUPS, d, W_GROUP), lambda bi, i: (layer, 0, 0, 0), **const),
            pl.BlockSpec((None, 2 * N_GROUPS, W_GROUP, d), lambda bi, i: (layer, 0, 0, 0), **const),
            pl.BlockSpec((d, F_PAD), lambda bi, i: (0, 0), **const),
            pl.BlockSpec((2, 1, W_GROUP), lambda bi, i: (0, 0, 0)),
            pl.BlockSpec((2, HEAD_DIM, 1), lambda bi, i: (0, 0, 0)),
            pl.BlockSpec((MXU_DIM, MXU_DIM), lambda bi, i: (0, 0)),
        ],
        out_specs=[
            tr_spec, nat_spec, tr_spec,
            pl.BlockSpec((N_GROUPS, None, tm, W_GROUP), lambda bi, i: (0, bi, i, 0)),
            pl.BlockSpec((None, tm, F_PAD), lambda bi, i: (bi, i, 0)),
        ] + extra_out_specs,
        out_shape=[
            tr_shape, nat_shape, tr_shape,
            jax.ShapeDtypeStruct((N_GROUPS, b, s, W_GROUP), F32),
            jax.ShapeDtypeStruct((b, s, F_PAD), F32),
        ] + extra_out_shapes,
        compiler_params=pltpu.CompilerParams(
            dimension_semantics=("arbitrary", "arbitrary"), vmem_limit_bytes=VMEM_LIMIT),
        name="proj" if prev is None else "out_proj_proj",
    )(x, *(prev or ()), *proj_weights)
    return outs if prev is None else (outs[-1], *outs[:-1])


def _cum_kernel(f_ref, b_ref, l_ref, o_ref):
    z = f_ref[...] + b_ref[...]
    lf = jnp.minimum(z, 0.0) - jnp.log1p(jnp.exp(-jnp.abs(z)))
    low = l_ref[...]
    carry = jnp.zeros((1, F_PAD), F32)
    for c in range(lf.shape[0] // CUM_BLK):
        rows = slice(c * CUM_BLK, (c + 1) * CUM_BLK)
        hi, mid, lo = _split3(lf[rows])
        cs = (_dot(low, hi) + _dot(low, mid)) + _dot(low, lo) + carry
        o_ref[rows] = cs
        carry = cs[CUM_BLK - 1:CUM_BLK, :]


def _cum_forget(f, bias_row, low):
    b, s, _ = f.shape
    return pl.pallas_call(
        _cum_kernel,
        grid=(b,),
        in_specs=[
            pl.BlockSpec((None, s, F_PAD), lambda bi: (bi, 0, 0)),
            pl.BlockSpec((1, F_PAD), lambda bi: (0, 0)),
            pl.BlockSpec((CUM_BLK, CUM_BLK), lambda bi: (0, 0)),
        ],
        out_specs=pl.BlockSpec((None, s, F_PAD), lambda bi: (bi, 0, 0)),
        out_shape=jax.ShapeDtypeStruct((b, s, F_PAD), F32),
        compiler_params=pltpu.CompilerParams(dimension_semantics=("arbitrary",)),
        name="cum_forget",
    )(f, bias_row, low)


def _gated_t(g, ot0, ot1):
    o = jnp.concatenate([ot0, ot1], axis=0).T
    return o * (g * jax.nn.sigmoid(g))


def _out_tile(b, p, i):
    return (b, i, p)


def _t_specs(grp, tq, s):
    return [
        pl.BlockSpec((None, None, LANES, tq), lambda b, p, i: (grp, b, p, i)),
        pl.BlockSpec((None, None, s, LANES), lambda b, p, i: (grp, b, 0, p)),
        pl.BlockSpec((None, None, LANES, s), lambda b, p, i: (grp, b, p, 0)),
        pl.BlockSpec((None, None, tq, LANES), lambda b, p, i: (grp, b, i, p)),
    ]


_ATT_PARAMS = pltpu.CompilerParams(
    dimension_semantics=("arbitrary", "arbitrary", "arbitrary"), vmem_limit_bytes=VMEM_LIMIT)


def _neg_abs(x):
    bits = lax.bitcast_convert_type(x, jnp.uint32) | jnp.uint32(0x80000000)
    return lax.bitcast_convert_type(bits, F32)


def _run_skewed(units):
    pending = list(units)
    active = []
    while pending or active:
        if pending:
            active.append(pending.pop(0))
        for g in list(active):
            try:
                next(g)
            except StopIteration:
                active.remove(g)


def _masked_heads_t(qt, other=None):
    if other is None:
        other = jnp.zeros((HEAD_DIM, qt.shape[1]), qt.dtype)
    return [jnp.concatenate([qt[:HEAD_DIM], other], axis=0),
            jnp.concatenate([other, qt[HEAD_DIM:]], axis=0)]


def _sb_kernel(qt_ref, k_ref, vt_ref, g_ref, u_ref, o_ref, acc_ref, carry_ref):
    t = qt_ref.shape[1]
    tk = u_ref.shape[0]
    n_sub = t // tk
    qm = _masked_heads_t(qt_ref[...])
    u = u_ref[...]
    acc_ref[...] = jnp.zeros_like(acc_ref)
    carry_ref[...] = jnp.zeros_like(carry_ref)

    def unit(j, kb, qc, diag):
        cols = slice(qc * tk, (qc + 1) * tk)
        static = isinstance(kb, int)
        k0 = kb * tk if static else pl.multiple_of(jnp.maximum(kb, 0) * tk, tk)
        z = _dot(k_ref[pl.ds(k0, tk), :], qm[j][:, cols])
        for _ in range(SB_SKEW):
            yield
        sp = jnp.maximum(z, 0.0) + jnp.log(1.0 + jnp.exp(_neg_abs(z)))
        if diag:
            strict = (lax.broadcasted_iota(jnp.int32, z.shape, 0)
                      < lax.broadcasted_iota(jnp.int32, z.shape, 1))
            sp = jnp.where(strict, sp, 0.0)
        spb = sp.astype(MM_DTYPE)
        yield
        incl = _dot(u, spb)
        for _ in range(SB_SKEW):
            yield
        w = jnp.exp(z + incl)
        if diag:
            w = jnp.where(strict, w, 0.0)
        wb = w.astype(MM_DTYPE)
        yield
        pv = _dot(vt_ref[j * HEAD_DIM:(j + 1) * HEAD_DIM, pl.ds(k0, tk)], wb)
        carry = carry_ref[j, :, cols]
        scale = jnp.exp(carry)
        step = incl[:1]
        if not static:
            scale = jnp.where(kb >= 0, scale, 0.0)
            step = jnp.where(kb >= 0, step, 0.0)
        acc_ref[j, :, cols] += scale * pv
        carry_ref[j, :, cols] = carry + step

    def diagonal(d, first_qc, masked=False):
        return [unit(j, qc - d, qc, masked) for qc in range(first_qc, n_sub) for j in range(2)]

    def live(d):
        lane = lax.broadcasted_iota(jnp.int32, carry_ref.shape, 2)
        return jnp.max(jnp.where(lane >= d * tk, carry_ref[...], NEG)) >= SB_DEAD_LOG

    def store(rows):
        o_ref[rows] = _gated_t(g_ref[rows], acc_ref[0, :, rows], acc_ref[1, :, rows]).astype(o_ref.dtype)

    def finish(qc):
        for _ in range(2 * SB_SKEW + 3):
            yield
        store(slice(qc * tk, (qc + 1) * tk))

    n_static = min(SB_STATIC_DIAGS, n_sub)
    static_units = []
    for d in range(n_static):
        for qc in range(d, n_sub):
            static_units += [unit(j, qc - d, qc, d == 0) for j in range(2)]
            if d == min(qc, n_static - 1):
                static_units.append(finish(qc))
    _run_skewed(static_units)

    def body(state):
        d, _ = state
        _run_skewed(diagonal(d, n_static))
        return d + 1, live(d + 1)

    if n_static < n_sub:
        d_end, _ = lax.while_loop(lambda st: (st[0] < n_sub) & st[1], body,
                                  (n_static, live(n_static)))
        pl.when(d_end > n_static)(lambda: store(slice(n_static * tk, t)))


def _sb_attention(qt, k, vt, g, u):
    _, b, _, s = qt.shape
    t = s
    tk = u.shape[0]
    return pl.pallas_call(
        _sb_kernel,
        grid=(b, W_GROUP // LANES, s // t),
        in_specs=_t_specs(0, t, s) + [pl.BlockSpec((tk, tk), lambda b_, p, i: (0, 0))],
        out_specs=pl.BlockSpec((None, t, LANES), _out_tile),
        out_shape=jax.ShapeDtypeStruct((b, s, W_GROUP), MM_DTYPE),
        scratch_shapes=[pltpu.VMEM((2, HEAD_DIM, t), F32), pltpu.VMEM((2, 1, t), F32)],
        compiler_params=_ATT_PARAMS,
        name="sb_attention",
    )(qt, k, vt, g, u)


def _chunk_kernel(qt_ref, k_ref, vt_ref, g_ref, bias_ref, o_ref, kpad, vaug, ot_ref):
    i = pl.program_id(2)
    ts = qt_ref.shape[1]
    win, tq = bias_ref.shape[1:]
    s = k_ref.shape[0]
    n_ones = vaug.shape[1] - HEAD_DIM

    @pl.when(i == 0)
    def _():
        kpad[:LEFT] = jnp.zeros((LEFT, LANES), kpad.dtype)
        kpad[LEFT:] = k_ref[...]
        vt = vt_ref[...].astype(F32)
        for j in range(2):
            vaug[j, :, :LEFT] = jnp.zeros((vaug.shape[1], LEFT), vaug.dtype)
            vaug[j, :, LEFT:] = jnp.concatenate(
                [vt[j * HEAD_DIM:(j + 1) * HEAD_DIM], jnp.ones((n_ones, s), F32)],
                axis=0).astype(vaug.dtype)

    qm = _masked_heads_t(qt_ref[...])

    def unit(j, r, first_step):
        cols = slice(r * tq, (r + 1) * tq)
        r0 = pl.multiple_of(i * ts + r * tq, tq)
        zs = []
        for kb in range(win // tq):
            k0 = pl.multiple_of(r0 + kb * tq, tq)
            z = _dot(kpad[pl.ds(k0, tq), :], qm[j][:, cols]) + bias_ref[j, kb * tq:(kb + 1) * tq]
            lo_key = LEFT - r * tq - kb * tq
            if first_step and lo_key > 0:
                key = lax.broadcasted_iota(jnp.int32, z.shape, 0)
                z = jnp.where(key >= lo_key, z, NEG)
            zs.append(z)
            yield
        m = zs[0].max(axis=0, keepdims=True)
        for z in zs[1:]:
            m = jnp.maximum(m, z.max(axis=0, keepdims=True))
        ps = []
        for z in zs:
            ps.append(jnp.exp(z - m).astype(MM_DTYPE))
            yield
        pv = None
        for kb, p in enumerate(ps):
            k0 = pl.multiple_of(r0 + kb * tq, tq)
            term = _dot(vaug[j, :, pl.ds(k0, tq)], p)
            pv = term if pv is None else pv + term
        ot_ref[j * HEAD_DIM:(j + 1) * HEAD_DIM, cols] = pv[:HEAD_DIM] / pv[HEAD_DIM:HEAD_DIM + 1]

    n_stages = 2 * (win // tq) + 1

    def finish(r):
        for _ in range(n_stages):
            yield
        cols = slice(r * tq, (r + 1) * tq)
        o_ref[cols] = _gated_t(g_ref[cols], ot_ref[:HEAD_DIM, cols],
                               ot_ref[HEAD_DIM:, cols]).astype(o_ref.dtype)

    def run(first_step):
        _run_skewed([un for r in range(ts // tq)
                     for un in (unit(0, r, first_step), unit(1, r, first_step), finish(r))])

    if s == ts:
        run(True)
    else:
        pl.when(i == 0)(lambda: run(True))
        pl.when(i > 0)(lambda: run(False))


def _chunk_attention(qt, k, vt, g, bias):
    _, b, _, s = qt.shape
    ts = min(TS_CHUNK, s)
    return pl.pallas_call(
        _chunk_kernel,
        grid=(b, W_GROUP // LANES, s // ts),
        in_specs=_t_specs(1, ts, s) + [
            pl.BlockSpec((2, WIN_CHUNK, TQ_CHUNK), lambda b_, p, i: (p, 0, 0))],
        out_specs=pl.BlockSpec((None, ts, LANES), _out_tile),
        out_shape=jax.ShapeDtypeStruct((b, s, W_GROUP), MM_DTYPE),
        scratch_shapes=[pltpu.VMEM((LEFT + s, LANES), MM_DTYPE),
                        pltpu.VMEM((2, FOX_V_ROWS, LEFT + s), MM_DTYPE),
                        pltpu.VMEM((LANES, ts), F32)],
        compiler_params=_ATT_PARAMS,
        name="chunk_attention",
    )(qt, k, vt, g, bias)


N_DIAG = WIN_CHUNK + TQ_CHUNK


def _bias_kernel(v_ref, o_ref):
    x = jnp.broadcast_to(v_ref[...], (WIN_CHUNK, N_DIAG))
    y = pltpu.roll(x, 1, 1, stride=1, stride_axis=0)
    t = y[:, WIN_CHUNK:]
    c = lax.broadcasted_iota(jnp.int32, t.shape, 0)
    r = lax.broadcasted_iota(jnp.int32, t.shape, 1)
    band = c - (r - (r & (CHUNK - 1)))
    o_ref[...] = jnp.where((band >= 0) & (band < LEFT + CHUNK), t, NEG)


def _chunk_bias_table(rel_bias):
    kk = np.arange(N_DIAG - 1)
    rel = kk - (WIN_CHUNK - 1) + LEFT
    vec = rel_bias[:, np.clip(rel, -REL_CLIP, REL_CLIP) + REL_CLIP].astype(F32)
    vec = jnp.pad(vec, ((0, 0), (0, 1)))[:, None, :]
    h = vec.shape[0]
    return pl.pallas_call(
        _bias_kernel,
        grid=(h,),
        in_specs=[pl.BlockSpec((None, 1, N_DIAG), lambda i: (i, 0, 0))],
        out_specs=pl.BlockSpec((None, WIN_CHUNK, TQ_CHUNK), lambda i: (i, 0, 0)),
        out_shape=jax.ShapeDtypeStruct((h, WIN_CHUNK, TQ_CHUNK), F32),
        compiler_params=pltpu.CompilerParams(dimension_semantics=("arbitrary",)),
        name="chunk_bias",
    )(vec)


def _fox_kernel(qt_ref, k_ref, vt_ref, g_ref, c_ref, o_ref, kaug, vaug, m_ref, acc_ref):
    i = pl.program_id(2)
    p = pl.program_id(1)
    t = qt_ref.shape[1]
    aug0 = [HEAD_DIM * (1 - j) for j in range(2)]
    n_ones = vaug.shape[1] - HEAD_DIM

    @pl.when(i == 0)
    def _():
        k = k_ref[...].astype(F32)
        vt = vt_ref[...].astype(F32)
        pieces = jnp.concatenate(_split3(-c_ref[...]), axis=1)
        row = lax.broadcasted_iota(jnp.int32, (3 * LANES, 2 * LANES), 0)
        col = lax.broadcasted_iota(jnp.int32, (3 * LANES, 2 * LANES), 1)
        place = None
        for j in range(2):
            for c in range(3):
                hit = (row == c * LANES + 2 * p + j) & (col == j * LANES + aug0[j] + c)
                place = hit if place is None else place | hit
        aug = _dot(pieces, jnp.where(place, 1.0, 0.0).astype(MM_DTYPE))
        for j in range(2):
            kaug[j] = jnp.where(_head_mask(k.shape, j, 1), k,
                                aug[:, j * LANES:(j + 1) * LANES]).astype(kaug.dtype)
            vaug[j] = jnp.concatenate(
                [vt[j * HEAD_DIM:(j + 1) * HEAD_DIM], jnp.ones((n_ones, vt.shape[1]), F32)],
                axis=0).astype(vaug.dtype)

    row = lax.broadcasted_iota(jnp.int32, (HEAD_DIM, LANES), 0)
    ones3 = jnp.tile(jnp.where(row < 3, 1.0, 0.0).astype(MM_DTYPE), (1, t // LANES))
    qa = _masked_heads_t(qt_ref[...], ones3)
    m_ref[...] = jnp.full(m_ref.shape, NEG, F32)
    acc_ref[...] = jnp.zeros_like(acc_ref)

    tk = TK_SB
    n_sub = t // tk

    def unit(j, k0, qc, diag):
        cols = slice(qc * tk, (qc + 1) * tk)
        z = _dot(kaug[j, pl.ds(k0, tk), :], qa[j][:, cols])
        for _ in range(FOX_SKEW):
            yield
        if diag:
            causal = (lax.broadcasted_iota(jnp.int32, z.shape, 0)
                      <= lax.broadcasted_iota(jnp.int32, z.shape, 1))
            z = jnp.where(causal, z, NEG)
        m = m_ref[j, :, cols]
        m_new = jnp.maximum(m, jnp.max(z, axis=0, keepdims=True))
        pr = jnp.exp(z - m_new).astype(MM_DTYPE)
        alpha = jnp.exp(m - m_new)
        m_ref[j, :, cols] = m_new
        yield
        pv = _dot(vaug[j, :, pl.ds(k0, tk)], pr)
        acc_ref[j, :, cols] = alpha * acc_ref[j, :, cols] + pv

    def units(base, n_blocks, diag):
        out = []
        for c in range(n_blocks):
            for qc in range(c if diag else 0, n_sub):
                for j in range(2):
                    out.append(unit(j, pl.multiple_of(base + c * tk, tk), qc, diag and qc == c))
        return out

    trip = min(FOX_KEYS_PER_TRIP, t)

    def body(kj, _):
        _run_skewed(units(kj * trip, trip // tk, False))
        return 0

    if kaug.shape[1] > t:
        lax.fori_loop(0, i * (t // trip), body, 0)
    _run_skewed(units(i * t, n_sub, True))
    outs = [acc_ref[j, :HEAD_DIM] / acc_ref[j, HEAD_DIM:HEAD_DIM + 1] for j in range(2)]
    o_ref[...] = _gated_t(g_ref[...], outs[0], outs[1]).astype(o_ref.dtype)


def _fox_attention(qt, k, vt, g, cum):
    _, b, _, s = qt.shape
    t = min(TQ_ATT, s)
    return pl.pallas_call(
        _fox_kernel,
        grid=(b, W_GROUP // LANES, s // t),
        in_specs=_t_specs(2, t, s) + [
            pl.BlockSpec((None, s, F_PAD), lambda b_, p, i: (b_, 0, 0))],
        out_specs=pl.BlockSpec((None, t, LANES), _out_tile),
        out_shape=jax.ShapeDtypeStruct((b, s, W_GROUP), MM_DTYPE),
        scratch_shapes=[pltpu.VMEM((2, s, LANES), MM_DTYPE),
                        pltpu.VMEM((2, FOX_V_ROWS, s), MM_DTYPE),
                        pltpu.VMEM((2, 1, t), F32), pltpu.VMEM((2, FOX_V_ROWS, t), F32)],
        compiler_params=_ATT_PARAMS,
        name="fox_attention",
    )(qt, k, vt, g, cum)


def _out_kernel(x_ref, a_ref, b_ref, c_ref, w_ref, o_ref):
    acc = _dot(a_ref[...], w_ref[0]) + _dot(b_ref[...], w_ref[1]) + _dot(c_ref[...], w_ref[2])
    o_ref[...] = x_ref[...] + acc


def _out_projection(x, ma, mb, mc, w):
    b, s, d = x.shape
    tm = min(TM_OUT, s)
    mix = pl.BlockSpec((None, tm, W_GROUP), lambda bi, i: (bi, i, 0))
    return pl.pallas_call(
        _out_kernel,
        grid=(b, s // tm),
        in_specs=[pl.BlockSpec((None, tm, d), lambda bi, i: (bi, i, 0)), mix, mix, mix,
                  pl.BlockSpec((N_GROUPS, W_GROUP, d), lambda bi, i: (0, 0, 0))],
        out_specs=pl.BlockSpec((None, tm, d), lambda bi, i: (bi, i, 0)),
        out_shape=jax.ShapeDtypeStruct(x.shape, x.dtype),
        compiler_params=pltpu.CompilerParams(
            dimension_semantics=("arbitrary", "arbitrary"), vmem_limit_bytes=VMEM_LIMIT),
        name="out_proj",
    )(x, ma, mb, mc, w)


def _constants():
    r = np.arange(MXU_DIM)
    bd = (r[:, None] // HEAD_DIM == r[None, :] // HEAD_DIM).astype(np.float32)
    r = np.arange(TK_SB)
    u_sb = -(r[None, :] >= r[:, None]).astype(np.float32)
    r = np.arange(CUM_BLK)
    low = (r[None, :] <= r[:, None]).astype(np.float32)
    return (jnp.asarray(bd, MM_DTYPE), jnp.asarray(u_sb, MM_DTYPE), jnp.asarray(low, MM_DTYPE))


def _relayout_kernel(w_ref, o_ref, *, transpose):
    w = w_ref[...]
    o_ref[...] = (w.T if transpose else w).astype(o_ref.dtype)


def _relayout_weights(w_in, first_col, transpose):
    depth, d, _ = w_in.shape
    blk = (W_GROUP, d) if transpose else (d, W_GROUP)
    return pl.pallas_call(
        functools.partial(_relayout_kernel, transpose=transpose),
        grid=(depth, 2 * N_GROUPS),
        in_specs=[pl.BlockSpec((None, d, W_GROUP),
                               lambda l, j: (l, 0, (j % N_GROUPS) * 4 + first_col
                                             + 2 * (j // N_GROUPS)))],
        out_specs=pl.BlockSpec((None, None) + blk, lambda l, j: (l, j, 0, 0)),
        out_shape=jax.ShapeDtypeStruct((depth, 2 * N_GROUPS) + blk, MM_DTYPE),
        compiler_params=pltpu.CompilerParams(dimension_semantics=("arbitrary", "arbitrary")),
        name="w_tr" if transpose else "w_nat",
    )(w_in)


def _proj_weights(ng, w_nat, w_tr, w_f_cols, qn_ch, kn_ch, qn_fox, kn_fox, bd):
    w_f = jnp.pad(w_f_cols, ((0, 0), (0, F_PAD - H_GROUP))).astype(MM_DTYPE)
    grow = jnp.stack([jnp.tile(kn_ch, H_GROUP), jnp.tile(kn_fox, H_GROUP)])[:, None, :].astype(F32)
    gcol = jnp.stack([qn_ch, qn_fox])[:, :, None].astype(F32)
    return (ng[None, :], w_nat, w_tr, w_f, grow, gcol, bd)


def _mixers(projected, b_forget, rel_bias, u_sb, low):
    qt, k, vt, g, f = projected
    bias_row = jnp.pad(b_forget.astype(F32), (0, F_PAD - H_GROUP))[None, :]
    cum = _cum_forget(f, bias_row, low)
    m_sb = _sb_attention(qt, k, vt, g, u_sb)
    m_ch = _chunk_attention(qt, k, vt, g, _chunk_bias_table(rel_bias))
    m_fx = _fox_attention(qt, k, vt, g, cum)
    return m_sb, m_ch, m_fx


def kernel(x, norm_g, w_in, b_forget, q_norm_ch, k_norm_ch, q_norm_fox, k_norm_fox, rel_bias, w_out):
    bd, u_sb, low = _constants()
    depth, d = norm_g.shape
    w_nat = _relayout_weights(w_in, 1, transpose=False)
    w_tr = _relayout_weights(w_in, 0, transpose=True)
    weights = [_proj_weights(norm_g[l], w_nat, w_tr, w_in[l, :, N_GROUPS * 4 * W_GROUP:],
                             q_norm_ch[l], k_norm_ch[l], q_norm_fox[l], k_norm_fox[l], bd)
               for l in range(depth)]
    w_o = [w_out[l].reshape(N_GROUPS, W_GROUP, d).astype(MM_DTYPE) for l in range(depth)]
    projected = _projection(x, weights[0], 0)
    for l in range(depth):
        mixed = _mixers(projected, b_forget[l], rel_bias[l], u_sb, low)
        if l + 1 < depth:
            x, *projected = _projection(x, weights[l + 1], l + 1, prev=(*mixed, w_o[l]))
        else:
            x = _out_projection(x, *mixed, w_o[l])
    return x
```

```python
import functools

import jax
import jax.numpy as jnp
import numpy as np
from jax import lax
from jax.experimental import pallas as pl
from jax.experimental.pallas import tpu as pltpu

D_MODEL = 1024
HEAD_DIM = 64
H_GROUP = 8
W_GROUP = H_GROUP * HEAD_DIM
N_GROUPS = 3
CHUNK = 64
N_LEFT_CHUNKS = 8
LEFT = N_LEFT_CHUNKS * CHUNK
REL_CLIP = 128
EPS = 1e-6
SCALE = HEAD_DIM ** -0.5

LANES = 128
MXU_DIM = 256
F_PAD = LANES
NEG = -1e30

MM_DTYPE = jnp.bfloat16
F32 = jnp.float32

TM_PROJ = 512
TM_OUT = 1024
TQ_ATT = 4096
FOX_KEYS_PER_TRIP = 1024
BF16_ROWS = 16
FOX_V_ROWS = HEAD_DIM + BF16_ROWS
FOX_SKEW = 5
SB_SKEW = 2
SB_STATIC_DIAGS = 2
SB_DEAD_LOG = -104.0
TK_SB = MXU_DIM
TQ_CHUNK = MXU_DIM
WIN_CHUNK = LEFT + TQ_CHUNK
TS_CHUNK = 4096
CUM_BLK = MXU_DIM
VMEM_LIMIT = 56 * 1024 * 1024


def _dot(a, b):
    return jnp.dot(a, b, preferred_element_type=F32)


def _dot_nt(a, b):
    return lax.dot_general(a, b, (((1,), (1,)), ((), ())), preferred_element_type=F32)


def _split2(x):
    hi = x.astype(MM_DTYPE)
    lo = (x - hi.astype(F32)).astype(MM_DTYPE)
    return hi, lo


def _split3(x):
    hi = x.astype(MM_DTYPE)
    mid, lo = _split2(x - hi.astype(F32))
    return hi, mid, lo


def _head_mask(shape, j, axis):
    idx = lax.broadcasted_iota(jnp.int32, shape, axis)
    return (idx >= j * HEAD_DIM) & (idx < (j + 1) * HEAD_DIM)


def _norm_rows(y, gain_row, bd):
    sq = (y * y).astype(MM_DTYPE)
    parts = []
    for c in range(W_GROUP // MXU_DIM):
        sl = slice(c * MXU_DIM, (c + 1) * MXU_DIM)
        parts.append(_dot(sq[:, sl], bd))
    ssq = jnp.concatenate(parts, axis=1)
    return y * lax.rsqrt(ssq * (1.0 / HEAD_DIM) + EPS) * gain_row


def _norm_cols(yt, gain_col):
    y3 = yt.reshape(H_GROUP, HEAD_DIM, yt.shape[1])
    ssq = jnp.sum(y3 * y3, axis=1, keepdims=True)
    y3 = y3 * lax.rsqrt(ssq * (1.0 / HEAD_DIM) + EPS) * gain_col
    return y3.reshape(yt.shape)


def _proj_kernel(x_ref, *refs):
    _project(x_ref[...], *refs)


def _out_proj_kernel(x_ref, a_ref, b_ref, c_ref, wo_ref, *refs):
    *proj_refs, xo_ref = refs
    tm = x_ref.shape[0]
    parts = []
    for rows in (slice(0, tm // 2), slice(tm // 2, tm)):
        x = x_ref[rows] + (_dot(a_ref[rows], wo_ref[0]) + _dot(b_ref[rows], wo_ref[1])
                           + _dot(c_ref[rows], wo_ref[2]))
        xo_ref[rows] = x
        parts.append(x)
    _project(jnp.concatenate(parts, axis=0), *proj_refs)


def _project(x, ng_ref, wn_ref, wt_ref, wf_ref, grow_ref, gcol_ref, bd_ref,
             qt_ref, k_ref, vt_ref, g_ref, f_ref):
    h = x * lax.rsqrt(jnp.mean(x * x, axis=-1, keepdims=True) + EPS) * ng_ref[...]
    hb = h.astype(MM_DTYPE)
    bd = bd_ref[...]
    dt = qt_ref.dtype
    for grp in range(N_GROUPS):
        qt = _dot_nt(wt_ref[grp], hb)
        k = _dot(hb, wn_ref[grp])
        if grp > 0:
            qt = _norm_cols(qt, gcol_ref[grp - 1])
            k = _norm_rows(k, grow_ref[grp - 1], bd)
        qt_ref[grp] = (qt * SCALE).astype(dt)
        k_ref[grp] = k.astype(dt)
        vt_ref[grp] = _dot_nt(wt_ref[N_GROUPS + grp], hb).astype(dt)
        g_ref[grp] = _dot(hb, wn_ref[N_GROUPS + grp])
    f_ref[...] = _dot(hb, wf_ref[...])


def _projection(x, proj_weights, layer, prev=None):
    b, s, d = x.shape
    tm = min(TM_PROJ, s)
    const = dict(pipeline_mode=pl.Buffered(1))
    row_spec = pl.BlockSpec((None, tm, d), lambda bi, i: (bi, i, 0))
    nat_spec = pl.BlockSpec((N_GROUPS, None, tm, W_GROUP), lambda bi, i: (0, bi, i, 0))
    tr_spec = pl.BlockSpec((N_GROUPS, None, W_GROUP, tm), lambda bi, i: (0, bi, 0, i))
    nat_shape = jax.ShapeDtypeStruct((N_GROUPS, b, s, W_GROUP), MM_DTYPE)
    tr_shape = jax.ShapeDtypeStruct((N_GROUPS, b, W_GROUP, s), MM_DTYPE)
    prev_specs, extra_out_specs, extra_out_shapes = [], [], []
    if prev is not None:
        mix = pl.BlockSpec((None, tm, W_GROUP), lambda bi, i: (bi, i, 0))
        prev_specs = [mix, mix, mix,
                      pl.BlockSpec((None, N_GROUPS, W_GROUP, d),
                                   lambda bi, i: (layer - 1, 0, 0, 0), **const)]
        extra_out_specs = [row_spec]
        extra_out_shapes = [jax.ShapeDtypeStruct(x.shape, x.dtype)]
    outs = pl.pallas_call(
        _proj_kernel if prev is None else _out_proj_kernel,
        grid=(b, s // tm),
        in_specs=[row_spec] + prev_specs + [
            pl.BlockSpec((1, d), lambda bi, i: (0, 0)),
            pl.BlockSpec((None, 2 * N_GROUPS, d, W_GROUP), lambda bi, i: (layer, 0, 0, 0), **const),
            pl.BlockSpec((None, 2 * N_GROUPS, W_GROUP, d), lambda bi, i: (layer, 0, 0, 0), **const),
            pl.BlockSpec((d, F_PAD), lambda bi, i: (0, 0), **const),
            pl.BlockSpec((2, 1, W_GROUP), lambda bi, i: (0, 0, 0)),
            pl.BlockSpec((2, HEAD_DIM, 1), lambda bi, i: (0, 0, 0)),
            pl.BlockSpec((MXU_DIM, MXU_DIM), lambda bi, i: (0, 0)),
        ],
        out_specs=[
            tr_spec, nat_spec, tr_spec,
            pl.BlockSpec((N_GROUPS, None, tm, W_GROUP), lambda bi, i: (0, bi, i, 0)),
            pl.BlockSpec((None, tm, F_PAD), lambda bi, i: (bi, i, 0)),
        ] + extra_out_specs,
        out_shape=[
            tr_shape, nat_shape, tr_shape,
            jax.ShapeDtypeStruct((N_GROUPS, b, s, W_GROUP), F32),
            jax.ShapeDtypeStruct((b, s, F_PAD), F32),
        ] + extra_out_shapes,
        compiler_params=pltpu.CompilerParams(
            dimension_semantics=("arbitrary", "arbitrary"), vmem_limit_bytes=VMEM_LIMIT),
        name="proj" if prev is None else "out_proj_proj",
    )(x, *(prev or ()), *proj_weights)
    return outs if prev is None else (outs[-1], *outs[:-1])


def _cum_kernel(f_ref, b_ref, l_ref, o_ref):
    z = f_ref[...] + b_ref[...]
    lf = jnp.minimum(z, 0.0) - jnp.log1p(jnp.exp(-jnp.abs(z)))
    low = l_ref[...]
    carry = jnp.zeros((1, F_PAD), F32)
    for c in range(lf.shape[0] // CUM_BLK):
        rows = slice(c * CUM_BLK, (c + 1) * CUM_BLK)
        hi, mid, lo = _split3(lf[rows])
        cs = (_dot(low, hi) + _dot(low, mid)) + _dot(low, lo) + carry
        o_ref[rows] = cs
        carry = cs[CUM_BLK - 1:CUM_BLK, :]


def _cum_forget(f, bias_row, low):
    b, s, _ = f.shape
    return pl.pallas_call(
        _cum_kernel,
        grid=(b,),
        in_specs=[
            pl.BlockSpec((None, s, F_PAD), lambda bi: (bi, 0, 0)),
            pl.BlockSpec((1, F_PAD), lambda bi: (0, 0)),
            pl.BlockSpec((CUM_BLK, CUM_BLK), lambda bi: (0, 0)),
        ],
        out_specs=pl.BlockSpec((None, s, F_PAD), lambda bi: (bi, 0, 0)),
        out_shape=jax.ShapeDtypeStruct((b, s, F_PAD), F32),
        compiler_params=pltpu.CompilerParams(dimension_semantics=("arbitrary",)),
        name="cum_forget",
    )(f, bias_row, low)


def _gated_t(g, ot0, ot1):
    o = jnp.concatenate([ot0, ot1], axis=0).T
    return o * (g * jax.nn.sigmoid(g))


def _out_tile(b, p, i):
    return (b, i, p)


def _t_specs(grp, tq, s):
    return [
        pl.BlockSpec((None, None, LANES, tq), lambda b, p, i: (grp, b, p, i)),
        pl.BlockSpec((None, None, s, LANES), lambda b, p, i: (grp, b, 0, p)),
        pl.BlockSpec((None, None, LANES, s), lambda b, p, i: (grp, b, p, 0)),
        pl.BlockSpec((None, None, tq, LANES), lambda b, p, i: (grp, b, i, p)),
    ]


_ATT_PARAMS = pltpu.CompilerParams(
    dimension_semantics=("arbitrary", "arbitrary", "arbitrary"), vmem_limit_bytes=VMEM_LIMIT)


def _neg_abs(x):
    bits = lax.bitcast_convert_type(x, jnp.uint32) | jnp.uint32(0x80000000)
    return lax.bitcast_convert_type(bits, F32)


def _run_skewed(units):
    pending = list(units)
    active = []
    while pending or active:
        if pending:
            active.append(pending.pop(0))
        for g in list(active):
            try:
                next(g)
            except StopIteration:
                active.remove(g)


def _masked_heads_t(qt, other=None):
    if other is None:
        other = jnp.zeros((HEAD_DIM, qt.shape[1]), qt.dtype)
    return [jnp.concatenate([qt[:HEAD_DIM], other], axis=0),
            jnp.concatenate([other, qt[HEAD_DIM:]], axis=0)]


def _sb_kernel(qt_ref, k_ref, vt_ref, g_ref, u_ref, o_ref, acc_ref, carry_ref):
    t = qt_ref.shape[1]
    tk = u_ref.shape[0]
    n_sub = t // tk
    qm = _masked_heads_t(qt_ref[...])
    u = u_ref[...]
    acc_ref[...] = jnp.zeros_like(acc_ref)
    carry_ref[...] = jnp.zeros_like(carry_ref)

    def unit(j, kb, qc, diag):
        cols = slice(qc * tk, (qc + 1) * tk)
        static = isinstance(kb, int)
        k0 = kb * tk if static else pl.multiple_of(jnp.maximum(kb, 0) * tk, tk)
        z = _dot(k_ref[pl.ds(k0, tk), :], qm[j][:, cols])
        for _ in range(SB_SKEW):
            yield
        sp = jnp.maximum(z, 0.0) + jnp.log(1.0 + jnp.exp(_neg_abs(z)))
        if diag:
            strict = (lax.broadcasted_iota(jnp.int32, z.shape, 0)
                      < lax.broadcasted_iota(jnp.int32, z.shape, 1))
            sp = jnp.where(strict, sp, 0.0)
        spb = sp.astype(MM_DTYPE)
        yield
        incl = _dot(u, spb)
        for _ in range(SB_SKEW):
            yield
        w = jnp.exp(z + incl)
        if diag:
            w = jnp.where(strict, w, 0.0)
        wb = w.astype(MM_DTYPE)
        yield
        pv = _dot(vt_ref[j * HEAD_DIM:(j + 1) * HEAD_DIM, pl.ds(k0, tk)], wb)
        carry = carry_ref[j, :, cols]
        scale = jnp.exp(carry)
        step = incl[:1]
        if not static:
            scale = jnp.where(kb >= 0, scale, 0.0)
            step = jnp.where(kb >= 0, step, 0.0)
        acc_ref[j, :, cols] += scale * pv
        carry_ref[j, :, cols] = carry + step

    def diagonal(d, first_qc, masked=False):
        return [unit(j, qc - d, qc, masked) for qc in range(first_qc, n_sub) for j in range(2)]

    def live(d):
        lane = lax.broadcasted_iota(jnp.int32, carry_ref.shape, 2)
        return jnp.max(jnp.where(lane >= d * tk, carry_ref[...], NEG)) >= SB_DEAD_LOG

    def store(rows):
        o_ref[rows] = _gated_t(g_ref[rows], acc_ref[0, :, rows], acc_ref[1, :, rows]).astype(o_ref.dtype)

    def finish(qc):
        for _ in range(2 * SB_SKEW + 3):
            yield
        store(slice(qc * tk, (qc + 1) * tk))

    n_static = min(SB_STATIC_DIAGS, n_sub)
    static_units = []
    for d in range(n_static):
        for qc in range(d, n_sub):
            static_units += [unit(j, qc - d, qc, d == 0) for j in range(2)]
            if d == min(qc, n_static - 1):
                static_units.append(finish(qc))
    _run_skewed(static_units)

    def body(state):
        d, _ = state
        _run_skewed(diagonal(d, n_static))
        return d + 1, live(d + 1)

    if n_static < n_sub:
        d_end, _ = lax.while_loop(lambda st: (st[0] < n_sub) & st[1], body,
                                  (n_static, live(n_static)))
        pl.when(d_end > n_static)(lambda: store(slice(n_static * tk, t)))


def _sb_attention(qt, k, vt, g, u):
    _, b, _, s = qt.shape
    t = s
    tk = u.shape[0]
    return pl.pallas_call(
        _sb_kernel,
        grid=(b, W_GROUP // LANES, s // t),
        in_specs=_t_specs(0, t, s) + [pl.BlockSpec((tk, tk), lambda b_, p, i: (0, 0))],
        out_specs=pl.BlockSpec((None, t, LANES), _out_tile),
        out_shape=jax.ShapeDtypeStruct((b, s, W_GROUP), MM_DTYPE),
        scratch_shapes=[pltpu.VMEM((2, HEAD_DIM, t), F32), pltpu.VMEM((2, 1, t), F32)],
        compiler_params=_ATT_PARAMS,
        name="sb_attention",
    )(qt, k, vt, g, u)


def _chunk_kernel(qt_ref, k_ref, vt_ref, g_ref, bias_ref, o_ref, kpad, vaug, ot_ref):
    i = pl.program_id(2)
    ts = qt_ref.shape[1]
    win, tq = bias_ref.shape[1:]
    s = k_ref.shape[0]
    n_ones = vaug.shape[1] - HEAD_DIM

    @pl.when(i == 0)
    def _():
        kpad[:LEFT] = jnp.zeros((LEFT, LANES), kpad.dtype)
        kpad[LEFT:] = k_ref[...]
        vt = vt_ref[...].astype(F32)
        for j in range(2):
            vaug[j, :, :LEFT] = jnp.zeros((vaug.shape[1], LEFT), vaug.dtype)
            vaug[j, :, LEFT:] = jnp.concatenate(
                [vt[j * HEAD_DIM:(j + 1) * HEAD_DIM], jnp.ones((n_ones, s), F32)],
                axis=0).astype(vaug.dtype)

    qm = _masked_heads_t(qt_ref[...])

    def unit(j, r, first_step):
        cols = slice(r * tq, (r + 1) * tq)
        r0 = pl.multiple_of(i * ts + r * tq, tq)
        zs = []
        for kb in range(win // tq):
            k0 = pl.multiple_of(r0 + kb * tq, tq)
            z = _dot(kpad[pl.ds(k0, tq), :], qm[j][:, cols]) + bias_ref[j, kb * tq:(kb + 1) * tq]
            lo_key = LEFT - r * tq - kb * tq
            if first_step and lo_key > 0:
                key = lax.broadcasted_iota(jnp.int32, z.shape, 0)
                z = jnp.where(key >= lo_key, z, NEG)
            zs.append(z)
            yield
        m = zs[0].max(axis=0, keepdims=True)
        for z in zs[1:]:
            m = jnp.maximum(m, z.max(axis=0, keepdims=True))
        ps = []
        for z in zs:
            ps.append(jnp.exp(z - m).astype(MM_DTYPE))
            yield
        pv = None
        for kb, p in enumerate(ps):
            k0 = pl.multiple_of(r0 + kb * tq, tq)
            term = _dot(vaug[j, :, pl.ds(k0, tq)], p)
            pv = term if pv is None else pv + term
        ot_ref[j * HEAD_DIM:(j + 1) * HEAD_DIM, cols] = pv[:HEAD_DIM] / pv[HEAD_DIM:HEAD_DIM + 1]

    n_stages = 2 * (win // tq) + 1

    def finish(r):
        for _ in range(n_stages):
            yield
        cols = slice(r * tq, (r + 1) * tq)
        o_ref[cols] = _gated_t(g_ref[cols], ot_ref[:HEAD_DIM, cols],
                               ot_ref[HEAD_DIM:, cols]).astype(o_ref.dtype)

    def run(first_step):
        _run_skewed([un for r in range(ts // tq)
                     for un in (unit(0, r, first_step), unit(1, r, first_step), finish(r))])

    if s == ts:
        run(True)
    else:
        pl.when(i == 0)(lambda: run(True))
        pl.when(i > 0)(lambda: run(False))


def _chunk_attention(qt, k, vt, g, bias):
    _, b, _, s = qt.shape
    ts = min(TS_CHUNK, s)
    return pl.pallas_call(
        _chunk_kernel,
        grid=(b, W_GROUP // LANES, s // ts),
        in_specs=_t_specs(1, ts, s) + [
            pl.BlockSpec((2, WIN_CHUNK, TQ_CHUNK), lambda b_, p, i: (p, 0, 0))],
        out_specs=pl.BlockSpec((None, ts, LANES), _out_tile),
        out_shape=jax.ShapeDtypeStruct((b, s, W_GROUP), MM_DTYPE),
        scratch_shapes=[pltpu.VMEM((LEFT + s, LANES), MM_DTYPE),
                        pltpu.VMEM((2, FOX_V_ROWS, LEFT + s), MM_DTYPE),
                        pltpu.VMEM((LANES, ts), F32)],
        compiler_params=_ATT_PARAMS,
        name="chunk_attention",
    )(qt, k, vt, g, bias)


N_DIAG = WIN_CHUNK + TQ_CHUNK


def _bias_kernel(v_ref, o_ref):
    x = jnp.broadcast_to(v_ref[...], (WIN_CHUNK, N_DIAG))
    y = pltpu.roll(x, 1, 1, stride=1, stride_axis=0)
    t = y[:, WIN_CHUNK:]
    c = lax.broadcasted_iota(jnp.int32, t.shape, 0)
    r = lax.broadcasted_iota(jnp.int32, t.shape, 1)
    band = c - (r - (r & (CHUNK - 1)))
    o_ref[...] = jnp.where((band >= 0) & (band < LEFT + CHUNK), t, NEG)


def _chunk_bias_table(rel_bias):
    kk = np.arange(N_DIAG - 1)
    rel = kk - (WIN_CHUNK - 1) + LEFT
    vec = rel_bias[:, np.clip(rel, -REL_CLIP, REL_CLIP) + REL_CLIP].astype(F32)
    vec = jnp.pad(vec, ((0, 0), (0, 1)))[:, None, :]
    h = vec.shape[0]
    return pl.pallas_call(
        _bias_kernel,
        grid=(h,),
        in_specs=[pl.BlockSpec((None, 1, N_DIAG), lambda i: (i, 0, 0))],
        out_specs=pl.BlockSpec((None, WIN_CHUNK, TQ_CHUNK), lambda i: (i, 0, 0)),
        out_shape=jax.ShapeDtypeStruct((h, WIN_CHUNK, TQ_CHUNK), F32),
        compiler_params=pltpu.CompilerParams(dimension_semantics=("arbitrary",)),
        name="chunk_bias",
    )(vec)


def _fox_kernel(qt_ref, k_ref, vt_ref, g_ref, c_ref, o_ref, kaug, vaug, m_ref, acc_ref):
    i = pl.program_id(2)
    p = pl.program_id(1)
    t = qt_ref.shape[1]
    aug0 = [HEAD_DIM * (1 - j) for j in range(2)]
    tk = vaug.shape[3]
    n_ones = vaug.shape[2] - HEAD_DIM

    @pl.when(i == 0)
    def _():
        k = k_ref[...].astype(F32)
        vt = vt_ref[...].astype(F32)
        pieces = jnp.concatenate(_split3(-c_ref[...]), axis=1)
        row = lax.broadcasted_iota(jnp.int32, (3 * LANES, 2 * LANES), 0)
        col = lax.broadcasted_iota(jnp.int32, (3 * LANES, 2 * LANES), 1)
        place = None
        for j in range(2):
            for c in range(3):
                hit = (row == c * LANES + 2 * p + j) & (col == j * LANES + aug0[j] + c)
                place = hit if place is None else place | hit
        aug = _dot(pieces, jnp.where(place, 1.0, 0.0).astype(MM_DTYPE))
        for j in range(2):
            kaug[j] = jnp.where(_head_mask(k.shape, j, 1), k,
                                aug[:, j * LANES:(j + 1) * LANES]).astype(kaug.dtype)
            va = jnp.concatenate(
                [vt[j * HEAD_DIM:(j + 1) * HEAD_DIM], jnp.ones((n_ones, vt.shape[1]), F32)],
                axis=0).astype(vaug.dtype)
            for c in range(vaug.shape[1]):
                vaug[j, c] = va[:, c * tk:(c + 1) * tk]

    row = lax.broadcasted_iota(jnp.int32, (HEAD_DIM, LANES), 0)
    ones3 = jnp.tile(jnp.where(row < 3, 1.0, 0.0).astype(MM_DTYPE), (1, t // LANES))
    qa = _masked_heads_t(qt_ref[...], ones3)
    m_ref[...] = jnp.full(m_ref.shape, NEG, F32)
    acc_ref[...] = jnp.zeros_like(acc_ref)
    n_sub = t // tk

    def unit(j, kb, qc, diag):
        cols = slice(qc * tk, (qc + 1) * tk)
        k0 = pl.multiple_of(kb * tk, tk)
        z = _dot(kaug[j, pl.ds(k0, tk), :], qa[j][:, cols])
        for _ in range(FOX_SKEW):
            yield
        if diag:
            causal = (lax.broadcasted_iota(jnp.int32, z.shape, 0)
                      <= lax.broadcasted_iota(jnp.int32, z.shape, 1))
            z = jnp.where(causal, z, NEG)
        m = m_ref[j, qc]
        m_new = jnp.maximum(m, jnp.max(z, axis=0, keepdims=True))
        pr = jnp.exp(z - m_new).astype(MM_DTYPE)
        alpha = jnp.exp(m - m_new)
        m_ref[j, qc] = m_new
        yield
        pv = _dot(vaug[j, kb], pr)
        acc_ref[j, qc] = alpha * acc_ref[j, qc] + pv

    def units(first_kb, n_blocks, diag):
        out = []
        for c in range(n_blocks):
            for qc in range(c if diag else 0, n_sub):
                for j in range(2):
                    out.append(unit(j, first_kb + c, qc, diag and qc == c))
        return out

    trip = min(FOX_KEYS_PER_TRIP, t) // tk

    def body(kj, _):
        _run_skewed(units(kj * trip, trip, False))
        return 0

    if kaug.shape[1] > t:
        lax.fori_loop(0, i * (n_sub // trip), body, 0)
    _run_skewed(units(i * n_sub, n_sub, True))
    outs = [jnp.concatenate([acc_ref[j, qc, :HEAD_DIM] / acc_ref[j, qc, HEAD_DIM:HEAD_DIM + 1]
                             for qc in range(n_sub)], axis=1) for j in range(2)]
    o_ref[...] = _gated_t(g_ref[...], outs[0], outs[1]).astype(o_ref.dtype)


def _fox_attention(qt, k, vt, g, cum):
    _, b, _, s = qt.shape
    t = min(TQ_ATT, s)
    return pl.pallas_call(
        _fox_kernel,
        grid=(b, W_GROUP // LANES, s // t),
        in_specs=_t_specs(2, t, s) + [
            pl.BlockSpec((None, s, F_PAD), lambda b_, p, i: (b_, 0, 0))],
        out_specs=pl.BlockSpec((None, t, LANES), _out_tile),
        out_shape=jax.ShapeDtypeStruct((b, s, W_GROUP), MM_DTYPE),
        scratch_shapes=[pltpu.VMEM((2, s, LANES), MM_DTYPE),
                        pltpu.VMEM((2, s // TK_SB, FOX_V_ROWS, TK_SB), MM_DTYPE),
                        pltpu.VMEM((2, t // TK_SB, 1, TK_SB), F32),
                        pltpu.VMEM((2, t // TK_SB, FOX_V_ROWS, TK_SB), F32)],
        compiler_params=_ATT_PARAMS,
        name="fox_attention",
    )(qt, k, vt, g, cum)


def _out_kernel(x_ref, a_ref, b_ref, c_ref, w_ref, o_ref):
    acc = _dot(a_ref[...], w_ref[0]) + _dot(b_ref[...], w_ref[1]) + _dot(c_ref[...], w_ref[2])
    o_ref[...] = x_ref[...] + acc


def _out_projection(x, ma, mb, mc, w, layer):
    b, s, d = x.shape
    tm = min(TM_OUT, s)
    mix = pl.BlockSpec((None, tm, W_GROUP), lambda bi, i: (bi, i, 0))
    return pl.pallas_call(
        _out_kernel,
        grid=(b, s // tm),
        in_specs=[pl.BlockSpec((None, tm, d), lambda bi, i: (bi, i, 0)), mix, mix, mix,
                  pl.BlockSpec((None, N_GROUPS, W_GROUP, d), lambda bi, i: (layer, 0, 0, 0))],
        out_specs=pl.BlockSpec((None, tm, d), lambda bi, i: (bi, i, 0)),
        out_shape=jax.ShapeDtypeStruct(x.shape, x.dtype),
        compiler_params=pltpu.CompilerParams(
            dimension_semantics=("arbitrary", "arbitrary"), vmem_limit_bytes=VMEM_LIMIT),
        name="out_proj",
    )(x, ma, mb, mc, w)


def _constants():
    r = np.arange(MXU_DIM)
    bd = (r[:, None] // HEAD_DIM == r[None, :] // HEAD_DIM).astype(np.float32)
    r = np.arange(TK_SB)
    u_sb = -(r[None, :] >= r[:, None]).astype(np.float32)
    r = np.arange(CUM_BLK)
    low = (r[None, :] <= r[:, None]).astype(np.float32)
    return (jnp.asarray(bd, MM_DTYPE), jnp.asarray(u_sb, MM_DTYPE), jnp.asarray(low, MM_DTYPE))


def _relayout_kernel(w_ref, o_ref, *, transpose):
    w = w_ref[...]
    o_ref[...] = (w.T if transpose else w).astype(o_ref.dtype)


def _relayout_weights(w_in, first_col, transpose):
    depth, d, _ = w_in.shape
    blk = (W_GROUP, d) if transpose else (d, W_GROUP)
    return pl.pallas_call(
        functools.partial(_relayout_kernel, transpose=transpose),
        grid=(depth, 2 * N_GROUPS),
        in_specs=[pl.BlockSpec((None, d, W_GROUP),
                               lambda l, j: (l, 0, (j % N_GROUPS) * 4 + first_col
                                             + 2 * (j // N_GROUPS)))],
        out_specs=pl.BlockSpec((None, None) + blk, lambda l, j: (l, j, 0, 0)),
        out_shape=jax.ShapeDtypeStruct((depth, 2 * N_GROUPS) + blk, MM_DTYPE),
        compiler_params=pltpu.CompilerParams(dimension_semantics=("arbitrary", "arbitrary")),
        name="w_tr" if transpose else "w_nat",
    )(w_in)


def _cast_w_out(w_out):
    depth, _, d = w_out.shape
    return pl.pallas_call(
        functools.partial(_relayout_kernel, transpose=False),
        grid=(depth, N_GROUPS),
        in_specs=[pl.BlockSpec((None, W_GROUP, d), lambda l, j: (l, j, 0))],
        out_specs=pl.BlockSpec((None, None, W_GROUP, d), lambda l, j: (l, j, 0, 0)),
        out_shape=jax.ShapeDtypeStruct((depth, N_GROUPS, W_GROUP, d), MM_DTYPE),
        compiler_params=pltpu.CompilerParams(dimension_semantics=("arbitrary", "arbitrary")),
        name="w_out_cast",
    )(w_out)


def _proj_weights(ng, w_nat, w_tr, w_f_cols, qn_ch, kn_ch, qn_fox, kn_fox, bd):
    w_f = jnp.pad(w_f_cols, ((0, 0), (0, F_PAD - H_GROUP))).astype(MM_DTYPE)
    grow = jnp.stack([jnp.tile(kn_ch, H_GROUP), jnp.tile(kn_fox, H_GROUP)])[:, None, :].astype(F32)
    gcol = jnp.stack([qn_ch, qn_fox])[:, :, None].astype(F32)
    return (ng[None, :], w_nat, w_tr, w_f, grow, gcol, bd)


def _mixers(projected, b_forget, rel_bias, u_sb, low):
    qt, k, vt, g, f = projected
    bias_row = jnp.pad(b_forget.astype(F32), (0, F_PAD - H_GROUP))[None, :]
    cum = _cum_forget(f, bias_row, low)
    m_sb = _sb_attention(qt, k, vt, g, u_sb)
    m_ch = _chunk_attention(qt, k, vt, g, _chunk_bias_table(rel_bias))
    m_fx = _fox_attention(qt, k, vt, g, cum)
    return m_sb, m_ch, m_fx


def kernel(x, norm_g, w_in, b_forget, q_norm_ch, k_norm_ch, q_norm_fox, k_norm_fox, rel_bias, w_out):
    bd, u_sb, low = _constants()
    depth, d = norm_g.shape
    w_nat = _relayout_weights(w_in, 1, transpose=False)
    w_tr = _relayout_weights(w_in, 0, transpose=True)
    weights = [_proj_weights(norm_g[l], w_nat, w_tr, w_in[l, :, N_GROUPS * 4 * W_GROUP:],
                             q_norm_ch[l], k_norm_ch[l], q_norm_fox[l], k_norm_fox[l], bd)
               for l in range(depth)]
    w_o = _cast_w_out(w_out)
    projected = _projection(x, weights[0], 0)
    for l in range(depth):
        mixed = _mixers(projected, b_forget[l], rel_bias[l], u_sb, low)
        if l + 1 < depth:
            x, *projected = _projection(x, weights[l + 1], l + 1, prev=(*mixed, w_o))
        else:
            x = _out_projection(x, *mixed, w_o, l)
    return x
```

```python
import functools

import jax
import jax.numpy as jnp
import numpy as np
from jax import lax
from jax.experimental import pallas as pl
from jax.experimental.pallas import tpu as pltpu

D_MODEL = 1024
HEAD_DIM = 64
H_GROUP = 8
W_GROUP = H_GROUP * HEAD_DIM
N_GROUPS = 3
CHUNK = 64
N_LEFT_CHUNKS = 8
LEFT = N_LEFT_CHUNKS * CHUNK
REL_CLIP = 128
EPS = 1e-6
SCALE = HEAD_DIM ** -0.5

LANES = 128
MXU_DIM = 256
F_PAD = LANES
NEG = -1e30

MM_DTYPE = jnp.bfloat16
F32 = jnp.float32

TM_PROJ = 512
TM_OUT = 1024
TQ_ATT = 4096
FOX_KEYS_PER_TRIP = 1024
BF16_ROWS = 16
FOX_V_ROWS = HEAD_DIM + BF16_ROWS
FOX_SKEW = 5
SB_SKEW = 2
SB_STATIC_DIAGS = 2
SB_DEAD_LOG = -104.0
TK_SB = MXU_DIM
TQ_CHUNK = MXU_DIM
WIN_CHUNK = LEFT + TQ_CHUNK
TS_CHUNK = 4096
CUM_BLK = MXU_DIM
VMEM_LIMIT = 56 * 1024 * 1024


def _dot(a, b):
    return jnp.dot(a, b, preferred_element_type=F32)


def _dot_nt(a, b):
    return lax.dot_general(a, b, (((1,), (1,)), ((), ())), preferred_element_type=F32)


def _split2(x):
    hi = x.astype(MM_DTYPE)
    lo = (x - hi.astype(F32)).astype(MM_DTYPE)
    return hi, lo


def _split3(x):
    hi = x.astype(MM_DTYPE)
    mid, lo = _split2(x - hi.astype(F32))
    return hi, mid, lo


def _head_mask(shape, j, axis):
    idx = lax.broadcasted_iota(jnp.int32, shape, axis)
    return (idx >= j * HEAD_DIM) & (idx < (j + 1) * HEAD_DIM)


def _norm_rows(y, gain_row, bd):
    sq = (y * y).astype(MM_DTYPE)
    parts = []
    for c in range(W_GROUP // MXU_DIM):
        sl = slice(c * MXU_DIM, (c + 1) * MXU_DIM)
        parts.append(_dot(sq[:, sl], bd))
    ssq = jnp.concatenate(parts, axis=1)
    return y * lax.rsqrt(ssq * (1.0 / HEAD_DIM) + EPS) * gain_row


def _norm_cols(yt, gain_col):
    y3 = yt.reshape(H_GROUP, HEAD_DIM, yt.shape[1])
    ssq = jnp.sum(y3 * y3, axis=1, keepdims=True)
    y3 = y3 * lax.rsqrt(ssq * (1.0 / HEAD_DIM) + EPS) * gain_col
    return y3.reshape(yt.shape)


def _proj_kernel(x_ref, *refs):
    _project(x_ref[...], *refs)


def _out_proj_kernel(x_ref, a_ref, b_ref, c_ref, wo_ref, *refs):
    *proj_refs, xo_ref = refs
    tm = x_ref.shape[0]
    parts = []
    for rows in (slice(0, tm // 2), slice(tm // 2, tm)):
        x = x_ref[rows] + (_dot(a_ref[rows], wo_ref[0]) + _dot(b_ref[rows], wo_ref[1])
                           + _dot(c_ref[rows], wo_ref[2]))
        xo_ref[rows] = x
        parts.append(x)
    _project(jnp.concatenate(parts, axis=0), *proj_refs)


def _project(x, ng_ref, wn_ref, wt_ref, wf_ref, grow_ref, gcol_ref, bd_ref,
             qt_ref, k_ref, vt_ref, g_ref, f_ref):
    h = x * lax.rsqrt(jnp.mean(x * x, axis=-1, keepdims=True) + EPS) * ng_ref[...]
    hb = h.astype(MM_DTYPE)
    bd = bd_ref[...]
    dt = qt_ref.dtype
    for grp in range(N_GROUPS):
        qt = _dot_nt(wt_ref[grp], hb)
        k = _dot(hb, wn_ref[grp])
        if grp > 0:
            qt = _norm_cols(qt, gcol_ref[grp - 1])
            k = _norm_rows(k, grow_ref[grp - 1], bd)
        qt_ref[grp] = (qt * SCALE).astype(dt)
        k_ref[grp] = k.astype(dt)
        vt_ref[grp] = _dot_nt(wt_ref[N_GROUPS + grp], hb).astype(dt)
        g_ref[grp] = _dot(hb, wn_ref[N_GROUPS + grp])
    f_ref[...] = _dot(hb, wf_ref[...].astype(MM_DTYPE))


def _projection(x, proj_weights, layer, prev=None):
    b, s, d = x.shape
    tm = min(TM_PROJ, s)
    const = dict(pipeline_mode=pl.Buffered(1))
    row_spec = pl.BlockSpec((None, tm, d), lambda bi, i: (bi, i, 0))
    nat_spec = pl.BlockSpec((N_GROUPS, None, tm, W_GROUP), lambda bi, i: (0, bi, i, 0))
    tr_spec = pl.BlockSpec((N_GROUPS, None, W_GROUP, tm), lambda bi, i: (0, bi, 0, i))
    nat_shape = jax.ShapeDtypeStruct((N_GROUPS, b, s, W_GROUP), MM_DTYPE)
    tr_shape = jax.ShapeDtypeStruct((N_GROUPS, b, W_GROUP, s), MM_DTYPE)
    prev_specs, extra_out_specs, extra_out_shapes = [], [], []
    if prev is not None:
        mix = pl.BlockSpec((None, tm, W_GROUP), lambda bi, i: (bi, i, 0))
        prev_specs = [mix, mix, mix,
                      pl.BlockSpec((None, N_GROUPS, W_GROUP, d),
                                   lambda bi, i: (layer - 1, 0, 0, 0), **const)]
        extra_out_specs = [row_spec]
        extra_out_shapes = [jax.ShapeDtypeStruct(x.shape, x.dtype)]
    outs = pl.pallas_call(
        _proj_kernel if prev is None else _out_proj_kernel,
        grid=(b, s // tm),
        in_specs=[row_spec] + prev_specs + [
            pl.BlockSpec((1, d), lambda bi, i: (0, 0)),
            pl.BlockSpec((None, 2 * N_GROUPS, d, W_GROUP), lambda bi, i: (layer, 0, 0, 0), **const),
            pl.BlockSpec((None, 2 * N_GROUPS, W_GROUP, d), lambda bi, i: (layer, 0, 0, 0), **const),
            pl.BlockSpec((d, F_PAD), lambda bi, i: (0, 0), **const),
            pl.BlockSpec((2, 1, W_GROUP), lambda bi, i: (0, 0, 0)),
            pl.BlockSpec((2, HEAD_DIM, 1), lambda bi, i: (0, 0, 0)),
            pl.BlockSpec((MXU_DIM, MXU_DIM), lambda bi, i: (0, 0)),
        ],
        out_specs=[
            tr_spec, nat_spec, tr_spec,
            pl.BlockSpec((N_GROUPS, None, tm, W_GROUP), lambda bi, i: (0, bi, i, 0)),
            pl.BlockSpec((None, tm, F_PAD), lambda bi, i: (bi, i, 0)),
        ] + extra_out_specs,
        out_shape=[
            tr_shape, nat_shape, tr_shape,
            jax.ShapeDtypeStruct((N_GROUPS, b, s, W_GROUP), F32),
            jax.ShapeDtypeStruct((b, s, F_PAD), F32),
        ] + extra_out_shapes,
        compiler_params=pltpu.CompilerParams(
            dimension_semantics=("arbitrary", "arbitrary"), vmem_limit_bytes=VMEM_LIMIT),
        name="proj" if prev is None else "out_proj_proj",
    )(x, *(prev or ()), *proj_weights)
    return outs if prev is None else (outs[-1], *outs[:-1])


def _cum_kernel(f_ref, b_ref, l_ref, o_ref):
    z = f_ref[...] + b_ref[...]
    lf = jnp.minimum(z, 0.0) - jnp.log1p(jnp.exp(-jnp.abs(z)))
    low = l_ref[...]
    carry = jnp.zeros((1, F_PAD), F32)
    for c in range(lf.shape[0] // CUM_BLK):
        rows = slice(c * CUM_BLK, (c + 1) * CUM_BLK)
        hi, mid, lo = _split3(lf[rows])
        cs = (_dot(low, hi) + _dot(low, mid)) + _dot(low, lo) + carry
        o_ref[rows] = cs
        carry = cs[CUM_BLK - 1:CUM_BLK, :]


def _cum_forget(f, bias_row, low):
    b, s, _ = f.shape
    return pl.pallas_call(
        _cum_kernel,
        grid=(b,),
        in_specs=[
            pl.BlockSpec((None, s, F_PAD), lambda bi: (bi, 0, 0)),
            pl.BlockSpec((1, F_PAD), lambda bi: (0, 0)),
            pl.BlockSpec((CUM_BLK, CUM_BLK), lambda bi: (0, 0)),
        ],
        out_specs=pl.BlockSpec((None, s, F_PAD), lambda bi: (bi, 0, 0)),
        out_shape=jax.ShapeDtypeStruct((b, s, F_PAD), F32),
        compiler_params=pltpu.CompilerParams(dimension_semantics=("arbitrary",)),
        name="cum_forget",
    )(f, bias_row, low)


def _gated_t(g, ot0, ot1):
    o = jnp.concatenate([ot0, ot1], axis=0).T
    return o * (g * jax.nn.sigmoid(g))


def _out_tile(b, p, i):
    return (b, i, p)


def _t_specs(grp, tq, s):
    return [
        pl.BlockSpec((None, None, LANES, tq), lambda b, p, i: (grp, b, p, i)),
        pl.BlockSpec((None, None, s, LANES), lambda b, p, i: (grp, b, 0, p)),
        pl.BlockSpec((None, None, LANES, s), lambda b, p, i: (grp, b, p, 0)),
        pl.BlockSpec((None, None, tq, LANES), lambda b, p, i: (grp, b, i, p)),
    ]


_ATT_PARAMS = pltpu.CompilerParams(
    dimension_semantics=("arbitrary", "arbitrary", "arbitrary"), vmem_limit_bytes=VMEM_LIMIT)


def _neg_abs(x):
    bits = lax.bitcast_convert_type(x, jnp.uint32) | jnp.uint32(0x80000000)
    return lax.bitcast_convert_type(bits, F32)


def _run_skewed(units):
    pending = list(units)
    active = []
    while pending or active:
        if pending:
            active.append(pending.pop(0))
        for g in list(active):
            try:
                next(g)
            except StopIteration:
                active.remove(g)


def _masked_heads_t(qt, other=None):
    if other is None:
        other = jnp.zeros((HEAD_DIM, qt.shape[1]), qt.dtype)
    return [jnp.concatenate([qt[:HEAD_DIM], other], axis=0),
            jnp.concatenate([other, qt[HEAD_DIM:]], axis=0)]


def _sb_kernel(qt_ref, k_ref, vt_ref, g_ref, u_ref, o_ref, acc_ref, carry_ref):
    t = qt_ref.shape[1]
    tk = u_ref.shape[0]
    n_sub = t // tk
    qm = _masked_heads_t(qt_ref[...])
    u = u_ref[...]
    acc_ref[...] = jnp.zeros_like(acc_ref)
    carry_ref[...] = jnp.zeros_like(carry_ref)

    def unit(j, kb, qc, diag):
        cols = slice(qc * tk, (qc + 1) * tk)
        static = isinstance(kb, int)
        k0 = kb * tk if static else pl.multiple_of(jnp.maximum(kb, 0) * tk, tk)
        z = _dot(k_ref[pl.ds(k0, tk), :], qm[j][:, cols])
        for _ in range(SB_SKEW):
            yield
        sp = jnp.maximum(z, 0.0) + jnp.log(1.0 + jnp.exp(_neg_abs(z)))
        if diag:
            strict = (lax.broadcasted_iota(jnp.int32, z.shape, 0)
                      < lax.broadcasted_iota(jnp.int32, z.shape, 1))
            sp = jnp.where(strict, sp, 0.0)
        spb = sp.astype(MM_DTYPE)
        yield
        incl = _dot(u, spb)
        for _ in range(SB_SKEW):
            yield
        w = jnp.exp(z + incl)
        if diag:
            w = jnp.where(strict, w, 0.0)
        wb = w.astype(MM_DTYPE)
        yield
        pv = _dot(vt_ref[j * HEAD_DIM:(j + 1) * HEAD_DIM, pl.ds(k0, tk)], wb)
        carry = carry_ref[j, :, cols]
        scale = jnp.exp(carry)
        step = incl[:1]
        if not static:
            scale = jnp.where(kb >= 0, scale, 0.0)
            step = jnp.where(kb >= 0, step, 0.0)
        acc_ref[j, :, cols] += scale * pv
        carry_ref[j, :, cols] = carry + step

    def diagonal(d, first_qc, masked=False):
        return [unit(j, qc - d, qc, masked) for qc in range(first_qc, n_sub) for j in range(2)]

    def live(d):
        lane = lax.broadcasted_iota(jnp.int32, carry_ref.shape, 2)
        return jnp.max(jnp.where(lane >= d * tk, carry_ref[...], NEG)) >= SB_DEAD_LOG

    def store(rows):
        o_ref[rows] = _gated_t(g_ref[rows], acc_ref[0, :, rows], acc_ref[1, :, rows]).astype(o_ref.dtype)

    def finish(qc):
        for _ in range(2 * SB_SKEW + 3):
            yield
        store(slice(qc * tk, (qc + 1) * tk))

    n_static = min(SB_STATIC_DIAGS, n_sub)
    static_units = []
    for d in range(n_static):
        for qc in range(d, n_sub):
            static_units += [unit(j, qc - d, qc, d == 0) for j in range(2)]
            if d == min(qc, n_static - 1):
                static_units.append(finish(qc))
    _run_skewed(static_units)

    def body(state):
        d, _ = state
        _run_skewed(diagonal(d, n_static))
        return d + 1, live(d + 1)

    if n_static < n_sub:
        d_end, _ = lax.while_loop(lambda st: (st[0] < n_sub) & st[1], body,
                                  (n_static, live(n_static)))
        pl.when(d_end > n_static)(lambda: store(slice(n_static * tk, t)))


def _sb_attention(qt, k, vt, g, u):
    _, b, _, s = qt.shape
    t = s
    tk = u.shape[0]
    return pl.pallas_call(
        _sb_kernel,
        grid=(b, W_GROUP // LANES, s // t),
        in_specs=_t_specs(0, t, s) + [pl.BlockSpec((tk, tk), lambda b_, p, i: (0, 0))],
        out_specs=pl.BlockSpec((None, t, LANES), _out_tile),
        out_shape=jax.ShapeDtypeStruct((b, s, W_GROUP), MM_DTYPE),
        scratch_shapes=[pltpu.VMEM((2, HEAD_DIM, t), F32), pltpu.VMEM((2, 1, t), F32)],
        compiler_params=_ATT_PARAMS,
        name="sb_attention",
    )(qt, k, vt, g, u)


def _chunk_kernel(qt_ref, k_ref, vt_ref, g_ref, bias_ref, o_ref, kpad, vaug, ot_ref):
    i = pl.program_id(2)
    ts = qt_ref.shape[1]
    win, tq = bias_ref.shape[1:]
    s = k_ref.shape[0]
    n_ones = vaug.shape[1] - HEAD_DIM

    @pl.when(i == 0)
    def _():
        kpad[:LEFT] = jnp.zeros((LEFT, LANES), kpad.dtype)
        kpad[LEFT:] = k_ref[...]
        vt = vt_ref[...].astype(F32)
        for j in range(2):
            vaug[j, :, :LEFT] = jnp.zeros((vaug.shape[1], LEFT), vaug.dtype)
            vaug[j, :, LEFT:] = jnp.concatenate(
                [vt[j * HEAD_DIM:(j + 1) * HEAD_DIM], jnp.ones((n_ones, s), F32)],
                axis=0).astype(vaug.dtype)

    qm = _masked_heads_t(qt_ref[...])

    def unit(j, r, first_step):
        cols = slice(r * tq, (r + 1) * tq)
        r0 = pl.multiple_of(i * ts + r * tq, tq)
        zs = []
        for kb in range(win // tq):
            k0 = pl.multiple_of(r0 + kb * tq, tq)
            z = _dot(kpad[pl.ds(k0, tq), :], qm[j][:, cols]) + bias_ref[j, kb * tq:(kb + 1) * tq]
            lo_key = LEFT - r * tq - kb * tq
            if first_step and lo_key > 0:
                key = lax.broadcasted_iota(jnp.int32, z.shape, 0)
                z = jnp.where(key >= lo_key, z, NEG)
            zs.append(z)
            yield
        m = zs[0].max(axis=0, keepdims=True)
        for z in zs[1:]:
            m = jnp.maximum(m, z.max(axis=0, keepdims=True))
        ps = []
        for z in zs:
            ps.append(jnp.exp(z - m).astype(MM_DTYPE))
            yield
        pv = None
        for kb, p in enumerate(ps):
            k0 = pl.multiple_of(r0 + kb * tq, tq)
            term = _dot(vaug[j, :, pl.ds(k0, tq)], p)
            pv = term if pv is None else pv + term
        ot_ref[j * HEAD_DIM:(j + 1) * HEAD_DIM, cols] = pv[:HEAD_DIM] / pv[HEAD_DIM:HEAD_DIM + 1]

    n_stages = 2 * (win // tq) + 1

    def finish(r):
        for _ in range(n_stages):
            yield
        cols = slice(r * tq, (r + 1) * tq)
        o_ref[cols] = _gated_t(g_ref[cols], ot_ref[:HEAD_DIM, cols],
                               ot_ref[HEAD_DIM:, cols]).astype(o_ref.dtype)

    def run(first_step):
        _run_skewed([un for r in range(ts // tq)
                     for un in (unit(0, r, first_step), unit(1, r, first_step), finish(r))])

    if s == ts:
        run(True)
    else:
        pl.when(i == 0)(lambda: run(True))
        pl.when(i > 0)(lambda: run(False))


def _chunk_attention(qt, k, vt, g, bias):
    _, b, _, s = qt.shape
    ts = min(TS_CHUNK, s)
    return pl.pallas_call(
        _chunk_kernel,
        grid=(b, W_GROUP // LANES, s // ts),
        in_specs=_t_specs(1, ts, s) + [
            pl.BlockSpec((2, WIN_CHUNK, TQ_CHUNK), lambda b_, p, i: (p, 0, 0))],
        out_specs=pl.BlockSpec((None, ts, LANES), _out_tile),
        out_shape=jax.ShapeDtypeStruct((b, s, W_GROUP), MM_DTYPE),
        scratch_shapes=[pltpu.VMEM((LEFT + s, LANES), MM_DTYPE),
                        pltpu.VMEM((2, FOX_V_ROWS, LEFT + s), MM_DTYPE),
                        pltpu.VMEM((LANES, ts), F32)],
        compiler_params=_ATT_PARAMS,
        name="chunk_attention",
    )(qt, k, vt, g, bias)


N_DIAG = WIN_CHUNK + TQ_CHUNK


def _bias_kernel(v_ref, o_ref):
    x = jnp.broadcast_to(v_ref[...], (WIN_CHUNK, N_DIAG))
    y = pltpu.roll(x, 1, 1, stride=1, stride_axis=0)
    t = y[:, WIN_CHUNK:]
    c = lax.broadcasted_iota(jnp.int32, t.shape, 0)
    r = lax.broadcasted_iota(jnp.int32, t.shape, 1)
    band = c - (r - (r & (CHUNK - 1)))
    o_ref[...] = jnp.where((band >= 0) & (band < LEFT + CHUNK), t, NEG)


def _chunk_bias_table(rel_bias):
    kk = np.arange(N_DIAG - 1)
    rel = kk - (WIN_CHUNK - 1) + LEFT
    vec = rel_bias[:, np.clip(rel, -REL_CLIP, REL_CLIP) + REL_CLIP].astype(F32)
    vec = jnp.pad(vec, ((0, 0), (0, 1)))[:, None, :]
    h = vec.shape[0]
    return pl.pallas_call(
        _bias_kernel,
        grid=(h,),
        in_specs=[pl.BlockSpec((None, 1, N_DIAG), lambda i: (i, 0, 0))],
        out_specs=pl.BlockSpec((None, WIN_CHUNK, TQ_CHUNK), lambda i: (i, 0, 0)),
        out_shape=jax.ShapeDtypeStruct((h, WIN_CHUNK, TQ_CHUNK), F32),
        compiler_params=pltpu.CompilerParams(dimension_semantics=("arbitrary",)),
        name="chunk_bias",
    )(vec)


def _fox_kernel(qt_ref, k_ref, vt_ref, g_ref, c_ref, o_ref, kaug, vaug, m_ref, acc_ref):
    i = pl.program_id(2)
    p = pl.program_id(1)
    t = qt_ref.shape[1]
    aug0 = [HEAD_DIM * (1 - j) for j in range(2)]
    tk = vaug.shape[3]
    n_ones = vaug.shape[2] - HEAD_DIM

    @pl.when(i == 0)
    def _():
        k = k_ref[...].astype(F32)
        vt = vt_ref[...].astype(F32)
        pieces = jnp.concatenate(_split3(-c_ref[...]), axis=1)
        row = lax.broadcasted_iota(jnp.int32, (3 * LANES, 2 * LANES), 0)
        col = lax.broadcasted_iota(jnp.int32, (3 * LANES, 2 * LANES), 1)
        place = None
        for j in range(2):
            for c in range(3):
                hit = (row == c * LANES + 2 * p + j) & (col == j * LANES + aug0[j] + c)
                place = hit if place is None else place | hit
        aug = _dot(pieces, jnp.where(place, 1.0, 0.0).astype(MM_DTYPE))
        for j in range(2):
            kaug[j] = jnp.where(_head_mask(k.shape, j, 1), k,
                                aug[:, j * LANES:(j + 1) * LANES]).astype(kaug.dtype)
            va = jnp.concatenate(
                [vt[j * HEAD_DIM:(j + 1) * HEAD_DIM], jnp.ones((n_ones, vt.shape[1]), F32)],
                axis=0).astype(vaug.dtype)
            for c in range(vaug.shape[1]):
                vaug[j, c] = va[:, c * tk:(c + 1) * tk]

    row = lax.broadcasted_iota(jnp.int32, (HEAD_DIM, LANES), 0)
    ones3 = jnp.tile(jnp.where(row < 3, 1.0, 0.0).astype(MM_DTYPE), (1, t // LANES))
    qa = _masked_heads_t(qt_ref[...], ones3)
    m_ref[...] = jnp.full(m_ref.shape, NEG, F32)
    acc_ref[...] = jnp.zeros_like(acc_ref)
    n_sub = t // tk

    def unit(j, kb, qc, diag):
        cols = slice(qc * tk, (qc + 1) * tk)
        k0 = pl.multiple_of(kb * tk, tk)
        z = _dot(kaug[j, pl.ds(k0, tk), :], qa[j][:, cols])
        for _ in range(FOX_SKEW):
            yield
        if diag:
            causal = (lax.broadcasted_iota(jnp.int32, z.shape, 0)
                      <= lax.broadcasted_iota(jnp.int32, z.shape, 1))
            z = jnp.where(causal, z, NEG)
        m = m_ref[j, qc]
        m_new = jnp.maximum(m, jnp.max(z, axis=0, keepdims=True))
        pr = jnp.exp(z - m_new).astype(MM_DTYPE)
        alpha = jnp.exp(m - m_new)
        m_ref[j, qc] = m_new
        yield
        pv = _dot(vaug[j, kb], pr)
        acc_ref[j, qc] = alpha * acc_ref[j, qc] + pv

    def units(first_kb, n_blocks, diag):
        out = []
        for c in range(n_blocks):
            for qc in range(c if diag else 0, n_sub):
                for j in range(2):
                    out.append(unit(j, first_kb + c, qc, diag and qc == c))
        return out

    trip = min(FOX_KEYS_PER_TRIP, t) // tk

    def body(kj, _):
        _run_skewed(units(kj * trip, trip, False))
        return 0

    if kaug.shape[1] > t:
        lax.fori_loop(0, i * (n_sub // trip), body, 0)
    _run_skewed(units(i * n_sub, n_sub, True))
    outs = [jnp.concatenate([acc_ref[j, qc, :HEAD_DIM] / acc_ref[j, qc, HEAD_DIM:HEAD_DIM + 1]
                             for qc in range(n_sub)], axis=1) for j in range(2)]
    o_ref[...] = _gated_t(g_ref[...], outs[0], outs[1]).astype(o_ref.dtype)


def _fox_attention(qt, k, vt, g, cum):
    _, b, _, s = qt.shape
    t = min(TQ_ATT, s)
    return pl.pallas_call(
        _fox_kernel,
        grid=(b, W_GROUP // LANES, s // t),
        in_specs=_t_specs(2, t, s) + [
            pl.BlockSpec((None, s, F_PAD), lambda b_, p, i: (b_, 0, 0))],
        out_specs=pl.BlockSpec((None, t, LANES), _out_tile),
        out_shape=jax.ShapeDtypeStruct((b, s, W_GROUP), MM_DTYPE),
        scratch_shapes=[pltpu.VMEM((2, s, LANES), MM_DTYPE),
                        pltpu.VMEM((2, s // TK_SB, FOX_V_ROWS, TK_SB), MM_DTYPE),
                        pltpu.VMEM((2, t // TK_SB, 1, TK_SB), F32),
                        pltpu.VMEM((2, t // TK_SB, FOX_V_ROWS, TK_SB), F32)],
        compiler_params=_ATT_PARAMS,
        name="fox_attention",
    )(qt, k, vt, g, cum)


def _out_kernel(x_ref, a_ref, b_ref, c_ref, w_ref, o_ref):
    acc = _dot(a_ref[...], w_ref[0]) + _dot(b_ref[...], w_ref[1]) + _dot(c_ref[...], w_ref[2])
    o_ref[...] = x_ref[...] + acc


def _out_projection(x, ma, mb, mc, w, layer):
    b, s, d = x.shape
    tm = min(TM_OUT, s)
    mix = pl.BlockSpec((None, tm, W_GROUP), lambda bi, i: (bi, i, 0))
    return pl.pallas_call(
        _out_kernel,
        grid=(b, s // tm),
        in_specs=[pl.BlockSpec((None, tm, d), lambda bi, i: (bi, i, 0)), mix, mix, mix,
                  pl.BlockSpec((None, N_GROUPS, W_GROUP, d), lambda bi, i: (layer, 0, 0, 0))],
        out_specs=pl.BlockSpec((None, tm, d), lambda bi, i: (bi, i, 0)),
        out_shape=jax.ShapeDtypeStruct(x.shape, x.dtype),
        compiler_params=pltpu.CompilerParams(
            dimension_semantics=("arbitrary", "arbitrary"), vmem_limit_bytes=VMEM_LIMIT),
        name="out_proj",
    )(x, ma, mb, mc, w)


def _constants():
    r = np.arange(MXU_DIM)
    bd = (r[:, None] // HEAD_DIM == r[None, :] // HEAD_DIM).astype(np.float32)
    r = np.arange(TK_SB)
    u_sb = -(r[None, :] >= r[:, None]).astype(np.float32)
    r = np.arange(CUM_BLK)
    low = (r[None, :] <= r[:, None]).astype(np.float32)
    return (jnp.asarray(bd, MM_DTYPE), jnp.asarray(u_sb, MM_DTYPE), jnp.asarray(low, MM_DTYPE))


def _relayout_kernel(w_ref, o_ref, *, transpose):
    w = w_ref[...]
    o_ref[...] = (w.T if transpose else w).astype(o_ref.dtype)


def _relayout_weights(w_in, first_col, transpose):
    depth, d, _ = w_in.shape
    blk = (W_GROUP, d) if transpose else (d, W_GROUP)
    return pl.pallas_call(
        functools.partial(_relayout_kernel, transpose=transpose),
        grid=(depth, 2 * N_GROUPS),
        in_specs=[pl.BlockSpec((None, d, W_GROUP),
                               lambda l, j: (l, 0, (j % N_GROUPS) * 4 + first_col
                                             + 2 * (j // N_GROUPS)))],
        out_specs=pl.BlockSpec((None, None) + blk, lambda l, j: (l, j, 0, 0)),
        out_shape=jax.ShapeDtypeStruct((depth, 2 * N_GROUPS) + blk, MM_DTYPE),
        compiler_params=pltpu.CompilerParams(dimension_semantics=("arbitrary", "arbitrary")),
        name="w_tr" if transpose else "w_nat",
    )(w_in)


def _cast_w_out(w_out):
    depth, _, d = w_out.shape
    return pl.pallas_call(
        functools.partial(_relayout_kernel, transpose=False),
        grid=(depth, N_GROUPS),
        in_specs=[pl.BlockSpec((None, W_GROUP, d), lambda l, j: (l, j, 0))],
        out_specs=pl.BlockSpec((None, None, W_GROUP, d), lambda l, j: (l, j, 0, 0)),
        out_shape=jax.ShapeDtypeStruct((depth, N_GROUPS, W_GROUP, d), MM_DTYPE),
        compiler_params=pltpu.CompilerParams(dimension_semantics=("arbitrary", "arbitrary")),
        name="w_out_cast",
    )(w_out)


def _proj_weights(ng, w_nat, w_tr, w_f_cols, qn_ch, kn_ch, qn_fox, kn_fox, bd):
    w_f = jnp.pad(w_f_cols, ((0, 0), (0, F_PAD - H_GROUP)))
    grow = jnp.stack([jnp.tile(kn_ch, H_GROUP), jnp.tile(kn_fox, H_GROUP)])[:, None, :].astype(F32)
    gcol = jnp.stack([qn_ch, qn_fox])[:, :, None].astype(F32)
    return (ng[None, :], w_nat, w_tr, w_f, grow, gcol, bd)


def _mixers(projected, b_forget, rel_bias, u_sb, low):
    qt, k, vt, g, f = projected
    bias_row = jnp.pad(b_forget.astype(F32), (0, F_PAD - H_GROUP))[None, :]
    cum = _cum_forget(f, bias_row, low)
    m_sb = _sb_attention(qt, k, vt, g, u_sb)
    m_ch = _chunk_attention(qt, k, vt, g, _chunk_bias_table(rel_bias))
    m_fx = _fox_attention(qt, k, vt, g, cum)
    return m_sb, m_ch, m_fx


def kernel(x, norm_g, w_in, b_forget, q_norm_ch, k_norm_ch, q_norm_fox, k_norm_fox, rel_bias, w_out):
    bd, u_sb, low = _constants()
    depth, d = norm_g.shape
    w_nat = _relayout_weights(w_in, 1, transpose=False)
    w_tr = _relayout_weights(w_in, 0, transpose=True)
    weights = [_proj_weights(norm_g[l], w_nat, w_tr, w_in[l, :, N_GROUPS * 4 * W_GROUP:],
                             q_norm_ch[l], k_norm_ch[l], q_norm_fox[l], k_norm_fox[l], bd)
               for l in range(depth)]
    w_o = _cast_w_out(w_out)
    projected = _projection(x, weights[0], 0)
    for l in range(depth):
        mixed = _mixers(projected, b_forget[l], rel_bias[l], u_sb, low)
        if l + 1 < depth:
            x, *projected = _projection(x, weights[l + 1], l + 1, prev=(*mixed, w_o))
        else:
            x = _out_projection(x, *mixed, w_o, l)
    return x
```
